```python
import jax, jax.numpy as jnp
from jax import lax
import numpy as np

D_MODEL = 1024
BATCH = 4
SEQ = 4096
DEPTH = 2

CHUNK = 64
N_MIXERS = 2
N_CONV_LAYERS = (DEPTH + 1) // 2
N_GDN_LAYERS = DEPTH // 2
CONV_WIDTH = 3
GDN_HEADS = 8
GDN_HEAD_DIM = D_MODEL // GDN_HEADS
GDN_QK_DIM = GDN_HEADS * GDN_HEAD_DIM
GDN_V_DIM = GDN_HEADS * GDN_HEAD_DIM
GDN_CONV_WIDTH = 4
GDN_IN_DIM = 2 * GDN_QK_DIM + 2 * GDN_V_DIM + 2 * GDN_HEADS
N_GROUPS = 8
EXPERTS_PER_GROUP = 8
N_EXPERTS = N_GROUPS * EXPERTS_PER_GROUP
TOP_K = 2
D_EXPERT = D_MODEL // 4
D_SHARED = D_MODEL // 2
MOE_BLOCK = 128
NORM_EPS = 1e-6

kernel_name = "hybrid_conv_gdn_hier_moe_adaln"


def rmsnorm(x, w):
    xf = x.astype(jnp.float32)
    y = xf * lax.rsqrt(jnp.mean(xf * xf, axis=-1, keepdims=True) + NORM_EPS)
    return (y * w.astype(jnp.float32)).astype(x.dtype)


def l2norm(x):
    xf = x.astype(jnp.float32)
    return xf * lax.rsqrt(jnp.sum(xf * xf, axis=-1, keepdims=True) + NORM_EPS)


def causal_dwconv(x, w):
    K = w.shape[0]
    T = x.shape[1]
    xp = jnp.pad(x, ((0, 0), (K - 1, 0), (0, 0)))
    y = xp[:, 0:T] * w[0]
    for k in range(1, K):
        y = y + xp[:, k:k + T] * w[k]
    return y


def short_conv_mixer(h, w_in, conv_w, w_out):
    bcx = h @ w_in
    b, cg, xv = jnp.split(bcx, 3, axis=-1)
    y = b * causal_dwconv(cg * xv, conv_w)
    return y @ w_out


def chunk_gated_delta_rule(q, k, v, g, beta):
    out_dtype = v.dtype
    B, T, H, Dk = q.shape
    Dv = v.shape[-1]
    n = T // CHUNK

    def blocks(a):
        a = a.astype(jnp.float32).reshape((B, n, CHUNK, H) + a.shape[3:])
        return jnp.moveaxis(a, 3, 1)

    q, k, v, g, beta = (blocks(a) for a in (q, k, v, g, beta))
    gc = jnp.cumsum(g, axis=-1)
    idx = jnp.arange(CHUNK)
    causal = idx[:, None] >= idx[None, :]
    strict = idx[:, None] > idx[None, :]
    diff = gc[..., :, None] - gc[..., None, :]
    decay = jnp.exp(jnp.where(causal, diff, -jnp.inf))
    kb = k * beta[..., None]
    lower = jnp.where(strict, jnp.einsum('bhncd,bhnsd->bhncs', kb, k) * decay, 0.0)
    rhs = jnp.concatenate([v * beta[..., None], kb * jnp.exp(gc)[..., None]], axis=-1)
    sol = lax.linalg.triangular_solve(lower, rhs, left_side=True, lower=True,
                                      unit_diagonal=True)
    u, w = sol[..., :Dv], sol[..., Dv:]
    attn = jnp.einsum('bhncd,bhnsd->bhncs', q, k) * decay
    q_dec = q * jnp.exp(gc)[..., None]
    g_last = gc[..., -1]
    k_dec = k * jnp.exp(g_last[..., None] - gc)[..., None]

    def step(S, inp):
        u_n, w_n, qd_n, kd_n, a_n, gl_n = inp
        v_new = u_n - jnp.einsum('bhcd,bhde->bhce', w_n, S)
        o_n = (jnp.einsum('bhcd,bhde->bhce', qd_n, S)
               + jnp.einsum('bhcs,bhse->bhce', a_n, v_new))
        S = S * jnp.exp(gl_n)[..., None, None] + jnp.einsum('bhcd,bhce->bhde', kd_n, v_new)
        return S, o_n

    xs = tuple(jnp.moveaxis(a, 2, 0) for a in (u, w, q_dec, k_dec, attn, g_last))
    S0 = jnp.zeros((B, H, Dk, Dv), jnp.float32)
    _, o = lax.scan(step, S0, xs)
    o = jnp.transpose(o, (1, 0, 3, 2, 4)).reshape(B, T, H, Dv)
    return o.astype(out_dtype)


def gated_deltanet_mixer(h, w_in, conv_w, a_log, dt_bias, norm_w, w_out):
    B, T, _ = h.shape
    H, Dh = GDN_HEADS, GDN_HEAD_DIM
    proj = h @ w_in
    s1 = 2 * GDN_QK_DIM + GDN_V_DIM
    s2 = s1 + GDN_V_DIM
    qkv = proj[..., :s1]
    z = proj[..., s1:s2]
    b = proj[..., s2:s2 + H]
    a = proj[..., s2 + H:]
    qkv = jax.nn.silu(causal_dwconv(qkv, conv_w))
    q = qkv[..., :GDN_QK_DIM].reshape(B, T, H, Dh)
    k = qkv[..., GDN_QK_DIM:2 * GDN_QK_DIM].reshape(B, T, H, Dh)
    v = qkv[..., 2 * GDN_QK_DIM:].reshape(B, T, H, Dh)
    q = l2norm(q) * (Dh ** -0.5)
    k = l2norm(k)
    beta = jax.nn.sigmoid(b.astype(jnp.float32))
    g = -jnp.exp(a_log.astype(jnp.float32)) * jax.nn.softplus(
        a.astype(jnp.float32) + dt_bias.astype(jnp.float32))
    o = chunk_gated_delta_rule(q, k, v, g, beta)
    o = rmsnorm(o, norm_w) * jax.nn.silu(z.reshape(B, T, H, Dh))
    return o.reshape(B, T, GDN_V_DIM) @ w_out


def hier_moe(h, w_rg, b_rg, w_re, b_re, w_gate, w_up, w_down, ws_gate, ws_up, ws_down, w_sg):
    B, T, D = h.shape
    xt = h.reshape(-1, D)
    N = xt.shape[0]
    gl = (xt @ w_rg + b_rg).astype(jnp.float32)
    gp = jax.nn.softmax(gl, axis=-1)
    g_sel = jnp.argmax(gl, axis=-1)
    p_sel = jnp.take_along_axis(gp, g_sel[:, None], axis=-1)[:, 0]
    el = (xt @ w_re + b_re).astype(jnp.float32).reshape(N, N_GROUPS, EXPERTS_PER_GROUP)
    el_sel = jnp.take_along_axis(el, g_sel[:, None, None], axis=1)[:, 0]
    top_v, top_i = lax.top_k(el_sel, TOP_K)
    gate = jax.nn.softmax(top_v, axis=-1) * p_sel[:, None]
    eid = g_sel[:, None] * EXPERTS_PER_GROUP + top_i
    S = N * TOP_K
    flat_e = eid.reshape(-1)
    flat_tok = jnp.repeat(jnp.arange(N), TOP_K)
    flat_gate = gate.reshape(-1)
    order = jnp.argsort(flat_e)
    e_sorted = flat_e[order]
    tok_sorted = flat_tok[order]
    gate_sorted = flat_gate[order].astype(xt.dtype)
    counts = jnp.bincount(flat_e, length=N_EXPERTS)
    starts = jnp.cumsum(counts) - counts
    padded = (counts + MOE_BLOCK - 1) // MOE_BLOCK * MOE_BLOCK
    pends = jnp.cumsum(padded)
    pstarts = pends - padded
    dest = pstarts[e_sorted] + (jnp.arange(S) - starts[e_sorted])
    n_blocks = -(-S // MOE_BLOCK) + N_EXPERTS
    P = n_blocks * MOE_BLOCK
    buf = jnp.zeros((P, D), xt.dtype).at[dest].set(xt[tok_sorted])
    block_e = jnp.minimum(
        jnp.searchsorted(pends, jnp.arange(n_blocks) * MOE_BLOCK, side='right'), N_EXPERTS - 1)

    def expert_block(args):
        xb, e = args
        hid = jax.nn.silu(xb @ w_gate[e]) * (xb @ w_up[e])
        return hid @ w_down[e]

    out = lax.map(expert_block, (buf.reshape(n_blocks, MOE_BLOCK, D), block_e)).reshape(P, D)
    y = jnp.zeros((N, D), xt.dtype).at[tok_sorted].add(out[dest] * gate_sorted[:, None])
    ys = (jax.nn.silu(xt @ ws_gate) * (xt @ ws_up)) @ ws_down
    y = y + jax.nn.sigmoid(xt @ w_sg) * ys
    return y.reshape(B, T, D)


def setup_inputs(seed: int = 0) -> dict:
    key = jax.random.key(seed)
    ks = jax.random.split(key, 32)
    D = D_MODEL
    nrm = jax.random.normal
    f32 = jnp.float32
    dt = jnp.exp(jax.random.uniform(ks[11], (N_GDN_LAYERS, GDN_HEADS), f32,
                                    np.log(1e-3), np.log(1e-1)))
    return {
        "x": nrm(ks[0], (BATCH, SEQ, D), f32),
        "c": nrm(ks[1], (BATCH, D), f32),
        "ada_w": nrm(ks[2], (DEPTH, D, 6 * D), f32) * (0.3 * D ** -0.5),
        "ada_b": nrm(ks[3], (DEPTH, 6 * D), f32) * 0.02,
        "norm_w": 1.0 + 0.05 * nrm(ks[4], (DEPTH, 4, D), f32),
        "conv_in_w": nrm(ks[5], (N_CONV_LAYERS, D, 3 * D), f32) * D ** -0.5,
        "conv_w": nrm(ks[6], (N_CONV_LAYERS, CONV_WIDTH, D), f32) * CONV_WIDTH ** -0.5,
        "conv_out_w": nrm(ks[7], (N_CONV_LAYERS, D, D), f32) * D ** -0.5,
        "gdn_in_w": nrm(ks[8], (N_GDN_LAYERS, D, GDN_IN_DIM), f32) * D ** -0.5,
        "gdn_conv_w": nrm(ks[9], (N_GDN_LAYERS, GDN_CONV_WIDTH, 2 * GDN_QK_DIM + GDN_V_DIM), f32) * 0.5,
        "gdn_a_log": jnp.log(jax.random.uniform(ks[10], (N_GDN_LAYERS, GDN_HEADS), f32, 1.0, 16.0)),
        "gdn_dt_bias": dt + jnp.log(-jnp.expm1(-dt)),
        "gdn_norm_w": 1.0 + 0.05 * nrm(ks[12], (N_GDN_LAYERS, GDN_HEAD_DIM), f32),
        "gdn_out_w": nrm(ks[13], (N_GDN_LAYERS, GDN_V_DIM, D), f32) * GDN_V_DIM ** -0.5,
        "moe_group_w": nrm(ks[14], (DEPTH, D, N_GROUPS), f32) * D ** -0.5,
        "moe_group_b": nrm(ks[15], (DEPTH, N_GROUPS), f32) * 0.01,
        "moe_expert_w": nrm(ks[16], (DEPTH, D, N_EXPERTS), f32) * D ** -0.5,
        "moe_expert_b": nrm(ks[17], (DEPTH, N_EXPERTS), f32) * 0.01,
        "moe_w_gate": nrm(ks[18], (DEPTH, N_EXPERTS, D, D_EXPERT), f32) * D ** -0.5,
        "moe_w_up": nrm(ks[19], (DEPTH, N_EXPERTS, D, D_EXPERT), f32) * D ** -0.5,
        "moe_w_down": nrm(ks[20], (DEPTH, N_EXPERTS, D_EXPERT, D), f32) * D_EXPERT ** -0.5,
        "shared_w_gate": nrm(ks[21], (DEPTH, D, D_SHARED), f32) * D ** -0.5,
        "shared_w_up": nrm(ks[22], (DEPTH, D, D_SHARED), f32) * D ** -0.5,
        "shared_w_down": nrm(ks[23], (DEPTH, D_SHARED, D), f32) * D_SHARED ** -0.5,
        "shared_gate_w": nrm(ks[24], (DEPTH, D, 1), f32) * D ** -0.5,
    }


def reference(x, c, ada_w, ada_b, norm_w, conv_in_w, conv_w, conv_out_w,
              gdn_in_w, gdn_conv_w, gdn_a_log, gdn_dt_bias, gdn_norm_w, gdn_out_w,
              moe_group_w, moe_group_b, moe_expert_w, moe_expert_b,
              moe_w_gate, moe_w_up, moe_w_down,
              shared_w_gate, shared_w_up, shared_w_down, shared_gate_w):
    cs = jax.nn.silu(c)
    for i in range(DEPTH):
        mod = (cs @ ada_w[i] + ada_b[i])[:, None, :]
        sh1, sc1, gt1, sh2, sc2, gt2 = jnp.split(mod, 6, axis=-1)
        h = rmsnorm(x, norm_w[i, 0]) * (1.0 + sc1) + sh1
        j = i // N_MIXERS
        if i % N_MIXERS == 0:
            y = short_conv_mixer(h, conv_in_w[j], conv_w[j], conv_out_w[j])
        else:
            y = gated_deltanet_mixer(h, gdn_in_w[j], gdn_conv_w[j], gdn_a_log[j],
                                     gdn_dt_bias[j], gdn_norm_w[j], gdn_out_w[j])
        x = x + gt1 * rmsnorm(y, norm_w[i, 1])
        h = rmsnorm(x, norm_w[i, 2]) * (1.0 + sc2) + sh2
        y = hier_moe(h, moe_group_w[i], moe_group_b[i], moe_expert_w[i], moe_expert_b[i],
                     moe_w_gate[i], moe_w_up[i], moe_w_down[i],
                     shared_w_gate[i], shared_w_up[i], shared_w_down[i], shared_gate_w[i])
        x = x + gt2 * rmsnorm(y, norm_w[i, 3])
    return x
```

```python
import functools

import jax
import jax.numpy as jnp
from jax import lax
from jax.experimental import pallas as pl
from jax.experimental.pallas import tpu as pltpu

F32 = jnp.float32
BF16 = jnp.bfloat16
I32 = jnp.int32
HIGHEST = lax.Precision.HIGHEST

D = 1024
BATCH = 4
SEQ = 4096
N_TOK = BATCH * SEQ
HEADS = 8
HEAD_DIM = 128
CHUNK = 64
CHUNK_SHIFT = 6
N_GROUPS = 8
N_EXPERTS = 64
D_EXPERT = 256
D_SHARED = 512
EPS = 1e-6
LANES = 128
SUBLANES = 8
EXPERT_LANE0 = N_GROUPS

MOE_BLOCK = 128
N_SLOTS = N_TOK * 2
N_BLOCKS = N_SLOTS // MOE_BLOCK + N_EXPERTS
P_ROWS = N_BLOCKS * MOE_BLOCK

TM_MIX = 512
TM_GDN = 256
TM_DMA = 256
TM_POS = 2048
VMEM_LIMIT = 56 * 1024 * 1024


def _cp(sem):
    return pltpu.CompilerParams(dimension_semantics=sem, vmem_limit_bytes=VMEM_LIMIT)


def _rms(x):
    return x * lax.rsqrt(jnp.mean(x * x, axis=-1, keepdims=True) + EPS)


def _silu(x):
    return x * jax.nn.sigmoid(x)


def _dot(a, b):
    return jnp.dot(a, b, preferred_element_type=F32)


def _ada_kernel(c_ref, w_ref, b_ref, o_ref):
    cs = _silu(c_ref[...])
    o_ref[0] = _dot(cs.astype(BF16), w_ref[0].astype(BF16)) + b_ref[0]


def _ada_mod(c, ada_w, ada_b):
    depth = ada_w.shape[0]
    tn = 1024
    c8 = jnp.pad(c, ((0, SUBLANES - BATCH), (0, 0)))
    mod = pl.pallas_call(
        _ada_kernel,
        grid=(depth, 6 * D // tn),
        in_specs=[
            pl.BlockSpec((SUBLANES, D), lambda l, j: (0, 0)),
            pl.BlockSpec((1, D, tn), lambda l, j: (l, 0, j)),
            pl.BlockSpec((1, 1, tn), lambda l, j: (l, 0, j)),
        ],
        out_specs=pl.BlockSpec((1, SUBLANES, tn), lambda l, j: (l, 0, j)),
        out_shape=jax.ShapeDtypeStruct((depth, SUBLANES, 6 * D), F32),
        compiler_params=_cp(("arbitrary", "arbitrary")),
        name="ada_mod",
    )(c8, ada_w, ada_b.reshape(depth, 1, 6 * D))
    mod = mod[:, :BATCH].reshape(depth, BATCH, 6, D)
    return jnp.pad(mod, ((0, 0), (0, 0), (0, 2), (0, 0)))


def _conv_mixer_kernel(x_ref, mod_ref, nw_ref, win_ref, cw_ref, wout_ref, o_ref, ubuf_ref, *, tiles_per_seq):
    tm = x_ref.shape[0]
    x = x_ref[...]
    mod = mod_ref[...]
    nw = nw_ref[...]
    h = _rms(x) * nw[0:1] * (1.0 + mod[1:2]) + mod[0:1]
    bcx = _dot(h.astype(BF16), win_ref[...])
    u = bcx[:, D:2 * D] * bcx[:, 2 * D:]

    @pl.when(pl.program_id(0) % tiles_per_seq == 0)
    def _():
        ubuf_ref[0:SUBLANES, :] = jnp.zeros((SUBLANES, D), F32)

    ubuf_ref[SUBLANES:SUBLANES + tm, :] = u
    cw = cw_ref[...]
    conv = (cw[0:1] * ubuf_ref[SUBLANES - 2:SUBLANES - 2 + tm, :]
            + cw[1:2] * ubuf_ref[SUBLANES - 1:SUBLANES - 1 + tm, :]
            + cw[2:3] * u)
    ubuf_ref[0:SUBLANES, :] = ubuf_ref[tm:tm + SUBLANES, :]
    y = _dot((bcx[:, :D] * conv).astype(BF16), wout_ref[...])
    o_ref[...] = x + mod[2:3] * (_rms(y) * nw[1:2])


def _conv_mixer(x2, mod_l, nw8, w_in, conv_w, w_out):
    tm = TM_MIX
    tiles_per_seq = SEQ // tm
    cw8 = jnp.pad(conv_w, ((0, SUBLANES - conv_w.shape[0]), (0, 0)))
    return pl.pallas_call(
        functools.partial(_conv_mixer_kernel, tiles_per_seq=tiles_per_seq),
        grid=(N_TOK // tm,),
        in_specs=[
            pl.BlockSpec((tm, D), lambda i: (i, 0)),
            pl.BlockSpec((None, SUBLANES, D), lambda i: (i // tiles_per_seq, 0, 0)),
            pl.BlockSpec((SUBLANES, D), lambda i: (0, 0)),
            pl.BlockSpec((D, 3 * D), lambda i: (0, 0)),
            pl.BlockSpec((SUBLANES, D), lambda i: (0, 0)),
            pl.BlockSpec((D, D), lambda i: (0, 0)),
        ],
        out_specs=pl.BlockSpec((tm, D), lambda i: (i, 0)),
        out_shape=jax.ShapeDtypeStruct((N_TOK, D), F32),
        scratch_shapes=[pltpu.VMEM((tm + SUBLANES, D), F32)],
        compiler_params=_cp(("arbitrary",)),
        name="conv_mixer",
    )(x2, mod_l, nw8, w_in.astype(BF16), cw8, w_out.astype(BF16))


def _lane_pick(lane, mask_val_pairs):
    out = jnp.zeros(lane.shape, F32)
    for idx, val in mask_val_pairs:
        out = jnp.where(lane == float(idx), val, out)
    return out


def _moe_pre_kernel(x_ref, mod_ref, nw_ref, wr_ref, br_ref, wgu_ref, wd_ref, wsg_ref,
                    h_ref, sh_ref, route_ref, cnt_ref, carry_ref):
    tm = x_ref.shape[0]
    x = x_ref[...]
    mod = mod_ref[...]
    nw = nw_ref[...]
    h = _rms(x) * nw[2:3] * (1.0 + mod[4:5]) + mod[3:4]
    h_ref[...] = h
    hb = h.astype(BF16)

    logits = jnp.dot(h, wr_ref[...], precision=HIGHEST, preferred_element_type=F32) + br_ref[0:1]
    lane_i = lax.broadcasted_iota(I32, (tm, LANES), 1)
    lane = lane_i.astype(F32)
    neg = jnp.float32(-jnp.inf)
    is_group = lane_i < N_GROUPS
    gl = jnp.where(is_group, logits, neg)
    gmax = jnp.max(gl, axis=-1, keepdims=True)
    gsel = jnp.min(jnp.where(gl == gmax, lane, float(LANES)), axis=-1, keepdims=True)
    psel = 1.0 / jnp.sum(jnp.where(is_group, jnp.exp(logits - gmax), 0.0), axis=-1, keepdims=True)
    lane_group = ((lane_i - EXPERT_LANE0) >> 3).astype(F32)
    in_group = (lane_i >= EXPERT_LANE0) & (lane_i < EXPERT_LANE0 + N_EXPERTS) & (lane_group == gsel)
    el = jnp.where(in_group, logits, neg)
    v1 = jnp.max(el, axis=-1, keepdims=True)
    i1 = jnp.min(jnp.where(el == v1, lane, float(LANES)), axis=-1, keepdims=True)
    el2 = jnp.where(lane == i1, neg, el)
    v2 = jnp.max(el2, axis=-1, keepdims=True)
    i2 = jnp.min(jnp.where(el2 == v2, lane, float(LANES)), axis=-1, keepdims=True)
    e2 = jnp.exp(v2 - v1)
    g1 = psel / (1.0 + e2)
    g2 = psel * e2 / (1.0 + e2)

    @pl.when(pl.program_id(0) == 0)
    def _():
        carry_ref[...] = jnp.zeros(carry_ref.shape, F32)

    oh1 = lane == i1
    oh2 = lane == i2
    cnt = jnp.where(oh1 | oh2, 1.0, 0.0).astype(BF16)
    row = lax.broadcasted_iota(I32, (tm, tm), 0)
    col = lax.broadcasted_iota(I32, (tm, tm), 1)
    tri = jnp.where(col < row, 1.0, 0.0).astype(BF16)
    pre = _dot(tri, cnt) + carry_ref[0:1]
    r1 = jnp.sum(jnp.where(oh1, pre, 0.0), axis=-1, keepdims=True)
    r2 = jnp.sum(jnp.where(oh2, pre, 0.0), axis=-1, keepdims=True)
    carry_ref[...] = carry_ref[...] + jnp.sum(cnt.astype(F32), axis=0, keepdims=True)
    cnt_ref[...] = carry_ref[...]
    route_ref[...] = _lane_pick(lane, [
        (0, i1 - EXPERT_LANE0), (1, i2 - EXPERT_LANE0),
        (2, g1), (3, g2), (4, r1), (5, r2)])

    gu = _dot(hb, wgu_ref[...])
    hid = _silu(gu[:, :D_SHARED]) * gu[:, D_SHARED:]
    ys = _dot(hid.astype(BF16), wd_ref[...])
    sg = jax.nn.sigmoid(jnp.sum(h * wsg_ref[0:1], axis=-1, keepdims=True))
    sh_ref[...] = sg * ys


def _moe_pre(x1, mod_l, nw8, w_rg, b_rg, w_re, b_re, ws_gate, ws_up, ws_down, w_sg):
    tm = TM_MIX
    tiles_per_seq = SEQ // tm
    pad_l = LANES - N_GROUPS - N_EXPERTS
    wr = jnp.pad(jnp.concatenate([w_rg, w_re], axis=1), ((0, 0), (0, pad_l)))
    br = jnp.pad(jnp.concatenate([b_rg, b_re])[None, :], ((0, SUBLANES - 1), (0, pad_l)))
    wgu = jnp.concatenate([ws_gate, ws_up], axis=1).astype(BF16)
    wsg8 = jnp.pad(w_sg.reshape(1, D), ((0, SUBLANES - 1), (0, 0)))
    const = lambda i: (0, 0)
    return pl.pallas_call(
        _moe_pre_kernel,
        grid=(N_TOK // tm,),
        in_specs=[
            pl.BlockSpec((tm, D), lambda i: (i, 0)),
            pl.BlockSpec((None, SUBLANES, D), lambda i: (i // tiles_per_seq, 0, 0)),
            pl.BlockSpec((SUBLANES, D), const),
            pl.BlockSpec((D, LANES), const),
            pl.BlockSpec((SUBLANES, LANES), const),
            pl.BlockSpec((D, 2 * D_SHARED), const),
            pl.BlockSpec((D_SHARED, D), const),
            pl.BlockSpec((SUBLANES, D), const),
        ],
        out_specs=[
            pl.BlockSpec((tm, D), lambda i: (i, 0)),
            pl.BlockSpec((tm, D), lambda i: (i, 0)),
            pl.BlockSpec((tm, LANES), lambda i: (i, 0)),
            pl.BlockSpec((SUBLANES, LANES), const),
        ],
        out_shape=[
            jax.ShapeDtypeStruct((N_TOK, D), F32),
            jax.ShapeDtypeStruct((N_TOK, D), F32),
            jax.ShapeDtypeStruct((N_TOK, LANES), F32),
            jax.ShapeDtypeStruct((SUBLANES, LANES), F32),
        ],
        scratch_shapes=[pltpu.VMEM((SUBLANES, LANES), F32)],
        compiler_params=_cp(("arbitrary",)),
        name="moe_pre",
    )(x1, mod_l, nw8, wr, br, wgu, ws_down.astype(BF16), wsg8)


def _n_blocks_per_expert(cnt_row):
    return jnp.floor((cnt_row + (MOE_BLOCK - 1)) * (1.0 / MOE_BLOCK))


def _moe_pos_kernel(cnt_ref, route_ref, pos_ref, blk_ref):
    tm = route_ref.shape[0]
    nb = _n_blocks_per_expert(cnt_ref[...])
    r = lax.broadcasted_iota(I32, (LANES, LANES), 0)
    c = lax.broadcasted_iota(I32, (LANES, LANES), 1)
    excl = jnp.dot(nb, jnp.where(r < c, 1.0, 0.0), precision=HIGHEST, preferred_element_type=F32)
    pstart = excl[0:1] * MOE_BLOCK
    route = route_ref[...]
    lane = lax.broadcasted_iota(I32, (tm, LANES), 1).astype(F32)
    e1 = route[:, 0:1] + EXPERT_LANE0
    e2 = route[:, 1:2] + EXPERT_LANE0
    p1 = jnp.sum(jnp.where(lane == e1, pstart, 0.0), axis=-1, keepdims=True) + route[:, 4:5]
    p2 = jnp.sum(jnp.where(lane == e2, pstart, 0.0), axis=-1, keepdims=True) + route[:, 5:6]
    pos_ref[...] = _lane_pick(lane, [(0, p1), (1, p2)]).astype(I32)

    @pl.when(pl.program_id(0) == 0)
    def _():
        nblk = blk_ref.shape[0]
        incl = excl[0:1] + nb[0:1]
        j = lax.broadcasted_iota(I32, (nblk, LANES), 0).astype(F32)
        ln = lax.broadcasted_iota(I32, (nblk, LANES), 1)
        is_e = (ln >= EXPERT_LANE0) & (ln < EXPERT_LANE0 + N_EXPERTS)
        be = jnp.sum(jnp.where(is_e & (incl <= j), 1.0, 0.0), axis=-1, keepdims=True)
        be = jnp.minimum(be, N_EXPERTS - 1.0)
        blk_ref[...] = jnp.broadcast_to(be, (nblk, LANES)).astype(I32)


def _moe_pos(cnt, route):
    tm = TM_POS
    nblk = (N_BLOCKS + SUBLANES - 1) // SUBLANES * SUBLANES
    pos, blk = pl.pallas_call(
        _moe_pos_kernel,
        grid=(N_TOK // tm,),
        in_specs=[
            pl.BlockSpec((SUBLANES, LANES), lambda i: (0, 0)),
            pl.BlockSpec((tm, LANES), lambda i: (i, 0)),
        ],
        out_specs=[
            pl.BlockSpec((tm, LANES), lambda i: (i, 0)),
            pl.BlockSpec((nblk, LANES), lambda i: (0, 0)),
        ],
        out_shape=[
            jax.ShapeDtypeStruct((N_TOK, LANES), I32),
            jax.ShapeDtypeStruct((nblk, LANES), I32),
        ],
        compiler_params=_cp(("arbitrary",)),
        name="moe_pos",
    )(cnt, route)
    return pos[:, :2].reshape(-1), blk[:N_BLOCKS, 0]


def _row_copy(src_ref, src_row, dst_ref, dst_row, sem):
    return pltpu.make_async_copy(src_ref.at[pl.ds(src_row, 1)], dst_ref.at[pl.ds(dst_row, 1)], sem)


def _scatter_kernel(pos_ref, h_ref, buf_in_ref, buf_ref, sem):
    del buf_in_ref
    n = pos_ref.shape[0]
    base = pl.program_id(0) * (n // 2)

    def issue(s, carry):
        _row_copy(h_ref, base + (s >> 1), buf_ref, pos_ref[s], sem).start()
        return carry

    lax.fori_loop(0, n, issue, 0)

    def drain(s, carry):
        _row_copy(h_ref, base + (s >> 1), buf_ref, pos_ref[s], sem).wait()
        return carry

    lax.fori_loop(0, n, drain, 0)


def _moe_scatter(pos_flat, h):
    tm = TM_DMA
    buf0 = jnp.zeros((P_ROWS, D), F32)
    return pl.pallas_call(
        _scatter_kernel,
        grid=(N_TOK // tm,),
        in_specs=[
            pl.BlockSpec((2 * tm,), lambda i: (i,), memory_space=pltpu.SMEM),
            pl.BlockSpec(memory_space=pl.ANY),
            pl.BlockSpec(memory_space=pl.ANY),
        ],
        out_specs=pl.BlockSpec(memory_space=pl.ANY),
        out_shape=jax.ShapeDtypeStruct((P_ROWS, D), F32),
        scratch_shapes=[pltpu.SemaphoreType.DMA],
        input_output_aliases={2: 0},
        compiler_params=_cp(("arbitrary",)),
        name="moe_scatter",
    )(pos_flat, h, buf0)


def _expert_kernel(be_ref, x_ref, wg_ref, wu_ref, wd_ref, o_ref, wgb_ref, wub_ref, wdb_ref):
    i = pl.program_id(0)
    prev = be_ref[jnp.maximum(i - 1, 0)]

    @pl.when((i == 0) | (be_ref[i] != prev))
    def _():
        wgb_ref[...] = wg_ref[...].astype(BF16)
        wub_ref[...] = wu_ref[...].astype(BF16)
        wdb_ref[...] = wd_ref[...].astype(BF16)

    xb = x_ref[...].astype(BF16)
    hid = _silu(_dot(xb, wgb_ref[...])) * _dot(xb, wub_ref[...])
    o_ref[...] = _dot(hid.astype(BF16), wdb_ref[...])


def _moe_experts(block_e, buf, w_gate, w_up, w_down, layer):
    bm = MOE_BLOCK
    return pl.pallas_call(
        _expert_kernel,
        grid_spec=pltpu.PrefetchScalarGridSpec(
            num_scalar_prefetch=1,
            grid=(N_BLOCKS,),
            in_specs=[
                pl.BlockSpec((bm, D), lambda i, be: (i, 0)),
                pl.BlockSpec((None, None, D, D_EXPERT), lambda i, be: (layer, be[i], 0, 0)),
                pl.BlockSpec((None, None, D, D_EXPERT), lambda i, be: (layer, be[i], 0, 0)),
                pl.BlockSpec((None, None, D_EXPERT, D), lambda i, be: (layer, be[i], 0, 0)),
            ],
            out_specs=pl.BlockSpec((bm, D), lambda i, be: (i, 0)),
            scratch_shapes=[
                pltpu.VMEM((D, D_EXPERT), BF16),
                pltpu.VMEM((D, D_EXPERT), BF16),
                pltpu.VMEM((D_EXPERT, D), BF16),
            ],
        ),
        out_shape=jax.ShapeDtypeStruct((P_ROWS, D), F32),
        compiler_params=_cp(("arbitrary",)),
        name="moe_experts",
    )(block_e, buf, w_gate, w_up, w_down)


def _combine_kernel(pos_ref, eo_ref, x_ref, sh_ref, route_ref, mod_ref, nw_ref, o_ref, g0_ref, g1_ref, sem):
    tm = x_ref.shape[0]

    def issue(r, carry):
        _row_copy(eo_ref, pos_ref[2 * r], g0_ref, r, sem).start()
        _row_copy(eo_ref, pos_ref[2 * r + 1], g1_ref, r, sem).start()
        return carry

    lax.fori_loop(0, tm, issue, 0)

    def drain(r, carry):
        _row_copy(eo_ref, pos_ref[2 * r], g0_ref, r, sem).wait()
        _row_copy(eo_ref, pos_ref[2 * r + 1], g1_ref, r, sem).wait()
        return carry

    lax.fori_loop(0, tm, drain, 0)

    route = route_ref[...]
    y = route[:, 2:3] * g0_ref[...] + route[:, 3:4] * g1_ref[...] + sh_ref[...]
    mod = mod_ref[...]
    o_ref[...] = x_ref[...] + mod[5:6] * (_rms(y) * nw_ref[3:4])


def _moe_combine(pos_flat, eo, x1, sh, route, mod_l, nw8):
    tm = TM_DMA
    tiles_per_seq = SEQ // tm
    return pl.pallas_call(
        _combine_kernel,
        grid=(N_TOK // tm,),
        in_specs=[
            pl.BlockSpec((2 * tm,), lambda i: (i,), memory_space=pltpu.SMEM),
            pl.BlockSpec(memory_space=pl.ANY),
            pl.BlockSpec((tm, D), lambda i: (i, 0)),
            pl.BlockSpec((tm, D), lambda i: (i, 0)),
            pl.BlockSpec((tm, LANES), lambda i: (i, 0)),
            pl.BlockSpec((None, SUBLANES, D), lambda i: (i // tiles_per_seq, 0, 0)),
            pl.BlockSpec((SUBLANES, D), lambda i: (0, 0)),
        ],
        out_specs=pl.BlockSpec((tm, D), lambda i: (i, 0)),
        out_shape=jax.ShapeDtypeStruct((N_TOK, D), F32),
        scratch_shapes=[pltpu.VMEM((tm, D), F32), pltpu.VMEM((tm, D), F32), pltpu.SemaphoreType.DMA],
        compiler_params=_cp(("arbitrary",)),
        name="moe_combine",
    )(pos_flat, eo, x1, sh, route, mod_l, nw8)


def _hier_moe_block(x1, mod_l, nw8, layer, w_rg, b_rg, w_re, b_re, w_gate, w_up, w_down,
                    ws_gate, ws_up, ws_down, w_sg):
    h, sh, route, cnt = _moe_pre(x1, mod_l, nw8, w_rg, b_rg, w_re, b_re, ws_gate, ws_up, ws_down, w_sg)
    pos_flat, block_e = _moe_pos(cnt, route)
    buf = _moe_scatter(pos_flat, h)
    eo = _moe_experts(block_e, buf, w_gate, w_up, w_down, layer)
    return _moe_combine(pos_flat, eo, x1, sh, route, mod_l, nw8)


def _gdn_pre_kernel(x_ref, mod_ref, nw_ref, w_ref, wba_ref, cw_ref, misc_ref,
                    q_ref, k_ref, v_ref, z_ref, gb_ref, gt_ref, cbuf_ref, *, tiles_per_seq):
    tm = x_ref.shape[0]
    qkv_w = 3 * D
    x = x_ref[...]
    mod = mod_ref[...]
    nw = nw_ref[...]
    h = _rms(x) * nw[0:1] * (1.0 + mod[1:2]) + mod[0:1]
    hb = h.astype(BF16)
    proj = _dot(hb, w_ref[...])
    z_ref[...] = proj[:, qkv_w:].astype(BF16)
    pre = proj[:, :qkv_w]

    @pl.when(pl.program_id(0) % tiles_per_seq == 0)
    def _():
        cbuf_ref[0:SUBLANES, :] = jnp.zeros((SUBLANES, qkv_w), F32)

    cbuf_ref[SUBLANES:SUBLANES + tm, :] = pre
    cw = cw_ref[...]
    conv = (cw[0:1] * cbuf_ref[SUBLANES - 3:SUBLANES - 3 + tm, :]
            + cw[1:2] * cbuf_ref[SUBLANES - 2:SUBLANES - 2 + tm, :]
            + cw[2:3] * cbuf_ref[SUBLANES - 1:SUBLANES - 1 + tm, :]
            + cw[3:4] * pre)
    cbuf_ref[0:SUBLANES, :] = cbuf_ref[tm:tm + SUBLANES, :]
    act = _silu(conv)
    for hd in range(HEADS):
        lo = hd * HEAD_DIM
        qh = act[:, lo:lo + HEAD_DIM]
        kh = act[:, D + lo:D + lo + HEAD_DIM]
        qn = qh * lax.rsqrt(jnp.sum(qh * qh, axis=-1, keepdims=True) + EPS) * (HEAD_DIM ** -0.5)
        kn = kh * lax.rsqrt(jnp.sum(kh * kh, axis=-1, keepdims=True) + EPS)
        q_ref[:, lo:lo + HEAD_DIM] = qn.astype(BF16)
        k_ref[:, lo:lo + HEAD_DIM] = kn.astype(BF16)
    v_ref[...] = act[:, 2 * D:].astype(BF16)

    ba = _dot(hb, wba_ref[...])
    misc = misc_ref[...]
    beta = jax.nn.sigmoid(ba)
    sp_in = ba + misc[1:2]
    softplus = jnp.maximum(sp_in, 0.0) + jnp.log(1.0 + jnp.exp(-jnp.abs(sp_in)))
    g = -jnp.exp(misc[0:1]) * softplus
    row = lax.broadcasted_iota(I32, (tm, tm), 0)
    col = lax.broadcasted_iota(I32, (tm, tm), 1)
    tri = jnp.where((col <= row) & ((col >> CHUNK_SHIFT) == (row >> CHUNK_SHIFT)), 1.0, 0.0)
    gc = jnp.dot(tri, g, precision=HIGHEST, preferred_element_type=F32)
    lane = lax.broadcasted_iota(I32, (tm, LANES), 1)
    gb = jnp.where(lane < HEADS, beta, gc)
    gb_ref[...] = gb
    for c in range(tm // CHUNK):
        blk = jnp.concatenate([gb[c * CHUNK:(c + 1) * CHUNK], jnp.zeros((LANES - CHUNK, LANES), F32)], axis=0)
        gt_ref[c] = blk.T[HEADS:2 * HEADS, :]


def _gdn_pre(x2, mod_l, nw8, w_in, conv_w, a_log, dt_bias):
    tm = TM_GDN
    tiles_per_seq = SEQ // tm
    qkvz = 4 * D
    w_main = w_in[:, :qkvz].astype(BF16)
    wba = jnp.pad(w_in[:, qkvz:], ((0, 0), (0, LANES - 2 * HEADS))).astype(BF16)
    cw8 = jnp.pad(conv_w, ((0, SUBLANES - conv_w.shape[0]), (0, 0)))
    misc = jnp.zeros((SUBLANES, LANES), F32)
    misc = misc.at[0, HEADS:2 * HEADS].set(a_log).at[1, HEADS:2 * HEADS].set(dt_bias)
    const = lambda i: (0, 0)
    tok = lambda i: (i, 0)
    return pl.pallas_call(
        functools.partial(_gdn_pre_kernel, tiles_per_seq=tiles_per_seq),
        grid=(N_TOK // tm,),
        in_specs=[
            pl.BlockSpec((tm, D), tok),
            pl.BlockSpec((None, SUBLANES, D), lambda i: (i // tiles_per_seq, 0, 0)),
            pl.BlockSpec((SUBLANES, D), const),
            pl.BlockSpec((D, qkvz), const),
            pl.BlockSpec((D, LANES), const),
            pl.BlockSpec((SUBLANES, 3 * D), const),
            pl.BlockSpec((SUBLANES, LANES), const),
        ],
        out_specs=[
            pl.BlockSpec((tm, D), tok),
            pl.BlockSpec((tm, D), tok),
            pl.BlockSpec((tm, D), tok),
            pl.BlockSpec((tm, D), tok),
            pl.BlockSpec((tm, LANES), tok),
            pl.BlockSpec((tm // CHUNK, HEADS, LANES), lambda i: (i, 0, 0)),
        ],
        out_shape=[
            jax.ShapeDtypeStruct((N_TOK, D), BF16),
            jax.ShapeDtypeStruct((N_TOK, D), BF16),
            jax.ShapeDtypeStruct((N_TOK, D), BF16),
            jax.ShapeDtypeStruct((N_TOK, D), BF16),
            jax.ShapeDtypeStruct((N_TOK, LANES), F32),
            jax.ShapeDtypeStruct((N_TOK // CHUNK, HEADS, LANES), F32),
        ],
        scratch_shapes=[pltpu.VMEM((tm + SUBLANES, 3 * D), F32)],
        compiler_params=_cp(("arbitrary",)),
        name="gdn_pre",
    )(x2, mod_l, nw8, w_main, wba, cw8, misc)


def _dot_nt(a, b):
    return lax.dot_general(a, b, (((1,), (1,)), ((), ())), preferred_element_type=F32)


def _dot_tn(a, b):
    return lax.dot_general(a, b, (((0,), (0,)), ((), ())), preferred_element_type=F32)


def _gdn_chunk_kernel(q_ref, k_ref, v_ref, gb_ref, gt_ref, o_ref, s_ref):
    @pl.when(pl.program_id(1) == 0)
    def _():
        s_ref[...] = jnp.zeros(s_ref.shape, F32)

    c = CHUNK
    gb = gb_ref[...]
    gt = gt_ref[...]
    row = lax.broadcasted_iota(I32, (c, c), 0)
    col = lax.broadcasted_iota(I32, (c, c), 1)
    causal = col <= row
    strict = col < row
    eye = jnp.where(col == row, 1.0, 0.0)
    for hd in range(HEADS):
        lo = hd * HEAD_DIM
        q = q_ref[:, lo:lo + HEAD_DIM].astype(F32)
        k = k_ref[:, lo:lo + HEAD_DIM].astype(F32)
        v = v_ref[:, lo:lo + HEAD_DIM].astype(F32)
        beta = gb[:, hd:hd + 1]
        gcol = gb[:, HEADS + hd:HEADS + hd + 1]
        grow = gt[hd:hd + 1, 0:c]
        glast = gcol[c - 1:c, :]
        egc = jnp.exp(gcol)
        decay = jnp.exp(jnp.where(causal, gcol - grow, -jnp.inf))
        kb = k * beta
        kbf = k.astype(BF16)
        lower = jnp.where(strict, _dot_nt(kb.astype(BF16), kbf) * decay, 0.0)
        t = eye - lower
        p = lower
        for _ in range(5):
            pb = p.astype(BF16)
            p = _dot(pb, pb)
            t = t + _dot(t.astype(BF16), p.astype(BF16))
        rhs = jnp.concatenate([v * beta, kb * egc], axis=-1)
        uw = _dot(t.astype(BF16), rhs.astype(BF16))
        u = uw[:, :HEAD_DIM]
        w = uw[:, HEAD_DIM:]
        attn = jnp.where(causal, _dot_nt(q.astype(BF16), kbf) * decay, 0.0)
        s = s_ref[hd]
        sb = s.astype(BF16)
        v_new = u - _dot(w.astype(BF16), sb)
        vb = v_new.astype(BF16)
        o = _dot((q * egc).astype(BF16), sb) + _dot(attn.astype(BF16), vb)
        kd = k * jnp.exp(glast - gcol)
        s_ref[hd] = s * jnp.exp(glast) + _dot_tn(kd.astype(BF16), vb)
        o_ref[:, lo:lo + HEAD_DIM] = o.astype(BF16)


def _gdn_chunks(q, k, v, gb, gt):
    n_chunks = SEQ // CHUNK
    tok = lambda b, c: (b * n_chunks + c, 0)
    return pl.pallas_call(
        _gdn_chunk_kernel,
        grid=(BATCH, n_chunks),
        in_specs=[
            pl.BlockSpec((CHUNK, D), tok),
            pl.BlockSpec((CHUNK, D), tok),
            pl.BlockSpec((CHUNK, D), tok),
            pl.BlockSpec((CHUNK, LANES), tok),
            pl.BlockSpec((None, HEADS, LANES), lambda b, c: (b * n_chunks + c, 0, 0)),
        ],
        out_specs=pl.BlockSpec((CHUNK, D), tok),
        out_shape=jax.ShapeDtypeStruct((N_TOK, D), BF16),
        scratch_shapes=[pltpu.VMEM((HEADS, HEAD_DIM, HEAD_DIM), F32)],
        compiler_params=_cp(("arbitrary", "arbitrary")),
        name="gdn_chunks",
    )(q, k, v, gb, gt)


def _gdn_post_kernel(o_ref, z_ref, x_ref, mod_ref, nw_ref, gnw_ref, wout_ref, out_ref):
    gnw = gnw_ref[0:1]
    parts = []
    for hd in range(HEADS):
        lo = hd * HEAD_DIM
        oh = o_ref[:, lo:lo + HEAD_DIM].astype(F32)
        zh = z_ref[:, lo:lo + HEAD_DIM].astype(F32)
        parts.append((_rms(oh) * gnw * _silu(zh)).astype(BF16))
    y = _dot(jnp.concatenate(parts, axis=-1), wout_ref[...])
    mod = mod_ref[...]
    out_ref[...] = x_ref[...] + mod[2:3] * (_rms(y) * nw_ref[1:2])


def _gdn_post(o, z, x2, mod_l, nw8, gdn_norm_w, w_out):
    tm = TM_MIX
    tiles_per_seq = SEQ // tm
    gnw8 = jnp.pad(gdn_norm_w.reshape(1, HEAD_DIM), ((0, SUBLANES - 1), (0, 0)))
    tok = lambda i: (i, 0)
    const = lambda i: (0, 0)
    return pl.pallas_call(
        _gdn_post_kernel,
        grid=(N_TOK // tm,),
        in_specs=[
            pl.BlockSpec((tm, D), tok),
            pl.BlockSpec((tm, D), tok),
            pl.BlockSpec((tm, D), tok),
            pl.BlockSpec((None, SUBLANES, D), lambda i: (i // tiles_per_seq, 0, 0)),
            pl.BlockSpec((SUBLANES, D), const),
            pl.BlockSpec((SUBLANES, HEAD_DIM), const),
            pl.BlockSpec((D, D), const),
        ],
        out_specs=pl.BlockSpec((tm, D), tok),
        out_shape=jax.ShapeDtypeStruct((N_TOK, D), F32),
        compiler_params=_cp(("arbitrary",)),
        name="gdn_post",
    )(o, z, x2, mod_l, nw8, gnw8, w_out.astype(BF16))


def kernel(x, c, ada_w, ada_b, norm_w, conv_in_w, conv_w, conv_out_w, gdn_in_w, gdn_conv_w, gdn_a_log,
           gdn_dt_bias, gdn_norm_w, gdn_out_w, moe_group_w, moe_group_b, moe_expert_w, moe_expert_b,
           moe_w_gate, moe_w_up, moe_w_down, shared_w_gate, shared_w_up, shared_w_down, shared_gate_w):
    mod = _ada_mod(c, ada_w, ada_b)
    nw8 = jnp.pad(norm_w, ((0, 0), (0, SUBLANES - norm_w.shape[1]), (0, 0)))
    x2 = x.reshape(N_TOK, D)

    def moe(x1, layer):
        return _hier_moe_block(
            x1, mod[layer], nw8[layer], layer,
            moe_group_w[layer], moe_group_b[layer], moe_expert_w[layer], moe_expert_b[layer],
            moe_w_gate, moe_w_up, moe_w_down,
            shared_w_gate[layer], shared_w_up[layer], shared_w_down[layer], shared_gate_w[layer])

    x2 = _conv_mixer(x2, mod[0], nw8[0], conv_in_w[0], conv_w[0], conv_out_w[0])
    x2 = moe(x2, 0)
    q, k, v, z, gb, gt = _gdn_pre(x2, mod[1], nw8[1], gdn_in_w[0], gdn_conv_w[0], gdn_a_log[0], gdn_dt_bias[0])
    o = _gdn_chunks(q, k, v, gb, gt)
    x2 = _gdn_post(o, z, x2, mod[1], nw8[1], gdn_norm_w[0], gdn_out_w[0])
    x2 = moe(x2, 1)
    return x2.reshape(BATCH, SEQ, D)
```

```python
import functools

import jax
import jax.numpy as jnp
from jax import lax
from jax.experimental import pallas as pl
from jax.experimental.pallas import tpu as pltpu

F32 = jnp.float32
BF16 = jnp.bfloat16
I32 = jnp.int32
HIGHEST = lax.Precision.HIGHEST

D = 1024
BATCH = 4
SEQ = 4096
N_TOK = BATCH * SEQ
HEADS = 8
HEAD_DIM = 128
CHUNK = 64
CHUNK_SHIFT = 6
N_GROUPS = 8
N_EXPERTS = 64
D_EXPERT = 256
D_SHARED = 512
EPS = 1e-6
LANES = 128
SUBLANES = 8
EXPERT_LANE0 = N_GROUPS

MOE_BLOCK = 128
N_SLOTS = N_TOK * 2
N_BLOCKS = N_SLOTS // MOE_BLOCK + N_EXPERTS
P_ROWS = N_BLOCKS * MOE_BLOCK

TM_MIX = 512
TM_GDN = 256
TM_DMA = 256
TM_POS = 2048
GDN_BATCH_PER_STEP = 4
VMEM_LIMIT = 56 * 1024 * 1024


def _cp(sem):
    return pltpu.CompilerParams(dimension_semantics=sem, vmem_limit_bytes=VMEM_LIMIT)


def _rms(x):
    return x * lax.rsqrt(jnp.mean(x * x, axis=-1, keepdims=True) + EPS)


def _silu(x):
    return x * jax.nn.sigmoid(x)


def _dot(a, b):
    return jnp.dot(a, b, preferred_element_type=F32)


def _ada_kernel(c_ref, w_ref, b_ref, o_ref):
    cs = _silu(c_ref[...])
    o_ref[0] = _dot(cs.astype(BF16), w_ref[0].astype(BF16)) + b_ref[0]


def _ada_mod(c, ada_w, ada_b):
    depth = ada_w.shape[0]
    tn = 1024
    c8 = jnp.pad(c, ((0, SUBLANES - BATCH), (0, 0)))
    mod = pl.pallas_call(
        _ada_kernel,
        grid=(depth, 6 * D // tn),
        in_specs=[
            pl.BlockSpec((SUBLANES, D), lambda l, j: (0, 0)),
            pl.BlockSpec((1, D, tn), lambda l, j: (l, 0, j)),
            pl.BlockSpec((1, 1, tn), lambda l, j: (l, 0, j)),
        ],
        out_specs=pl.BlockSpec((1, SUBLANES, tn), lambda l, j: (l, 0, j)),
        out_shape=jax.ShapeDtypeStruct((depth, SUBLANES, 6 * D), F32),
        compiler_params=_cp(("arbitrary", "arbitrary")),
        name="ada_mod",
    )(c8, ada_w, ada_b.reshape(depth, 1, 6 * D))
    mod = mod[:, :BATCH].reshape(depth, BATCH, 6, D)
    return jnp.pad(mod, ((0, 0), (0, 0), (0, 2), (0, 0)))


def _conv_mixer_kernel(x_ref, mod_ref, nw_ref, win_ref, cw_ref, wout_ref, o_ref, ubuf_ref, *, tiles_per_seq):
    tm = x_ref.shape[0]
    x = x_ref[...]
    mod = mod_ref[...]
    nw = nw_ref[...]
    h = _rms(x) * nw[0:1] * (1.0 + mod[1:2]) + mod[0:1]
    bcx = _dot(h.astype(BF16), win_ref[...])
    u = bcx[:, D:2 * D] * bcx[:, 2 * D:]

    @pl.when(pl.program_id(0) % tiles_per_seq == 0)
    def _():
        ubuf_ref[0:SUBLANES, :] = jnp.zeros((SUBLANES, D), F32)

    ubuf_ref[SUBLANES:SUBLANES + tm, :] = u
    cw = cw_ref[...]
    conv = (cw[0:1] * ubuf_ref[SUBLANES - 2:SUBLANES - 2 + tm, :]
            + cw[1:2] * ubuf_ref[SUBLANES - 1:SUBLANES - 1 + tm, :]
            + cw[2:3] * u)
    ubuf_ref[0:SUBLANES, :] = ubuf_ref[tm:tm + SUBLANES, :]
    y = _dot((bcx[:, :D] * conv).astype(BF16), wout_ref[...])
    o_ref[...] = x + mod[2:3] * (_rms(y) * nw[1:2])


def _conv_mixer(x2, mod_l, nw8, w_in, conv_w, w_out):
    tm = TM_MIX
    tiles_per_seq = SEQ // tm
    cw8 = jnp.pad(conv_w, ((0, SUBLANES - conv_w.shape[0]), (0, 0)))
    return pl.pallas_call(
        functools.partial(_conv_mixer_kernel, tiles_per_seq=tiles_per_seq),
        grid=(N_TOK // tm,),
        in_specs=[
            pl.BlockSpec((tm, D), lambda i: (i, 0)),
            pl.BlockSpec((None, SUBLANES, D), lambda i: (i // tiles_per_seq, 0, 0)),
            pl.BlockSpec((SUBLANES, D), lambda i: (0, 0)),
            pl.BlockSpec((D, 3 * D), lambda i: (0, 0)),
            pl.BlockSpec((SUBLANES, D), lambda i: (0, 0)),
            pl.BlockSpec((D, D), lambda i: (0, 0)),
        ],
        out_specs=pl.BlockSpec((tm, D), lambda i: (i, 0)),
        out_shape=jax.ShapeDtypeStruct((N_TOK, D), F32),
        scratch_shapes=[pltpu.VMEM((tm + SUBLANES, D), F32)],
        compiler_params=_cp(("arbitrary",)),
        name="conv_mixer",
    )(x2, mod_l, nw8, w_in.astype(BF16), cw8, w_out.astype(BF16))


def _lane_pick(lane, mask_val_pairs):
    out = jnp.zeros(lane.shape, F32)
    for idx, val in mask_val_pairs:
        out = jnp.where(lane == float(idx), val, out)
    return out


def _moe_pre_kernel(x_ref, mod_ref, nw_ref, wr_ref, br_ref, wgu_ref, wd_ref, wsg_ref,
                    h_ref, sh_ref, route_ref, cnt_ref, carry_ref):
    tm = x_ref.shape[0]
    x = x_ref[...]
    mod = mod_ref[...]
    nw = nw_ref[...]
    h = _rms(x) * nw[2:3] * (1.0 + mod[4:5]) + mod[3:4]
    h_ref[...] = h
    hb = h.astype(BF16)

    logits = jnp.dot(h, wr_ref[...], precision=HIGHEST, preferred_element_type=F32) + br_ref[0:1]
    lane_i = lax.broadcasted_iota(I32, (tm, LANES), 1)
    lane = lane_i.astype(F32)
    neg = jnp.float32(-jnp.inf)
    is_group = lane_i < N_GROUPS
    gl = jnp.where(is_group, logits, neg)
    gmax = jnp.max(gl, axis=-1, keepdims=True)
    gsel = jnp.min(jnp.where(gl == gmax, lane, float(LANES)), axis=-1, keepdims=True)
    psel = 1.0 / jnp.sum(jnp.where(is_group, jnp.exp(logits - gmax), 0.0), axis=-1, keepdims=True)
    lane_group = ((lane_i - EXPERT_LANE0) >> 3).astype(F32)
    in_group = (lane_i >= EXPERT_LANE0) & (lane_i < EXPERT_LANE0 + N_EXPERTS) & (lane_group == gsel)
    el = jnp.where(in_group, logits, neg)
    v1 = jnp.max(el, axis=-1, keepdims=True)
    i1 = jnp.min(jnp.where(el == v1, lane, float(LANES)), axis=-1, keepdims=True)
    el2 = jnp.where(lane == i1, neg, el)
    v2 = jnp.max(el2, axis=-1, keepdims=True)
    i2 = jnp.min(jnp.where(el2 == v2, lane, float(LANES)), axis=-1, keepdims=True)
    e2 = jnp.exp(v2 - v1)
    g1 = psel / (1.0 + e2)
    g2 = psel * e2 / (1.0 + e2)

    @pl.when(pl.program_id(0) == 0)
    def _():
        carry_ref[...] = jnp.zeros(carry_ref.shape, F32)

    oh1 = lane == i1
    oh2 = lane == i2
    cnt = jnp.where(oh1 | oh2, 1.0, 0.0).astype(BF16)
    row = lax.broadcasted_iota(I32, (tm, tm), 0)
    col = lax.broadcasted_iota(I32, (tm, tm), 1)
    tri = jnp.where(col < row, 1.0, 0.0).astype(BF16)
    pre = _dot(tri, cnt) + carry_ref[0:1]
    r1 = jnp.sum(jnp.where(oh1, pre, 0.0), axis=-1, keepdims=True)
    r2 = jnp.sum(jnp.where(oh2, pre, 0.0), axis=-1, keepdims=True)
    carry_ref[...] = carry_ref[...] + jnp.sum(cnt.astype(F32), axis=0, keepdims=True)
    cnt_ref[...] = carry_ref[...]
    route_ref[...] = _lane_pick(lane, [
        (0, i1 - EXPERT_LANE0), (1, i2 - EXPERT_LANE0),
        (2, g1), (3, g2), (4, r1), (5, r2)])

    gu = _dot(hb, wgu_ref[...])
    hid = _silu(gu[:, :D_SHARED]) * gu[:, D_SHARED:]
    ys = _dot(hid.astype(BF16), wd_ref[...])
    sg = jax.nn.sigmoid(jnp.sum(h * wsg_ref[0:1], axis=-1, keepdims=True))
    sh_ref[...] = sg * ys


def _moe_pre(x1, mod_l, nw8, w_rg, b_rg, w_re, b_re, ws_gate, ws_up, ws_down, w_sg):
    tm = TM_MIX
    tiles_per_seq = SEQ // tm
    pad_l = LANES - N_GROUPS - N_EXPERTS
    wr = jnp.pad(jnp.concatenate([w_rg, w_re], axis=1), ((0, 0), (0, pad_l)))
    br = jnp.pad(jnp.concatenate([b_rg, b_re])[None, :], ((0, SUBLANES - 1), (0, pad_l)))
    wgu = jnp.concatenate([ws_gate, ws_up], axis=1).astype(BF16)
    wsg8 = jnp.pad(w_sg.reshape(1, D), ((0, SUBLANES - 1), (0, 0)))
    const = lambda i: (0, 0)
    return pl.pallas_call(
        _moe_pre_kernel,
        grid=(N_TOK // tm,),
        in_specs=[
            pl.BlockSpec((tm, D), lambda i: (i, 0)),
            pl.BlockSpec((None, SUBLANES, D), lambda i: (i // tiles_per_seq, 0, 0)),
            pl.BlockSpec((SUBLANES, D), const),
            pl.BlockSpec((D, LANES), const),
            pl.BlockSpec((SUBLANES, LANES), const),
            pl.BlockSpec((D, 2 * D_SHARED), const),
            pl.BlockSpec((D_SHARED, D), const),
            pl.BlockSpec((SUBLANES, D), const),
        ],
        out_specs=[
            pl.BlockSpec((tm, D), lambda i: (i, 0)),
            pl.BlockSpec((tm, D), lambda i: (i, 0)),
            pl.BlockSpec((tm, LANES), lambda i: (i, 0)),
            pl.BlockSpec((SUBLANES, LANES), const),
        ],
        out_shape=[
            jax.ShapeDtypeStruct((N_TOK, D), F32),
            jax.ShapeDtypeStruct((N_TOK, D), F32),
            jax.ShapeDtypeStruct((N_TOK, LANES), F32),
            jax.ShapeDtypeStruct((SUBLANES, LANES), F32),
        ],
        scratch_shapes=[pltpu.VMEM((SUBLANES, LANES), F32)],
        compiler_params=_cp(("arbitrary",)),
        name="moe_pre",
    )(x1, mod_l, nw8, wr, br, wgu, ws_down.astype(BF16), wsg8)


def _n_blocks_per_expert(cnt_row):
    return jnp.floor((cnt_row + (MOE_BLOCK - 1)) * (1.0 / MOE_BLOCK))


def _moe_pos_kernel(cnt_ref, route_ref, pos_ref, blk_ref):
    tm = route_ref.shape[0]
    nb = _n_blocks_per_expert(cnt_ref[...])
    r = lax.broadcasted_iota(I32, (LANES, LANES), 0)
    c = lax.broadcasted_iota(I32, (LANES, LANES), 1)
    excl = jnp.dot(nb, jnp.where(r < c, 1.0, 0.0), precision=HIGHEST, preferred_element_type=F32)
    pstart = excl[0:1] * MOE_BLOCK
    route = route_ref[...]
    lane = lax.broadcasted_iota(I32, (tm, LANES), 1).astype(F32)
    e1 = route[:, 0:1] + EXPERT_LANE0
    e2 = route[:, 1:2] + EXPERT_LANE0
    p1 = jnp.sum(jnp.where(lane == e1, pstart, 0.0), axis=-1, keepdims=True) + route[:, 4:5]
    p2 = jnp.sum(jnp.where(lane == e2, pstart, 0.0), axis=-1, keepdims=True) + route[:, 5:6]
    pos_ref[...] = _lane_pick(lane, [(0, p1), (1, p2)]).astype(I32)

    @pl.when(pl.program_id(0) == 0)
    def _():
        nblk = blk_ref.shape[0]
        incl = excl[0:1] + nb[0:1]
        j = lax.broadcasted_iota(I32, (nblk, LANES), 0).astype(F32)
        ln = lax.broadcasted_iota(I32, (nblk, LANES), 1)
        is_e = (ln >= EXPERT_LANE0) & (ln < EXPERT_LANE0 + N_EXPERTS)
        be = jnp.sum(jnp.where(is_e & (incl <= j), 1.0, 0.0), axis=-1, keepdims=True)
        be = jnp.minimum(be, N_EXPERTS - 1.0)
        blk_ref[...] = jnp.broadcast_to(be, (nblk, LANES)).astype(I32)


def _moe_pos(cnt, route):
    tm = TM_POS
    nblk = (N_BLOCKS + SUBLANES - 1) // SUBLANES * SUBLANES
    pos, blk = pl.pallas_call(
        _moe_pos_kernel,
        grid=(N_TOK // tm,),
        in_specs=[
            pl.BlockSpec((SUBLANES, LANES), lambda i: (0, 0)),
            pl.BlockSpec((tm, LANES), lambda i: (i, 0)),
        ],
        out_specs=[
            pl.BlockSpec((tm, LANES), lambda i: (i, 0)),
            pl.BlockSpec((nblk, LANES), lambda i: (0, 0)),
        ],
        out_shape=[
            jax.ShapeDtypeStruct((N_TOK, LANES), I32),
            jax.ShapeDtypeStruct((nblk, LANES), I32),
        ],
        compiler_params=_cp(("arbitrary",)),
        name="moe_pos",
    )(cnt, route)
    return pos[:, :2].reshape(-1), blk[:N_BLOCKS, 0]


def _row_copy(src_ref, src_row, dst_ref, dst_row, sem):
    return pltpu.make_async_copy(src_ref.at[pl.ds(src_row, 1)], dst_ref.at[pl.ds(dst_row, 1)], sem)


def _scatter_kernel(pos_ref, h_ref, buf_in_ref, buf_ref, sem):
    del buf_in_ref
    n = pos_ref.shape[0]

    def issue(s, carry):
        _row_copy(h_ref, s >> 1, buf_ref, pos_ref[s], sem).start()
        return carry

    lax.fori_loop(0, n, issue, 0)

    def drain(s, carry):
        _row_copy(h_ref, s >> 1, buf_ref, pos_ref[s], sem).wait()
        return carry

    lax.fori_loop(0, n, drain, 0)


def _moe_scatter(pos_flat, h):
    tm = TM_DMA
    buf0 = jnp.zeros((P_ROWS, D), F32)
    return pl.pallas_call(
        _scatter_kernel,
        grid=(N_TOK // tm,),
        in_specs=[
            pl.BlockSpec((2 * tm,), lambda i: (i,), memory_space=pltpu.SMEM),
            pl.BlockSpec((tm, D), lambda i: (i, 0)),
            pl.BlockSpec(memory_space=pl.ANY),
        ],
        out_specs=pl.BlockSpec(memory_space=pl.ANY),
        out_shape=jax.ShapeDtypeStruct((P_ROWS, D), F32),
        scratch_shapes=[pltpu.SemaphoreType.DMA],
        input_output_aliases={2: 0},
        compiler_params=_cp(("arbitrary",)),
        name="moe_scatter",
    )(pos_flat, h, buf0)


def _expert_kernel(be_ref, x_ref, wg_ref, wu_ref, wd_ref, o_ref, wgb_ref, wub_ref, wdb_ref):
    i = pl.program_id(0)
    prev = be_ref[jnp.maximum(i - 1, 0)]

    @pl.when((i == 0) | (be_ref[i] != prev))
    def _():
        wgb_ref[...] = wg_ref[...].astype(BF16)
        wub_ref[...] = wu_ref[...].astype(BF16)
        wdb_ref[...] = wd_ref[...].astype(BF16)

    xb = x_ref[...].astype(BF16)
    hid = _silu(_dot(xb, wgb_ref[...])) * _dot(xb, wub_ref[...])
    o_ref[...] = _dot(hid.astype(BF16), wdb_ref[...])


def _moe_experts(block_e, buf, w_gate, w_up, w_down, layer):
    bm = MOE_BLOCK
    return pl.pallas_call(
        _expert_kernel,
        grid_spec=pltpu.PrefetchScalarGridSpec(
            num_scalar_prefetch=1,
            grid=(N_BLOCKS,),
            in_specs=[
                pl.BlockSpec((bm, D), lambda i, be: (i, 0)),
                pl.BlockSpec((None, None, D, D_EXPERT), lambda i, be: (layer, be[i], 0, 0)),
                pl.BlockSpec((None, None, D, D_EXPERT), lambda i, be: (layer, be[i], 0, 0)),
                pl.BlockSpec((None, None, D_EXPERT, D), lambda i, be: (layer, be[i], 0, 0)),
            ],
            out_specs=pl.BlockSpec((bm, D), lambda i, be: (i, 0)),
            scratch_shapes=[
                pltpu.VMEM((D, D_EXPERT), BF16),
                pltpu.VMEM((D, D_EXPERT), BF16),
                pltpu.VMEM((D_EXPERT, D), BF16),
            ],
        ),
        out_shape=jax.ShapeDtypeStruct((P_ROWS, D), F32),
        compiler_params=_cp(("arbitrary",)),
        name="moe_experts",
    )(block_e, buf, w_gate, w_up, w_down)


def _combine_kernel(pos_ref, eo_ref, x_ref, sh_ref, route_ref, mod_ref, nw_ref, o_ref, g0_ref, g1_ref, sem):
    tm = x_ref.shape[0]

    def issue(r, carry):
        _row_copy(eo_ref, pos_ref[2 * r], g0_ref, r, sem).start()
        _row_copy(eo_ref, pos_ref[2 * r + 1], g1_ref, r, sem).start()
        return carry

    lax.fori_loop(0, tm, issue, 0)

    def drain(r, carry):
        _row_copy(eo_ref, pos_ref[2 * r], g0_ref, r, sem).wait()
        _row_copy(eo_ref, pos_ref[2 * r + 1], g1_ref, r, sem).wait()
        return carry

    lax.fori_loop(0, tm, drain, 0)

    route = route_ref[...]
    y = route[:, 2:3] * g0_ref[...] + route[:, 3:4] * g1_ref[...] + sh_ref[...]
    mod = mod_ref[...]
    o_ref[...] = x_ref[...] + mod[5:6] * (_rms(y) * nw_ref[3:4])


def _moe_combine(pos_flat, eo, x1, sh, route, mod_l, nw8):
    tm = TM_DMA
    tiles_per_seq = SEQ // tm
    return pl.pallas_call(
        _combine_kernel,
        grid=(N_TOK // tm,),
        in_specs=[
            pl.BlockSpec((2 * tm,), lambda i: (i,), memory_space=pltpu.SMEM),
            pl.BlockSpec(memory_space=pl.ANY),
            pl.BlockSpec((tm, D), lambda i: (i, 0)),
            pl.BlockSpec((tm, D), lambda i: (i, 0)),
            pl.BlockSpec((tm, LANES), lambda i: (i, 0)),
            pl.BlockSpec((None, SUBLANES, D), lambda i: (i // tiles_per_seq, 0, 0)),
            pl.BlockSpec((SUBLANES, D), lambda i: (0, 0)),
        ],
        out_specs=pl.BlockSpec((tm, D), lambda i: (i, 0)),
        out_shape=jax.ShapeDtypeStruct((N_TOK, D), F32),
        scratch_shapes=[pltpu.VMEM((tm, D), F32), pltpu.VMEM((tm, D), F32), pltpu.SemaphoreType.DMA],
        compiler_params=_cp(("arbitrary",)),
        name="moe_combine",
    )(pos_flat, eo, x1, sh, route, mod_l, nw8)


def _hier_moe_block(x1, mod_l, nw8, layer, w_rg, b_rg, w_re, b_re, w_gate, w_up, w_down,
                    ws_gate, ws_up, ws_down, w_sg):
    h, sh, route, cnt = _moe_pre(x1, mod_l, nw8, w_rg, b_rg, w_re, b_re, ws_gate, ws_up, ws_down, w_sg)
    pos_flat, block_e = _moe_pos(cnt, route)
    buf = _moe_scatter(pos_flat, h)
    eo = _moe_experts(block_e, buf, w_gate, w_up, w_down, layer)
    return _moe_combine(pos_flat, eo, x1, sh, route, mod_l, nw8)


def _gdn_pre_kernel(x_ref, mod_ref, nw_ref, w_ref, wba_ref, cw_ref, misc_ref,
                    q_ref, k_ref, v_ref, z_ref, gb_ref, gt_ref, cbuf_ref, *, tiles_per_seq):
    tm = x_ref.shape[0]
    qkv_w = 3 * D
    x = x_ref[...]
    mod = mod_ref[...]
    nw = nw_ref[...]
    h = _rms(x) * nw[0:1] * (1.0 + mod[1:2]) + mod[0:1]
    hb = h.astype(BF16)
    proj = _dot(hb, w_ref[...])
    z_ref[...] = proj[:, qkv_w:].astype(BF16)
    pre = proj[:, :qkv_w]

    @pl.when(pl.program_id(0) % tiles_per_seq == 0)
    def _():
        cbuf_ref[0:SUBLANES, :] = jnp.zeros((SUBLANES, qkv_w), F32)

    cbuf_ref[SUBLANES:SUBLANES + tm, :] = pre
    cw = cw_ref[...]
    conv = (cw[0:1] * cbuf_ref[SUBLANES - 3:SUBLANES - 3 + tm, :]
            + cw[1:2] * cbuf_ref[SUBLANES - 2:SUBLANES - 2 + tm, :]
            + cw[2:3] * cbuf_ref[SUBLANES - 1:SUBLANES - 1 + tm, :]
            + cw[3:4] * pre)
    cbuf_ref[0:SUBLANES, :] = cbuf_ref[tm:tm + SUBLANES, :]
    act = _silu(conv)
    for hd in range(HEADS):
        lo = hd * HEAD_DIM
        qh = act[:, lo:lo + HEAD_DIM]
        kh = act[:, D + lo:D + lo + HEAD_DIM]
        qn = qh * lax.rsqrt(jnp.sum(qh * qh, axis=-1, keepdims=True) + EPS) * (HEAD_DIM ** -0.5)
        kn = kh * lax.rsqrt(jnp.sum(kh * kh, axis=-1, keepdims=True) + EPS)
        q_ref[:, lo:lo + HEAD_DIM] = qn.astype(BF16)
        k_ref[:, lo:lo + HEAD_DIM] = kn.astype(BF16)
    v_ref[...] = act[:, 2 * D:].astype(BF16)

    ba = _dot(hb, wba_ref[...])
    misc = misc_ref[...]
    beta = jax.nn.sigmoid(ba)
    sp_in = ba + misc[1:2]
    softplus = jnp.maximum(sp_in, 0.0) + jnp.log(1.0 + jnp.exp(-jnp.abs(sp_in)))
    g = -jnp.exp(misc[0:1]) * softplus
    row = lax.broadcasted_iota(I32, (tm, tm), 0)
    col = lax.broadcasted_iota(I32, (tm, tm), 1)
    tri = jnp.where((col <= row) & ((col >> CHUNK_SHIFT) == (row >> CHUNK_SHIFT)), 1.0, 0.0)
    gc = jnp.dot(tri, g, precision=HIGHEST, preferred_element_type=F32)
    lane = lax.broadcasted_iota(I32, (tm, LANES), 1)
    gb = jnp.where(lane < HEADS, beta, gc)
    gb_ref[...] = gb
    for c in range(tm // CHUNK):
        blk = jnp.concatenate([gb[c * CHUNK:(c + 1) * CHUNK], jnp.zeros((LANES - CHUNK, LANES), F32)], axis=0)
        gt_ref[c] = blk.T[HEADS:2 * HEADS, :]


def _gdn_pre(x2, mod_l, nw8, w_in, conv_w, a_log, dt_bias):
    tm = TM_GDN
    tiles_per_seq = SEQ // tm
    qkvz = 4 * D
    w_main = w_in[:, :qkvz].astype(BF16)
    wba = jnp.pad(w_in[:, qkvz:], ((0, 0), (0, LANES - 2 * HEADS))).astype(BF16)
    cw8 = jnp.pad(conv_w, ((0, SUBLANES - conv_w.shape[0]), (0, 0)))
    misc = jnp.zeros((SUBLANES, LANES), F32)
    misc = misc.at[0, HEADS:2 * HEADS].set(a_log).at[1, HEADS:2 * HEADS].set(dt_bias)
    const = lambda i: (0, 0)
    tok = lambda i: (i, 0)
    return pl.pallas_call(
        functools.partial(_gdn_pre_kernel, tiles_per_seq=tiles_per_seq),
        grid=(N_TOK // tm,),
        in_specs=[
            pl.BlockSpec((tm, D), tok),
            pl.BlockSpec((None, SUBLANES, D), lambda i: (i // tiles_per_seq, 0, 0)),
            pl.BlockSpec((SUBLANES, D), const),
            pl.BlockSpec((D, qkvz), const),
            pl.BlockSpec((D, LANES), const),
            pl.BlockSpec((SUBLANES, 3 * D), const),
            pl.BlockSpec((SUBLANES, LANES), const),
        ],
        out_specs=[
            pl.BlockSpec((tm, D), tok),
            pl.BlockSpec((tm, D), tok),
            pl.BlockSpec((tm, D), tok),
            pl.BlockSpec((tm, D), tok),
            pl.BlockSpec((tm, LANES), tok),
            pl.BlockSpec((tm // CHUNK, HEADS, LANES), lambda i: (i, 0, 0)),
        ],
        out_shape=[
            jax.ShapeDtypeStruct((N_TOK, D), BF16),
            jax.ShapeDtypeStruct((N_TOK, D), BF16),
            jax.ShapeDtypeStruct((N_TOK, D), BF16),
            jax.ShapeDtypeStruct((N_TOK, D), BF16),
            jax.ShapeDtypeStruct((N_TOK, LANES), F32),
            jax.ShapeDtypeStruct((N_TOK // CHUNK, HEADS, LANES), F32),
        ],
        scratch_shapes=[pltpu.VMEM((tm + SUBLANES, 3 * D), F32)],
        compiler_params=_cp(("arbitrary",)),
        name="gdn_pre",
    )(x2, mod_l, nw8, w_main, wba, cw8, misc)


def _dot_nt(a, b):
    return lax.dot_general(a, b, (((1,), (1,)), ((), ())), preferred_element_type=F32)


def _dot_tn(a, b):
    return lax.dot_general(a, b, (((0,), (0,)), ((), ())), preferred_element_type=F32)


def _gdn_chunk_kernel(q_ref, k_ref, v_ref, gb_ref, gt_ref, o_ref, s_ref):
    @pl.when(pl.program_id(1) == 0)
    def _():
        s_ref[...] = jnp.zeros(s_ref.shape, F32)

    c = CHUNK
    nb = q_ref.shape[0]
    row = lax.broadcasted_iota(I32, (c, c), 0)
    col = lax.broadcasted_iota(I32, (c, c), 1)
    causal = col <= row
    strict = col < row
    eye = jnp.where(col == row, 1.0, 0.0)
    chains = [(b, hd) for b in range(nb) for hd in range(HEADS)]
    st = []
    for b, hd in chains:
        lo = hd * HEAD_DIM
        gb = gb_ref[b]
        q = q_ref[b, :, lo:lo + HEAD_DIM].astype(F32)
        k = k_ref[b, :, lo:lo + HEAD_DIM].astype(F32)
        v = v_ref[b, :, lo:lo + HEAD_DIM].astype(F32)
        beta = gb[:, hd:hd + 1]
        gcol = gb[:, HEADS + hd:HEADS + hd + 1]
        grow = gt_ref[b, hd:hd + 1, 0:c]
        glast = gcol[c - 1:c, :]
        egc = jnp.exp(gcol)
        kb = k * beta
        st.append(dict(
            decay=jnp.exp(jnp.where(causal, gcol - grow, -jnp.inf)),
            kq=jnp.concatenate([kb, q], axis=0).astype(BF16),
            kbf=k.astype(BF16),
            rhs=jnp.concatenate([v * beta, kb * egc], axis=-1).astype(BF16),
            qd=(q * egc).astype(BF16),
            kd=(k * jnp.exp(glast - gcol)).astype(BF16),
            eg=jnp.exp(glast)))
    for x in st:
        kk = _dot_nt(x["kq"], x["kbf"])
        x["p"] = jnp.where(strict, kk[:c] * x["decay"], 0.0)
        x["attn"] = jnp.where(causal, kk[c:] * x["decay"], 0.0).astype(BF16)
        x["t"] = eye - x["p"]
    for _ in range(5):
        for x in st:
            pb = x["p"].astype(BF16)
            x["p"] = _dot(pb, pb)
        for x in st:
            x["t"] = x["t"] + _dot(x["t"].astype(BF16), x["p"].astype(BF16))
    for x in st:
        x["uw"] = _dot(x["t"].astype(BF16), x["rhs"])
    for x, (b, hd) in zip(st, chains):
        s = s_ref[b * HEADS + hd]
        x["s"] = s
        ws = _dot(jnp.concatenate([x["uw"][:, HEAD_DIM:].astype(BF16), x["qd"]], axis=0), s.astype(BF16))
        x["vb"] = (x["uw"][:, :HEAD_DIM] - ws[:c]).astype(BF16)
        x["o"] = ws[c:]
    for x, (b, hd) in zip(st, chains):
        lo = hd * HEAD_DIM
        s_ref[b * HEADS + hd] = x["s"] * x["eg"] + _dot_tn(x["kd"], x["vb"])
        o_ref[b, :, lo:lo + HEAD_DIM] = (x["o"] + _dot(x["attn"], x["vb"])).astype(BF16)


def _gdn_chunks(q, k, v, gb, gt):
    nb = GDN_BATCH_PER_STEP
    n_chunks = SEQ // CHUNK
    tok = lambda b, c: (b, c, 0)
    o = pl.pallas_call(
        _gdn_chunk_kernel,
        grid=(BATCH // nb, n_chunks),
        in_specs=[
            pl.BlockSpec((nb, CHUNK, D), tok),
            pl.BlockSpec((nb, CHUNK, D), tok),
            pl.BlockSpec((nb, CHUNK, D), tok),
            pl.BlockSpec((nb, CHUNK, LANES), tok),
            pl.BlockSpec((nb, None, HEADS, LANES), lambda b, c: (b, c, 0, 0)),
        ],
        out_specs=pl.BlockSpec((nb, CHUNK, D), tok),
        out_shape=jax.ShapeDtypeStruct((BATCH, SEQ, D), BF16),
        scratch_shapes=[pltpu.VMEM((nb * HEADS, HEAD_DIM, HEAD_DIM), F32)],
        compiler_params=_cp(("arbitrary", "arbitrary")),
        name="gdn_chunks",
    )(q.reshape(BATCH, SEQ, D), k.reshape(BATCH, SEQ, D), v.reshape(BATCH, SEQ, D),
      gb.reshape(BATCH, SEQ, LANES), gt.reshape(BATCH, n_chunks, HEADS, LANES))
    return o.reshape(N_TOK, D)


def _gdn_post_kernel(o_ref, z_ref, x_ref, mod_ref, nw_ref, gnw_ref, wout_ref, out_ref):
    gnw = gnw_ref[0:1]
    parts = []
    for hd in range(HEADS):
        lo = hd * HEAD_DIM
        oh = o_ref[:, lo:lo + HEAD_DIM].astype(F32)
        zh = z_ref[:, lo:lo + HEAD_DIM].astype(F32)
        parts.append((_rms(oh) * gnw * _silu(zh)).astype(BF16))
    y = _dot(jnp.concatenate(parts, axis=-1), wout_ref[...])
    mod = mod_ref[...]
    out_ref[...] = x_ref[...] + mod[2:3] * (_rms(y) * nw_ref[1:2])


def _gdn_post(o, z, x2, mod_l, nw8, gdn_norm_w, w_out):
    tm = TM_MIX
    tiles_per_seq = SEQ // tm
    gnw8 = jnp.pad(gdn_norm_w.reshape(1, HEAD_DIM), ((0, SUBLANES - 1), (0, 0)))
    tok = lambda i: (i, 0)
    const = lambda i: (0, 0)
    return pl.pallas_call(
        _gdn_post_kernel,
        grid=(N_TOK // tm,),
        in_specs=[
            pl.BlockSpec((tm, D), tok),
            pl.BlockSpec((tm, D), tok),
            pl.BlockSpec((tm, D), tok),
            pl.BlockSpec((None, SUBLANES, D), lambda i: (i // tiles_per_seq, 0, 0)),
            pl.BlockSpec((SUBLANES, D), const),
            pl.BlockSpec((SUBLANES, HEAD_DIM), const),
            pl.BlockSpec((D, D), const),
        ],
        out_specs=pl.BlockSpec((tm, D), tok),
        out_shape=jax.ShapeDtypeStruct((N_TOK, D), F32),
        compiler_params=_cp(("arbitrary",)),
        name="gdn_post",
    )(o, z, x2, mod_l, nw8, gnw8, w_out.astype(BF16))


def kernel(x, c, ada_w, ada_b, norm_w, conv_in_w, conv_w, conv_out_w, gdn_in_w, gdn_conv_w, gdn_a_log,
           gdn_dt_bias, gdn_norm_w, gdn_out_w, moe_group_w, moe_group_b, moe_expert_w, moe_expert_b,
           moe_w_gate, moe_w_up, moe_w_down, shared_w_gate, shared_w_up, shared_w_down, shared_gate_w):
    mod = _ada_mod(c, ada_w, ada_b)
    nw8 = jnp.pad(norm_w, ((0, 0), (0, SUBLANES - norm_w.shape[1]), (0, 0)))
    x2 = x.reshape(N_TOK, D)

    def moe(x1, layer):
        return _hier_moe_block(
            x1, mod[layer], nw8[layer], layer,
            moe_group_w[layer], moe_group_b[layer], moe_expert_w[layer], moe_expert_b[layer],
            moe_w_gate, moe_w_up, moe_w_down,
            shared_w_gate[layer], shared_w_up[layer], shared_w_down[layer], shared_gate_w[layer])

    x2 = _conv_mixer(x2, mod[0], nw8[0], conv_in_w[0], conv_w[0], conv_out_w[0])
    x2 = moe(x2, 0)
    q, k, v, z, gb, gt = _gdn_pre(x2, mod[1], nw8[1], gdn_in_w[0], gdn_conv_w[0], gdn_a_log[0], gdn_dt_bias[0])
    o = _gdn_chunks(q, k, v, gb, gt)
    x2 = _gdn_post(o, z, x2, mod[1], nw8[1], gdn_norm_w[0], gdn_out_w[0])
    x2 = moe(x2, 1)
    return x2.reshape(BATCH, SEQ, D)
```

```python
import functools

import jax
import jax.numpy as jnp
from jax import lax
from jax.experimental import pallas as pl
from jax.experimental.pallas import tpu as pltpu
from jax.experimental.pallas import tpu_sc as plsc

F32 = jnp.float32
BF16 = jnp.bfloat16
I32 = jnp.int32
HIGHEST = lax.Precision.HIGHEST

D = 1024
BATCH = 4
SEQ = 4096
N_TOK = BATCH * SEQ
HEADS = 8
HEAD_DIM = 128
CHUNK = 64
CHUNK_SHIFT = 6
N_GROUPS = 8
N_EXPERTS = 64
D_EXPERT = 256
D_SHARED = 512
EPS = 1e-6
LANES = 128
SUBLANES = 8
EXPERT_LANE0 = N_GROUPS

MOE_BLOCK = 128
N_SLOTS = N_TOK * 2
N_BLOCKS = N_SLOTS // MOE_BLOCK + N_EXPERTS
P_ROWS = N_BLOCKS * MOE_BLOCK

TM_MIX = 512
TM_GDN = 256
TM_DMA = 256
TM_POS = 2048
GDN_BATCH_PER_STEP = 4
VMEM_LIMIT = 56 * 1024 * 1024


def _cp(sem):
    return pltpu.CompilerParams(dimension_semantics=sem, vmem_limit_bytes=VMEM_LIMIT)


def _rms(x):
    return x * lax.rsqrt(jnp.mean(x * x, axis=-1, keepdims=True) + EPS)


def _silu(x):
    return x * jax.nn.sigmoid(x)


def _dot(a, b):
    return jnp.dot(a, b, preferred_element_type=F32)


def _ada_kernel(c_ref, w_ref, b_ref, o_ref):
    cs = _silu(c_ref[...])
    o_ref[0] = _dot(cs.astype(BF16), w_ref[0].astype(BF16)) + b_ref[0]


def _ada_mod(c, ada_w, ada_b):
    depth = ada_w.shape[0]
    tn = 1024
    c8 = jnp.pad(c, ((0, SUBLANES - BATCH), (0, 0)))
    mod = pl.pallas_call(
        _ada_kernel,
        grid=(depth, 6 * D // tn),
        in_specs=[
            pl.BlockSpec((SUBLANES, D), lambda l, j: (0, 0)),
            pl.BlockSpec((1, D, tn), lambda l, j: (l, 0, j)),
            pl.BlockSpec((1, 1, tn), lambda l, j: (l, 0, j)),
        ],
        out_specs=pl.BlockSpec((1, SUBLANES, tn), lambda l, j: (l, 0, j)),
        out_shape=jax.ShapeDtypeStruct((depth, SUBLANES, 6 * D), F32),
        compiler_params=_cp(("arbitrary", "arbitrary")),
        name="ada_mod",
    )(c8, ada_w, ada_b.reshape(depth, 1, 6 * D))
    mod = mod[:, :BATCH].reshape(depth, BATCH, 6, D)
    return jnp.pad(mod, ((0, 0), (0, 0), (0, 2), (0, 0)))


def _conv_mixer_kernel(x_ref, mod_ref, nw_ref, win_ref, cw_ref, wout_ref, o_ref, ubuf_ref, *, tiles_per_seq):
    tm = x_ref.shape[0]
    x = x_ref[...]
    mod = mod_ref[...]
    nw = nw_ref[...]
    h = _rms(x) * nw[0:1] * (1.0 + mod[1:2]) + mod[0:1]
    bcx = _dot(h.astype(BF16), win_ref[...])
    u = bcx[:, D:2 * D] * bcx[:, 2 * D:]

    @pl.when(pl.program_id(0) % tiles_per_seq == 0)
    def _():
        ubuf_ref[0:SUBLANES, :] = jnp.zeros((SUBLANES, D), F32)

    ubuf_ref[SUBLANES:SUBLANES + tm, :] = u
    cw = cw_ref[...]
    conv = (cw[0:1] * ubuf_ref[SUBLANES - 2:SUBLANES - 2 + tm, :]
            + cw[1:2] * ubuf_ref[SUBLANES - 1:SUBLANES - 1 + tm, :]
            + cw[2:3] * u)
    ubuf_ref[0:SUBLANES, :] = ubuf_ref[tm:tm + SUBLANES, :]
    y = _dot((bcx[:, :D] * conv).astype(BF16), wout_ref[...])
    o_ref[...] = x + mod[2:3] * (_rms(y) * nw[1:2])


def _conv_mixer(x2, mod_l, nw8, w_in, conv_w, w_out):
    tm = TM_MIX
    tiles_per_seq = SEQ // tm
    cw8 = jnp.pad(conv_w, ((0, SUBLANES - conv_w.shape[0]), (0, 0)))
    return pl.pallas_call(
        functools.partial(_conv_mixer_kernel, tiles_per_seq=tiles_per_seq),
        grid=(N_TOK // tm,),
        in_specs=[
            pl.BlockSpec((tm, D), lambda i: (i, 0)),
            pl.BlockSpec((None, SUBLANES, D), lambda i: (i // tiles_per_seq, 0, 0)),
            pl.BlockSpec((SUBLANES, D), lambda i: (0, 0)),
            pl.BlockSpec((D, 3 * D), lambda i: (0, 0)),
            pl.BlockSpec((SUBLANES, D), lambda i: (0, 0)),
            pl.BlockSpec((D, D), lambda i: (0, 0)),
        ],
        out_specs=pl.BlockSpec((tm, D), lambda i: (i, 0)),
        out_shape=jax.ShapeDtypeStruct((N_TOK, D), F32),
        scratch_shapes=[pltpu.VMEM((tm + SUBLANES, D), F32)],
        compiler_params=_cp(("arbitrary",)),
        name="conv_mixer",
    )(x2, mod_l, nw8, w_in.astype(BF16), cw8, w_out.astype(BF16))


def _lane_pick(lane, mask_val_pairs):
    out = jnp.zeros(lane.shape, F32)
    for idx, val in mask_val_pairs:
        out = jnp.where(lane == float(idx), val, out)
    return out


def _moe_pre_kernel(x_ref, mod_ref, nw_ref, wr_ref, br_ref, wgu_ref, wd_ref, wsg_ref,
                    h_ref, sh_ref, route_ref, cnt_ref, carry_ref):
    tm = x_ref.shape[0]
    x = x_ref[...]
    mod = mod_ref[...]
    nw = nw_ref[...]
    h = _rms(x) * nw[2:3] * (1.0 + mod[4:5]) + mod[3:4]
    h_ref[...] = h
    hb = h.astype(BF16)

    logits = jnp.dot(h, wr_ref[...], precision=HIGHEST, preferred_element_type=F32) + br_ref[0:1]
    lane_i = lax.broadcasted_iota(I32, (tm, LANES), 1)
    lane = lane_i.astype(F32)
    neg = jnp.float32(-jnp.inf)
    is_group = lane_i < N_GROUPS
    gl = jnp.where(is_group, logits, neg)
    gmax = jnp.max(gl, axis=-1, keepdims=True)
    gsel = jnp.min(jnp.where(gl == gmax, lane, float(LANES)), axis=-1, keepdims=True)
    psel = 1.0 / jnp.sum(jnp.where(is_group, jnp.exp(logits - gmax), 0.0), axis=-1, keepdims=True)
    lane_group = ((lane_i - EXPERT_LANE0) >> 3).astype(F32)
    in_group = (lane_i >= EXPERT_LANE0) & (lane_i < EXPERT_LANE0 + N_EXPERTS) & (lane_group == gsel)
    el = jnp.where(in_group, logits, neg)
    v1 = jnp.max(el, axis=-1, keepdims=True)
    i1 = jnp.min(jnp.where(el == v1, lane, float(LANES)), axis=-1, keepdims=True)
    el2 = jnp.where(lane == i1, neg, el)
    v2 = jnp.max(el2, axis=-1, keepdims=True)
    i2 = jnp.min(jnp.where(el2 == v2, lane, float(LANES)), axis=-1, keepdims=True)
    e2 = jnp.exp(v2 - v1)
    g1 = psel / (1.0 + e2)
    g2 = psel * e2 / (1.0 + e2)

    @pl.when(pl.program_id(0) == 0)
    def _():
        carry_ref[...] = jnp.zeros(carry_ref.shape, F32)

    oh1 = lane == i1
    oh2 = lane == i2
    cnt = jnp.where(oh1 | oh2, 1.0, 0.0).astype(BF16)
    row = lax.broadcasted_iota(I32, (tm, tm), 0)
    col = lax.broadcasted_iota(I32, (tm, tm), 1)
    tri = jnp.where(col < row, 1.0, 0.0).astype(BF16)
    pre = _dot(tri, cnt) + carry_ref[0:1]
    r1 = jnp.sum(jnp.where(oh1, pre, 0.0), axis=-1, keepdims=True)
    r2 = jnp.sum(jnp.where(oh2, pre, 0.0), axis=-1, keepdims=True)
    carry_ref[...] = carry_ref[...] + jnp.sum(cnt.astype(F32), axis=0, keepdims=True)
    cnt_ref[...] = carry_ref[...]
    route_ref[...] = _lane_pick(lane, [
        (0, i1 - EXPERT_LANE0), (1, i2 - EXPERT_LANE0),
        (2, g1), (3, g2), (4, r1), (5, r2)])

    gu = _dot(hb, wgu_ref[...])
    hid = _silu(gu[:, :D_SHARED]) * gu[:, D_SHARED:]
    ys = _dot(hid.astype(BF16), wd_ref[...])
    sg = jax.nn.sigmoid(jnp.sum(h * wsg_ref[0:1], axis=-1, keepdims=True))
    sh_ref[...] = sg * ys


def _moe_pre(x1, mod_l, nw8, w_rg, b_rg, w_re, b_re, ws_gate, ws_up, ws_down, w_sg):
    tm = TM_MIX
    tiles_per_seq = SEQ // tm
    pad_l = LANES - N_GROUPS - N_EXPERTS
    wr = jnp.pad(jnp.concatenate([w_rg, w_re], axis=1), ((0, 0), (0, pad_l)))
    br = jnp.pad(jnp.concatenate([b_rg, b_re])[None, :], ((0, SUBLANES - 1), (0, pad_l)))
    wgu = jnp.concatenate([ws_gate, ws_up], axis=1).astype(BF16)
    wsg8 = jnp.pad(w_sg.reshape(1, D), ((0, SUBLANES - 1), (0, 0)))
    const = lambda i: (0, 0)
    return pl.pallas_call(
        _moe_pre_kernel,
        grid=(N_TOK // tm,),
        in_specs=[
            pl.BlockSpec((tm, D), lambda i: (i, 0)),
            pl.BlockSpec((None, SUBLANES, D), lambda i: (i // tiles_per_seq, 0, 0)),
            pl.BlockSpec((SUBLANES, D), const),
            pl.BlockSpec((D, LANES), const),
            pl.BlockSpec((SUBLANES, LANES), const),
            pl.BlockSpec((D, 2 * D_SHARED), const),
            pl.BlockSpec((D_SHARED, D), const),
            pl.BlockSpec((SUBLANES, D), const),
        ],
        out_specs=[
            pl.BlockSpec((tm, D), lambda i: (i, 0)),
            pl.BlockSpec((tm, D), lambda i: (i, 0)),
            pl.BlockSpec((tm, LANES), lambda i: (i, 0)),
            pl.BlockSpec((SUBLANES, LANES), const),
        ],
        out_shape=[
            jax.ShapeDtypeStruct((N_TOK, D), F32),
            jax.ShapeDtypeStruct((N_TOK, D), F32),
            jax.ShapeDtypeStruct((N_TOK, LANES), F32),
            jax.ShapeDtypeStruct((SUBLANES, LANES), F32),
        ],
        scratch_shapes=[pltpu.VMEM((SUBLANES, LANES), F32)],
        compiler_params=_cp(("arbitrary",)),
        name="moe_pre",
    )(x1, mod_l, nw8, wr, br, wgu, ws_down.astype(BF16), wsg8)


def _n_blocks_per_expert(cnt_row):
    return jnp.floor((cnt_row + (MOE_BLOCK - 1)) * (1.0 / MOE_BLOCK))


def _moe_pos_kernel(cnt_ref, route_ref, pos_ref, blk_ref):
    tm = route_ref.shape[0]
    nb = _n_blocks_per_expert(cnt_ref[...])
    r = lax.broadcasted_iota(I32, (LANES, LANES), 0)
    c = lax.broadcasted_iota(I32, (LANES, LANES), 1)
    excl = jnp.dot(nb, jnp.where(r < c, 1.0, 0.0), precision=HIGHEST, preferred_element_type=F32)
    pstart = excl[0:1] * MOE_BLOCK
    route = route_ref[...]
    lane = lax.broadcasted_iota(I32, (tm, LANES), 1).astype(F32)
    e1 = route[:, 0:1] + EXPERT_LANE0
    e2 = route[:, 1:2] + EXPERT_LANE0
    p1 = jnp.sum(jnp.where(lane == e1, pstart, 0.0), axis=-1, keepdims=True) + route[:, 4:5]
    p2 = jnp.sum(jnp.where(lane == e2, pstart, 0.0), axis=-1, keepdims=True) + route[:, 5:6]
    pos_ref[...] = _lane_pick(lane, [(0, p1), (1, p2)]).astype(I32)

    @pl.when(pl.program_id(0) == 0)
    def _():
        nblk = blk_ref.shape[0]
        incl = excl[0:1] + nb[0:1]
        j = lax.broadcasted_iota(I32, (nblk, LANES), 0).astype(F32)
        ln = lax.broadcasted_iota(I32, (nblk, LANES), 1)
        is_e = (ln >= EXPERT_LANE0) & (ln < EXPERT_LANE0 + N_EXPERTS)
        be = jnp.sum(jnp.where(is_e & (incl <= j), 1.0, 0.0), axis=-1, keepdims=True)
        be = jnp.minimum(be, N_EXPERTS - 1.0)
        blk_ref[...] = jnp.broadcast_to(be, (nblk, LANES)).astype(I32)


def _moe_pos(cnt, route):
    tm = TM_POS
    nblk = (N_BLOCKS + SUBLANES - 1) // SUBLANES * SUBLANES
    pos, blk = pl.pallas_call(
        _moe_pos_kernel,
        grid=(N_TOK // tm,),
        in_specs=[
            pl.BlockSpec((SUBLANES, LANES), lambda i: (0, 0)),
            pl.BlockSpec((tm, LANES), lambda i: (i, 0)),
        ],
        out_specs=[
            pl.BlockSpec((tm, LANES), lambda i: (i, 0)),
            pl.BlockSpec((nblk, LANES), lambda i: (0, 0)),
        ],
        out_shape=[
            jax.ShapeDtypeStruct((N_TOK, LANES), I32),
            jax.ShapeDtypeStruct((nblk, LANES), I32),
        ],
        compiler_params=_cp(("arbitrary",)),
        name="moe_pos",
    )(cnt, route)
    return pos[:, :2].reshape(-1), blk[:N_BLOCKS, 0]


def _row_copy(src_ref, src_row, dst_ref, dst_row, sem):
    return pltpu.make_async_copy(src_ref.at[pl.ds(src_row, 1)], dst_ref.at[pl.ds(dst_row, 1)], sem)


def _scatter_kernel(pos_ref, h_ref, buf_in_ref, buf_ref, sem):
    del buf_in_ref
    n = pos_ref.shape[0]

    def issue(s, carry):
        _row_copy(h_ref, s >> 1, buf_ref, pos_ref[s], sem).start()
        return carry

    lax.fori_loop(0, n, issue, 0)

    def drain(s, carry):
        _row_copy(h_ref, s >> 1, buf_ref, pos_ref[s], sem).wait()
        return carry

    lax.fori_loop(0, n, drain, 0)


def _moe_scatter(pos_flat, h):
    tm = TM_DMA
    buf0 = jnp.zeros((P_ROWS, D), F32)
    return pl.pallas_call(
        _scatter_kernel,
        grid=(N_TOK // tm,),
        in_specs=[
            pl.BlockSpec((2 * tm,), lambda i: (i,), memory_space=pltpu.SMEM),
            pl.BlockSpec((tm, D), lambda i: (i, 0)),
            pl.BlockSpec(memory_space=pl.ANY),
        ],
        out_specs=pl.BlockSpec(memory_space=pl.ANY),
        out_shape=jax.ShapeDtypeStruct((P_ROWS, D), F32),
        scratch_shapes=[pltpu.SemaphoreType.DMA],
        input_output_aliases={2: 0},
        compiler_params=_cp(("arbitrary",)),
        name="moe_scatter",
    )(pos_flat, h, buf0)


def _expert_kernel(be_ref, x_ref, wg_ref, wu_ref, wd_ref, o_ref, wgb_ref, wub_ref, wdb_ref):
    i = pl.program_id(0)
    prev = be_ref[jnp.maximum(i - 1, 0)]

    @pl.when((i == 0) | (be_ref[i] != prev))
    def _():
        wgb_ref[...] = wg_ref[...].astype(BF16)
        wub_ref[...] = wu_ref[...].astype(BF16)
        wdb_ref[...] = wd_ref[...].astype(BF16)

    xb = x_ref[...].astype(BF16)
    hid = _silu(_dot(xb, wgb_ref[...])) * _dot(xb, wub_ref[...])
    o_ref[...] = _dot(hid.astype(BF16), wdb_ref[...])


def _moe_experts(block_e, buf, w_gate, w_up, w_down, layer):
    bm = MOE_BLOCK
    return pl.pallas_call(
        _expert_kernel,
        grid_spec=pltpu.PrefetchScalarGridSpec(
            num_scalar_prefetch=1,
            grid=(N_BLOCKS,),
            in_specs=[
                pl.BlockSpec((bm, D), lambda i, be: (i, 0)),
                pl.BlockSpec((None, None, D, D_EXPERT), lambda i, be: (layer, be[i], 0, 0)),
                pl.BlockSpec((None, None, D, D_EXPERT), lambda i, be: (layer, be[i], 0, 0)),
                pl.BlockSpec((None, None, D_EXPERT, D), lambda i, be: (layer, be[i], 0, 0)),
            ],
            out_specs=pl.BlockSpec((bm, D), lambda i, be: (i, 0)),
            scratch_shapes=[
                pltpu.VMEM((D, D_EXPERT), BF16),
                pltpu.VMEM((D, D_EXPERT), BF16),
                pltpu.VMEM((D_EXPERT, D), BF16),
            ],
        ),
        out_shape=jax.ShapeDtypeStruct((P_ROWS, D), F32),
        compiler_params=_cp(("arbitrary",)),
        name="moe_experts",
    )(block_e, buf, w_gate, w_up, w_down)


SC_CORES = 2
SC_SUBCORES = 16
SC_WORKERS = SC_CORES * SC_SUBCORES
SC_CHUNK = 32


def _sc_gather_rows(table, idx):
    n_idx = idx.shape[0]
    width = table.shape[1]
    per_w = n_idx // SC_WORKERS
    mesh = plsc.VectorSubcoreMesh(core_axis_name="c", subcore_axis_name="s")

    def body(table_hbm, idx_hbm, out_hbm, idx_v, rows_v, sem):
        wid = lax.axis_index("s") * SC_CORES + lax.axis_index("c")
        base = wid * per_w
        pltpu.sync_copy(idx_hbm.at[pl.ds(base, per_w)], idx_v)

        @pl.loop(0, per_w // SC_CHUNK)
        def _(j):
            off = pl.multiple_of(j * SC_CHUNK, SC_CHUNK)
            pltpu.async_copy(table_hbm.at[idx_v.at[pl.ds(off, SC_CHUNK)]], rows_v, sem).wait()
            pltpu.sync_copy(rows_v, out_hbm.at[pl.ds(base + off, SC_CHUNK)])

    return pl.kernel(
        body,
        out_type=jax.ShapeDtypeStruct((n_idx, width), table.dtype),
        mesh=mesh,
        scratch_types=[
            pltpu.VMEM((per_w,), I32),
            pltpu.VMEM((SC_CHUNK, width), table.dtype),
            pltpu.SemaphoreType.DMA,
        ],
        name="sc_gather_rows",
    )(table, idx)


def _combine_kernel(y2_ref, x_ref, sh_ref, route_ref, mod_ref, nw_ref, o_ref):
    route = route_ref[...]
    y = route[:, 2:3] * y2_ref[:, :D] + route[:, 3:4] * y2_ref[:, D:] + sh_ref[...]
    mod = mod_ref[...]
    o_ref[...] = x_ref[...] + mod[5:6] * (_rms(y) * nw_ref[3:4])


def _moe_combine(pos_flat, eo, x1, sh, route, mod_l, nw8):
    tm = TM_MIX
    tiles_per_seq = SEQ // tm
    y2 = _sc_gather_rows(eo, pos_flat).reshape(N_TOK, 2 * D)
    return pl.pallas_call(
        _combine_kernel,
        grid=(N_TOK // tm,),
        in_specs=[
            pl.BlockSpec((tm, 2 * D), lambda i: (i, 0)),
            pl.BlockSpec((tm, D), lambda i: (i, 0)),
            pl.BlockSpec((tm, D), lambda i: (i, 0)),
            pl.BlockSpec((tm, LANES), lambda i: (i, 0)),
            pl.BlockSpec((None, SUBLANES, D), lambda i: (i // tiles_per_seq, 0, 0)),
            pl.BlockSpec((SUBLANES, D), lambda i: (0, 0)),
        ],
        out_specs=pl.BlockSpec((tm, D), lambda i: (i, 0)),
        out_shape=jax.ShapeDtypeStruct((N_TOK, D), F32),
        compiler_params=_cp(("arbitrary",)),
        name="moe_combine",
    )(y2, x1, sh, route, mod_l, nw8)


def _hier_moe_block(x1, mod_l, nw8, layer, w_rg, b_rg, w_re, b_re, w_gate, w_up, w_down,
                    ws_gate, ws_up, ws_down, w_sg):
    h, sh, route, cnt = _moe_pre(x1, mod_l, nw8, w_rg, b_rg, w_re, b_re, ws_gate, ws_up, ws_down, w_sg)
    pos_flat, block_e = _moe_pos(cnt, route)
    buf = _moe_scatter(pos_flat, h)
    eo = _moe_experts(block_e, buf, w_gate, w_up, w_down, layer)
    return _moe_combine(pos_flat, eo, x1, sh, route, mod_l, nw8)


def _gdn_pre_kernel(x_ref, mod_ref, nw_ref, w_ref, wba_ref, cw_ref, misc_ref,
                    q_ref, k_ref, v_ref, z_ref, gb_ref, gt_ref, cbuf_ref, *, tiles_per_seq):
    tm = x_ref.shape[0]
    qkv_w = 3 * D
    x = x_ref[...]
    mod = mod_ref[...]
    nw = nw_ref[...]
    h = _rms(x) * nw[0:1] * (1.0 + mod[1:2]) + mod[0:1]
    hb = h.astype(BF16)
    proj = _dot(hb, w_ref[...])
    z_ref[...] = proj[:, qkv_w:].astype(BF16)
    pre = proj[:, :qkv_w]

    @pl.when(pl.program_id(0) % tiles_per_seq == 0)
    def _():
        cbuf_ref[0:SUBLANES, :] = jnp.zeros((SUBLANES, qkv_w), F32)

    cbuf_ref[SUBLANES:SUBLANES + tm, :] = pre
    cw = cw_ref[...]
    conv = (cw[0:1] * cbuf_ref[SUBLANES - 3:SUBLANES - 3 + tm, :]
            + cw[1:2] * cbuf_ref[SUBLANES - 2:SUBLANES - 2 + tm, :]
            + cw[2:3] * cbuf_ref[SUBLANES - 1:SUBLANES - 1 + tm, :]
            + cw[3:4] * pre)
    cbuf_ref[0:SUBLANES, :] = cbuf_ref[tm:tm + SUBLANES, :]
    act = _silu(conv)
    for hd in range(HEADS):
        lo = hd * HEAD_DIM
        qh = act[:, lo:lo + HEAD_DIM]
        kh = act[:, D + lo:D + lo + HEAD_DIM]
        qn = qh * lax.rsqrt(jnp.sum(qh * qh, axis=-1, keepdims=True) + EPS) * (HEAD_DIM ** -0.5)
        kn = kh * lax.rsqrt(jnp.sum(kh * kh, axis=-1, keepdims=True) + EPS)
        q_ref[:, lo:lo + HEAD_DIM] = qn.astype(BF16)
        k_ref[:, lo:lo + HEAD_DIM] = kn.astype(BF16)
    v_ref[...] = act[:, 2 * D:].astype(BF16)

    ba = _dot(hb, wba_ref[...])
    misc = misc_ref[...]
    beta = jax.nn.sigmoid(ba)
    sp_in = ba + misc[1:2]
    softplus = jnp.maximum(sp_in, 0.0) + jnp.log(1.0 + jnp.exp(-jnp.abs(sp_in)))
    g = -jnp.exp(misc[0:1]) * softplus
    row = lax.broadcasted_iota(I32, (tm, tm), 0)
    col = lax.broadcasted_iota(I32, (tm, tm), 1)
    tri = jnp.where((col <= row) & ((col >> CHUNK_SHIFT) == (row >> CHUNK_SHIFT)), 1.0, 0.0)
    gc = jnp.dot(tri, g, precision=HIGHEST, preferred_element_type=F32)
    lane = lax.broadcasted_iota(I32, (tm, LANES), 1)
    gb = jnp.where(lane < HEADS, beta, gc)
    gb_ref[...] = gb
    for c in range(tm // CHUNK):
        blk = jnp.concatenate([gb[c * CHUNK:(c + 1) * CHUNK], jnp.zeros((LANES - CHUNK, LANES), F32)], axis=0)
        gt_ref[c] = blk.T[HEADS:2 * HEADS, :]


def _gdn_pre(x2, mod_l, nw8, w_in, conv_w, a_log, dt_bias):
    tm = TM_GDN
    tiles_per_seq = SEQ // tm
    qkvz = 4 * D
    w_main = w_in[:, :qkvz].astype(BF16)
    wba = jnp.pad(w_in[:, qkvz:], ((0, 0), (0, LANES - 2 * HEADS))).astype(BF16)
    cw8 = jnp.pad(conv_w, ((0, SUBLANES - conv_w.shape[0]), (0, 0)))
    misc = jnp.zeros((SUBLANES, LANES), F32)
    misc = misc.at[0, HEADS:2 * HEADS].set(a_log).at[1, HEADS:2 * HEADS].set(dt_bias)
    const = lambda i: (0, 0)
    tok = lambda i: (i, 0)
    return pl.pallas_call(
        functools.partial(_gdn_pre_kernel, tiles_per_seq=tiles_per_seq),
        grid=(N_TOK // tm,),
        in_specs=[
            pl.BlockSpec((tm, D), tok),
            pl.BlockSpec((None, SUBLANES, D), lambda i: (i // tiles_per_seq, 0, 0)),
            pl.BlockSpec((SUBLANES, D), const),
            pl.BlockSpec((D, qkvz), const),
            pl.BlockSpec((D, LANES), const),
            pl.BlockSpec((SUBLANES, 3 * D), const),
            pl.BlockSpec((SUBLANES, LANES), const),
        ],
        out_specs=[
            pl.BlockSpec((tm, D), tok),
            pl.BlockSpec((tm, D), tok),
            pl.BlockSpec((tm, D), tok),
            pl.BlockSpec((tm, D), tok),
            pl.BlockSpec((tm, LANES), tok),
            pl.BlockSpec((tm // CHUNK, HEADS, LANES), lambda i: (i, 0, 0)),
        ],
        out_shape=[
            jax.ShapeDtypeStruct((N_TOK, D), BF16),
            jax.ShapeDtypeStruct((N_TOK, D), BF16),
            jax.ShapeDtypeStruct((N_TOK, D), BF16),
            jax.ShapeDtypeStruct((N_TOK, D), BF16),
            jax.ShapeDtypeStruct((N_TOK, LANES), F32),
            jax.ShapeDtypeStruct((N_TOK // CHUNK, HEADS, LANES), F32),
        ],
        scratch_shapes=[pltpu.VMEM((tm + SUBLANES, 3 * D), F32)],
        compiler_params=_cp(("arbitrary",)),
        name="gdn_pre",
    )(x2, mod_l, nw8, w_main, wba, cw8, misc)


def _dot_nt(a, b):
    return lax.dot_general(a, b, (((1,), (1,)), ((), ())), preferred_element_type=F32)


def _dot_tn(a, b):
    return lax.dot_general(a, b, (((0,), (0,)), ((), ())), preferred_element_type=F32)


def _gdn_chunk_kernel(q_ref, k_ref, v_ref, gb_ref, gt_ref, o_ref, s_ref):
    @pl.when(pl.program_id(1) == 0)
    def _():
        s_ref[...] = jnp.zeros(s_ref.shape, F32)

    c = CHUNK
    nb = q_ref.shape[0]
    row = lax.broadcasted_iota(I32, (c, c), 0)
    col = lax.broadcasted_iota(I32, (c, c), 1)
    causal = col <= row
    strict = col < row
    eye = jnp.where(col == row, 1.0, 0.0)
    chains = [(b, hd) for b in range(nb) for hd in range(HEADS)]
    st = []
    for b, hd in chains:
        lo = hd * HEAD_DIM
        gb = gb_ref[b]
        q = q_ref[b, :, lo:lo + HEAD_DIM].astype(F32)
        k = k_ref[b, :, lo:lo + HEAD_DIM].astype(F32)
        v = v_ref[b, :, lo:lo + HEAD_DIM].astype(F32)
        beta = gb[:, hd:hd + 1]
        gcol = gb[:, HEADS + hd:HEADS + hd + 1]
        grow = gt_ref[b, hd:hd + 1, 0:c]
        glast = gcol[c - 1:c, :]
        egc = jnp.exp(gcol)
        kb = k * beta
        st.append(dict(
            decay=jnp.exp(jnp.where(causal, gcol - grow, -jnp.inf)),
            kq=jnp.concatenate([kb, q], axis=0).astype(BF16),
            kbf=k.astype(BF16),
            rhs=jnp.concatenate([v * beta, kb * egc], axis=-1).astype(BF16),
            qd=(q * egc).astype(BF16),
            kd=(k * jnp.exp(glast - gcol)).astype(BF16),
            eg=jnp.exp(glast)))
    for x in st:
        kk = _dot_nt(x["kq"], x["kbf"])
        x["p"] = jnp.where(strict, kk[:c] * x["decay"], 0.0)
        x["attn"] = jnp.where(causal, kk[c:] * x["decay"], 0.0).astype(BF16)
        x["t"] = eye - x["p"]
    for _ in range(5):
        for x in st:
            pb = x["p"].astype(BF16)
            x["p"] = _dot(pb, pb)
        for x in st:
            x["t"] = x["t"] + _dot(x["t"].astype(BF16), x["p"].astype(BF16))
    for x in st:
        x["uw"] = _dot(x["t"].astype(BF16), x["rhs"])
    for x, (b, hd) in zip(st, chains):
        s = s_ref[b * HEADS + hd]
        x["s"] = s
        ws = _dot(jnp.concatenate([x["uw"][:, HEAD_DIM:].astype(BF16), x["qd"]], axis=0), s.astype(BF16))
        x["vb"] = (x["uw"][:, :HEAD_DIM] - ws[:c]).astype(BF16)
        x["o"] = ws[c:]
    for x, (b, hd) in zip(st, chains):
        lo = hd * HEAD_DIM
        s_ref[b * HEADS + hd] = x["s"] * x["eg"] + _dot_tn(x["kd"], x["vb"])
        o_ref[b, :, lo:lo + HEAD_DIM] = (x["o"] + _dot(x["attn"], x["vb"])).astype(BF16)


def _gdn_chunks(q, k, v, gb, gt):
    nb = GDN_BATCH_PER_STEP
    n_chunks = SEQ // CHUNK
    tok = lambda b, c: (b, c, 0)
    o = pl.pallas_call(
        _gdn_chunk_kernel,
        grid=(BATCH // nb, n_chunks),
        in_specs=[
            pl.BlockSpec((nb, CHUNK, D), tok),
            pl.BlockSpec((nb, CHUNK, D), tok),
            pl.BlockSpec((nb, CHUNK, D), tok),
            pl.BlockSpec((nb, CHUNK, LANES), tok),
            pl.BlockSpec((nb, None, HEADS, LANES), lambda b, c: (b, c, 0, 0)),
        ],
        out_specs=pl.BlockSpec((nb, CHUNK, D), tok),
        out_shape=jax.ShapeDtypeStruct((BATCH, SEQ, D), BF16),
        scratch_shapes=[pltpu.VMEM((nb * HEADS, HEAD_DIM, HEAD_DIM), F32)],
        compiler_params=_cp(("arbitrary", "arbitrary")),
        name="gdn_chunks",
    )(q.reshape(BATCH, SEQ, D), k.reshape(BATCH, SEQ, D), v.reshape(BATCH, SEQ, D),
      gb.reshape(BATCH, SEQ, LANES), gt.reshape(BATCH, n_chunks, HEADS, LANES))
    return o.reshape(N_TOK, D)


def _gdn_post_kernel(o_ref, z_ref, x_ref, mod_ref, nw_ref, gnw_ref, wout_ref, out_ref):
    gnw = gnw_ref[0:1]
    parts = []
    for hd in range(HEADS):
        lo = hd * HEAD_DIM
        oh = o_ref[:, lo:lo + HEAD_DIM].astype(F32)
        zh = z_ref[:, lo:lo + HEAD_DIM].astype(F32)
        parts.append((_rms(oh) * gnw * _silu(zh)).astype(BF16))
    y = _dot(jnp.concatenate(parts, axis=-1), wout_ref[...])
    mod = mod_ref[...]
    out_ref[...] = x_ref[...] + mod[2:3] * (_rms(y) * nw_ref[1:2])


def _gdn_post(o, z, x2, mod_l, nw8, gdn_norm_w, w_out):
    tm = TM_MIX
    tiles_per_seq = SEQ // tm
    gnw8 = jnp.pad(gdn_norm_w.reshape(1, HEAD_DIM), ((0, SUBLANES - 1), (0, 0)))
    tok = lambda i: (i, 0)
    const = lambda i: (0, 0)
    return pl.pallas_call(
        _gdn_post_kernel,
        grid=(N_TOK // tm,),
        in_specs=[
            pl.BlockSpec((tm, D), tok),
            pl.BlockSpec((tm, D), tok),
            pl.BlockSpec((tm, D), tok),
            pl.BlockSpec((None, SUBLANES, D), lambda i: (i // tiles_per_seq, 0, 0)),
            pl.BlockSpec((SUBLANES, D), const),
            pl.BlockSpec((SUBLANES, HEAD_DIM), const),
            pl.BlockSpec((D, D), const),
        ],
        out_specs=pl.BlockSpec((tm, D), tok),
        out_shape=jax.ShapeDtypeStruct((N_TOK, D), F32),
        compiler_params=_cp(("arbitrary",)),
        name="gdn_post",
    )(o, z, x2, mod_l, nw8, gnw8, w_out.astype(BF16))


def kernel(x, c, ada_w, ada_b, norm_w, conv_in_w, conv_w, conv_out_w, gdn_in_w, gdn_conv_w, gdn_a_log,
           gdn_dt_bias, gdn_norm_w, gdn_out_w, moe_group_w, moe_group_b, moe_expert_w, moe_expert_b,
           moe_w_gate, moe_w_up, moe_w_down, shared_w_gate, shared_w_up, shared_w_down, shared_gate_w):
    mod = _ada_mod(c, ada_w, ada_b)
    nw8 = jnp.pad(norm_w, ((0, 0), (0, SUBLANES - norm_w.shape[1]), (0, 0)))
    x2 = x.reshape(N_TOK, D)

    def moe(x1, layer):
        return _hier_moe_block(
            x1, mod[layer], nw8[layer], layer,
            moe_group_w[layer], moe_group_b[layer], moe_expert_w[layer], moe_expert_b[layer],
            moe_w_gate, moe_w_up, moe_w_down,
            shared_w_gate[layer], shared_w_up[layer], shared_w_down[layer], shared_gate_w[layer])

    x2 = _conv_mixer(x2, mod[0], nw8[0], conv_in_w[0], conv_w[0], conv_out_w[0])
    x2 = moe(x2, 0)
    q, k, v, z, gb, gt = _gdn_pre(x2, mod[1], nw8[1], gdn_in_w[0], gdn_conv_w[0], gdn_a_log[0], gdn_dt_bias[0])
    o = _gdn_chunks(q, k, v, gb, gt)
    x2 = _gdn_post(o, z, x2, mod[1], nw8[1], gdn_norm_w[0], gdn_out_w[0])
    x2 = moe(x2, 1)
    return x2.reshape(BATCH, SEQ, D)
```

```python
import functools

import jax
import jax.numpy as jnp
from jax import lax
from jax.experimental import pallas as pl
from jax.experimental.pallas import tpu as pltpu
from jax.experimental.pallas import tpu_sc as plsc

F32 = jnp.float32
BF16 = jnp.bfloat16
I32 = jnp.int32
HIGHEST = lax.Precision.HIGHEST

D = 1024
BATCH = 4
SEQ = 4096
N_TOK = BATCH * SEQ
HEADS = 8
HEAD_DIM = 128
CHUNK = 64
CHUNK_SHIFT = 6
N_GROUPS = 8
N_EXPERTS = 64
D_EXPERT = 256
D_SHARED = 512
EPS = 1e-6
LANES = 128
SUBLANES = 8
EXPERT_LANE0 = N_GROUPS

MOE_BLOCK = 128
N_SLOTS = N_TOK * 2
N_BLOCKS = N_SLOTS // MOE_BLOCK + N_EXPERTS
P_ROWS = N_BLOCKS * MOE_BLOCK

TM_MIX = 512
TM_GDN = 256
TM_DMA = 256
TM_POS = 2048
GDN_BATCH_PER_STEP = 4
VMEM_LIMIT = 56 * 1024 * 1024


def _cp(sem):
    return pltpu.CompilerParams(dimension_semantics=sem, vmem_limit_bytes=VMEM_LIMIT)


def _rms(x):
    return x * lax.rsqrt(jnp.mean(x * x, axis=-1, keepdims=True) + EPS)


def _silu(x):
    return x * jax.nn.sigmoid(x)


def _dot(a, b):
    return jnp.dot(a, b, preferred_element_type=F32)


def _ada_kernel(c_ref, w_ref, b_ref, o_ref):
    cs = _silu(c_ref[...])
    o_ref[0] = _dot(cs.astype(BF16), w_ref[0].astype(BF16)) + b_ref[0]


def _ada_mod(c, ada_w, ada_b):
    depth = ada_w.shape[0]
    tn = 1024
    c8 = jnp.pad(c, ((0, SUBLANES - BATCH), (0, 0)))
    mod = pl.pallas_call(
        _ada_kernel,
        grid=(depth, 6 * D // tn),
        in_specs=[
            pl.BlockSpec((SUBLANES, D), lambda l, j: (0, 0)),
            pl.BlockSpec((1, D, tn), lambda l, j: (l, 0, j)),
            pl.BlockSpec((1, 1, tn), lambda l, j: (l, 0, j)),
        ],
        out_specs=pl.BlockSpec((1, SUBLANES, tn), lambda l, j: (l, 0, j)),
        out_shape=jax.ShapeDtypeStruct((depth, SUBLANES, 6 * D), F32),
        compiler_params=_cp(("arbitrary", "arbitrary")),
        name="ada_mod",
    )(c8, ada_w, ada_b.reshape(depth, 1, 6 * D))
    mod = mod[:, :BATCH].reshape(depth, BATCH, 6, D)
    return jnp.pad(mod, ((0, 0), (0, 0), (0, 2), (0, 0)))


def _conv_mixer_kernel(x_ref, mod_ref, nw_ref, win_ref, cw_ref, wout_ref, o_ref, ubuf_ref, *, tiles_per_seq):
    tm = x_ref.shape[0]
    x = x_ref[...]
    mod = mod_ref[...]
    nw = nw_ref[...]
    h = _rms(x) * nw[0:1] * (1.0 + mod[1:2]) + mod[0:1]
    bcx = _dot(h.astype(BF16), win_ref[...])
    u = bcx[:, D:2 * D] * bcx[:, 2 * D:]

    @pl.when(pl.program_id(0) % tiles_per_seq == 0)
    def _():
        ubuf_ref[0:SUBLANES, :] = jnp.zeros((SUBLANES, D), F32)

    ubuf_ref[SUBLANES:SUBLANES + tm, :] = u
    cw = cw_ref[...]
    conv = (cw[0:1] * ubuf_ref[SUBLANES - 2:SUBLANES - 2 + tm, :]
            + cw[1:2] * ubuf_ref[SUBLANES - 1:SUBLANES - 1 + tm, :]
            + cw[2:3] * u)
    ubuf_ref[0:SUBLANES, :] = ubuf_ref[tm:tm + SUBLANES, :]
    y = _dot((bcx[:, :D] * conv).astype(BF16), wout_ref[...])
    o_ref[...] = x + mod[2:3] * (_rms(y) * nw[1:2])


def _conv_mixer(x2, mod_l, nw8, w_in, conv_w, w_out):
    tm = TM_MIX
    tiles_per_seq = SEQ // tm
    cw8 = jnp.pad(conv_w, ((0, SUBLANES - conv_w.shape[0]), (0, 0)))
    return pl.pallas_call(
        functools.partial(_conv_mixer_kernel, tiles_per_seq=tiles_per_seq),
        grid=(N_TOK // tm,),
        in_specs=[
            pl.BlockSpec((tm, D), lambda i: (i, 0)),
            pl.BlockSpec((None, SUBLANES, D), lambda i: (i // tiles_per_seq, 0, 0)),
            pl.BlockSpec((SUBLANES, D), lambda i: (0, 0)),
            pl.BlockSpec((D, 3 * D), lambda i: (0, 0)),
            pl.BlockSpec((SUBLANES, D), lambda i: (0, 0)),
            pl.BlockSpec((D, D), lambda i: (0, 0)),
        ],
        out_specs=pl.BlockSpec((tm, D), lambda i: (i, 0)),
        out_shape=jax.ShapeDtypeStruct((N_TOK, D), F32),
        scratch_shapes=[pltpu.VMEM((tm + SUBLANES, D), F32)],
        compiler_params=_cp(("arbitrary",)),
        name="conv_mixer",
    )(x2, mod_l, nw8, w_in.astype(BF16), cw8, w_out.astype(BF16))


def _lane_pick(lane, mask_val_pairs):
    out = jnp.zeros(lane.shape, F32)
    for idx, val in mask_val_pairs:
        out = jnp.where(lane == float(idx), val, out)
    return out


def _moe_pre_kernel(x_ref, mod_ref, nw_ref, wr_ref, br_ref, wgu_ref, wd_ref, wsg_ref,
                    h_ref, sh_ref, route_ref, cnt_ref, carry_ref):
    tm = x_ref.shape[0]
    x = x_ref[...]
    mod = mod_ref[...]
    nw = nw_ref[...]
    h = _rms(x) * nw[2:3] * (1.0 + mod[4:5]) + mod[3:4]
    h_ref[...] = h
    hb = h.astype(BF16)

    logits = jnp.dot(h, wr_ref[...], precision=HIGHEST, preferred_element_type=F32) + br_ref[0:1]
    lane_i = lax.broadcasted_iota(I32, (tm, LANES), 1)
    lane = lane_i.astype(F32)
    neg = jnp.float32(-jnp.inf)
    is_group = lane_i < N_GROUPS
    gl = jnp.where(is_group, logits, neg)
    gmax = jnp.max(gl, axis=-1, keepdims=True)
    gsel = jnp.min(jnp.where(gl == gmax, lane, float(LANES)), axis=-1, keepdims=True)
    psel = 1.0 / jnp.sum(jnp.where(is_group, jnp.exp(logits - gmax), 0.0), axis=-1, keepdims=True)
    lane_group = ((lane_i - EXPERT_LANE0) >> 3).astype(F32)
    in_group = (lane_i >= EXPERT_LANE0) & (lane_i < EXPERT_LANE0 + N_EXPERTS) & (lane_group == gsel)
    el = jnp.where(in_group, logits, neg)
    v1 = jnp.max(el, axis=-1, keepdims=True)
    i1 = jnp.min(jnp.where(el == v1, lane, float(LANES)), axis=-1, keepdims=True)
    el2 = jnp.where(lane == i1, neg, el)
    v2 = jnp.max(el2, axis=-1, keepdims=True)
    i2 = jnp.min(jnp.where(el2 == v2, lane, float(LANES)), axis=-1, keepdims=True)
    e2 = jnp.exp(v2 - v1)
    g1 = psel / (1.0 + e2)
    g2 = psel * e2 / (1.0 + e2)

    @pl.when(pl.program_id(0) == 0)
    def _():
        carry_ref[...] = jnp.zeros(carry_ref.shape, F32)

    oh1 = lane == i1
    oh2 = lane == i2
    cnt = jnp.where(oh1 | oh2, 1.0, 0.0).astype(BF16)
    row = lax.broadcasted_iota(I32, (tm, tm), 0)
    col = lax.broadcasted_iota(I32, (tm, tm), 1)
    tri = jnp.where(col < row, 1.0, 0.0).astype(BF16)
    pre = _dot(tri, cnt) + carry_ref[0:1]
    r1 = jnp.sum(jnp.where(oh1, pre, 0.0), axis=-1, keepdims=True)
    r2 = jnp.sum(jnp.where(oh2, pre, 0.0), axis=-1, keepdims=True)
    carry_ref[...] = carry_ref[...] + jnp.sum(cnt.astype(F32), axis=0, keepdims=True)
    cnt_ref[...] = carry_ref[...]
    route_ref[...] = _lane_pick(lane, [
        (0, i1 - EXPERT_LANE0), (1, i2 - EXPERT_LANE0),
        (2, g1), (3, g2), (4, r1), (5, r2)])

    gu = _dot(hb, wgu_ref[...])
    hid = _silu(gu[:, :D_SHARED]) * gu[:, D_SHARED:]
    ys = _dot(hid.astype(BF16), wd_ref[...])
    sg = jax.nn.sigmoid(jnp.sum(h * wsg_ref[0:1], axis=-1, keepdims=True))
    sh_ref[...] = sg * ys


def _moe_pre(x1, mod_l, nw8, w_rg, b_rg, w_re, b_re, ws_gate, ws_up, ws_down, w_sg):
    tm = TM_MIX
    tiles_per_seq = SEQ // tm
    pad_l = LANES - N_GROUPS - N_EXPERTS
    wr = jnp.pad(jnp.concatenate([w_rg, w_re], axis=1), ((0, 0), (0, pad_l)))
    br = jnp.pad(jnp.concatenate([b_rg, b_re])[None, :], ((0, SUBLANES - 1), (0, pad_l)))
    wgu = jnp.concatenate([ws_gate, ws_up], axis=1).astype(BF16)
    wsg8 = jnp.pad(w_sg.reshape(1, D), ((0, SUBLANES - 1), (0, 0)))
    const = lambda i: (0, 0)
    return pl.pallas_call(
        _moe_pre_kernel,
        grid=(N_TOK // tm,),
        in_specs=[
            pl.BlockSpec((tm, D), lambda i: (i, 0)),
            pl.BlockSpec((None, SUBLANES, D), lambda i: (i // tiles_per_seq, 0, 0)),
            pl.BlockSpec((SUBLANES, D), const),
            pl.BlockSpec((D, LANES), const),
            pl.BlockSpec((SUBLANES, LANES), const),
            pl.BlockSpec((D, 2 * D_SHARED), const),
            pl.BlockSpec((D_SHARED, D), const),
            pl.BlockSpec((SUBLANES, D), const),
        ],
        out_specs=[
            pl.BlockSpec((tm, D), lambda i: (i, 0)),
            pl.BlockSpec((tm, D), lambda i: (i, 0)),
            pl.BlockSpec((tm, LANES), lambda i: (i, 0)),
            pl.BlockSpec((SUBLANES, LANES), const),
        ],
        out_shape=[
            jax.ShapeDtypeStruct((N_TOK, D), F32),
            jax.ShapeDtypeStruct((N_TOK, D), F32),
            jax.ShapeDtypeStruct((N_TOK, LANES), F32),
            jax.ShapeDtypeStruct((SUBLANES, LANES), F32),
        ],
        scratch_shapes=[pltpu.VMEM((SUBLANES, LANES), F32)],
        compiler_params=_cp(("arbitrary",)),
        name="moe_pre",
    )(x1, mod_l, nw8, wr, br, wgu, ws_down.astype(BF16), wsg8)


def _n_blocks_per_expert(cnt_row):
    return jnp.floor((cnt_row + (MOE_BLOCK - 1)) * (1.0 / MOE_BLOCK))


def _moe_pos_kernel(cnt_ref, route_ref, pos_ref, blk_ref):
    tm = route_ref.shape[0]
    nb = _n_blocks_per_expert(cnt_ref[...])
    r = lax.broadcasted_iota(I32, (LANES, LANES), 0)
    c = lax.broadcasted_iota(I32, (LANES, LANES), 1)
    excl = jnp.dot(nb, jnp.where(r < c, 1.0, 0.0), precision=HIGHEST, preferred_element_type=F32)
    pstart = excl[0:1] * MOE_BLOCK
    route = route_ref[...]
    lane = lax.broadcasted_iota(I32, (tm, LANES), 1).astype(F32)
    e1 = route[:, 0:1] + EXPERT_LANE0
    e2 = route[:, 1:2] + EXPERT_LANE0
    p1 = jnp.sum(jnp.where(lane == e1, pstart, 0.0), axis=-1, keepdims=True) + route[:, 4:5]
    p2 = jnp.sum(jnp.where(lane == e2, pstart, 0.0), axis=-1, keepdims=True) + route[:, 5:6]
    pos_ref[...] = _lane_pick(lane, [(0, p1), (1, p2)]).astype(I32)

    @pl.when(pl.program_id(0) == 0)
    def _():
        nblk = blk_ref.shape[0]
        incl = excl[0:1] + nb[0:1]
        j = lax.broadcasted_iota(I32, (nblk, LANES), 0).astype(F32)
        ln = lax.broadcasted_iota(I32, (nblk, LANES), 1)
        is_e = (ln >= EXPERT_LANE0) & (ln < EXPERT_LANE0 + N_EXPERTS)
        be = jnp.sum(jnp.where(is_e & (incl <= j), 1.0, 0.0), axis=-1, keepdims=True)
        be = jnp.minimum(be, N_EXPERTS - 1.0)
        blk_ref[...] = jnp.broadcast_to(be, (nblk, LANES)).astype(I32)


def _moe_pos(cnt, route):
    tm = TM_POS
    nblk = (N_BLOCKS + SUBLANES - 1) // SUBLANES * SUBLANES
    pos, blk = pl.pallas_call(
        _moe_pos_kernel,
        grid=(N_TOK // tm,),
        in_specs=[
            pl.BlockSpec((SUBLANES, LANES), lambda i: (0, 0)),
            pl.BlockSpec((tm, LANES), lambda i: (i, 0)),
        ],
        out_specs=[
            pl.BlockSpec((tm, LANES), lambda i: (i, 0)),
            pl.BlockSpec((nblk, LANES), lambda i: (0, 0)),
        ],
        out_shape=[
            jax.ShapeDtypeStruct((N_TOK, LANES), I32),
            jax.ShapeDtypeStruct((nblk, LANES), I32),
        ],
        compiler_params=_cp(("arbitrary",)),
        name="moe_pos",
    )(cnt, route)
    return pos[:, :2].T.reshape(-1), blk[:N_BLOCKS, 0]


def _expert_kernel(be_ref, x_ref, wg_ref, wu_ref, wd_ref, o_ref, wgb_ref, wub_ref, wdb_ref):
    i = pl.program_id(0)
    prev = be_ref[jnp.maximum(i - 1, 0)]

    @pl.when((i == 0) | (be_ref[i] != prev))
    def _():
        wgb_ref[...] = wg_ref[...].astype(BF16)
        wub_ref[...] = wu_ref[...].astype(BF16)
        wdb_ref[...] = wd_ref[...].astype(BF16)

    xb = x_ref[...].astype(BF16)
    hid = _silu(_dot(xb, wgb_ref[...])) * _dot(xb, wub_ref[...])
    o_ref[...] = _dot(hid.astype(BF16), wdb_ref[...])


def _moe_experts(block_e, buf, w_gate, w_up, w_down, layer):
    bm = MOE_BLOCK
    return pl.pallas_call(
        _expert_kernel,
        grid_spec=pltpu.PrefetchScalarGridSpec(
            num_scalar_prefetch=1,
            grid=(N_BLOCKS,),
            in_specs=[
                pl.BlockSpec((bm, D), lambda i, be: (i, 0)),
                pl.BlockSpec((None, None, D, D_EXPERT), lambda i, be: (layer, be[i], 0, 0)),
                pl.BlockSpec((None, None, D, D_EXPERT), lambda i, be: (layer, be[i], 0, 0)),
                pl.BlockSpec((None, None, D_EXPERT, D), lambda i, be: (layer, be[i], 0, 0)),
            ],
            out_specs=pl.BlockSpec((bm, D), lambda i, be: (i, 0)),
            scratch_shapes=[
                pltpu.VMEM((D, D_EXPERT), BF16),
                pltpu.VMEM((D, D_EXPERT), BF16),
                pltpu.VMEM((D_EXPERT, D), BF16),
            ],
        ),
        out_shape=jax.ShapeDtypeStruct((P_ROWS, D), F32),
        compiler_params=_cp(("arbitrary",)),
        name="moe_experts",
    )(block_e, buf, w_gate, w_up, w_down)


SC_CORES = 2
SC_SUBCORES = 16
SC_WORKERS = SC_CORES * SC_SUBCORES
SC_CHUNK = 32


def _sc_gather_rows(table, idx):
    n_idx = idx.shape[0]
    width = table.shape[1]
    per_w = n_idx // SC_WORKERS
    mesh = plsc.VectorSubcoreMesh(core_axis_name="c", subcore_axis_name="s")

    def body(table_hbm, idx_hbm, out_hbm, idx_v, rows_v, sem):
        wid = lax.axis_index("s") * SC_CORES + lax.axis_index("c")
        base = wid * per_w
        pltpu.sync_copy(idx_hbm.at[pl.ds(base, per_w)], idx_v)

        @pl.loop(0, per_w // SC_CHUNK)
        def _(j):
            off = pl.multiple_of(j * SC_CHUNK, SC_CHUNK)
            pltpu.async_copy(table_hbm.at[idx_v.at[pl.ds(off, SC_CHUNK)]], rows_v, sem).wait()
            pltpu.sync_copy(rows_v, out_hbm.at[pl.ds(base + off, SC_CHUNK)])

    return pl.kernel(
        body,
        out_type=jax.ShapeDtypeStruct((n_idx, width), table.dtype),
        mesh=mesh,
        scratch_types=[
            pltpu.VMEM((per_w,), I32),
            pltpu.VMEM((SC_CHUNK, width), table.dtype),
            pltpu.SemaphoreType.DMA,
        ],
        name="sc_gather_rows",
    )(table, idx)


def _sc_dispatch_rows(h, pos_km):
    per_w = P_ROWS // SC_WORKERS
    lanes = 16
    mesh = plsc.VectorSubcoreMesh(core_axis_name="c", subcore_axis_name="s")

    def body(h_hbm, pos_hbm, buf_hbm, pos_v, tok_v, rows_v, sem):
        wid = lax.axis_index("s") * SC_CORES + lax.axis_index("c")
        base = wid * per_w
        pltpu.sync_copy(pos_hbm, pos_v)

        @pl.loop(0, per_w // lanes)
        def _(i):
            tok_v[pl.ds(pl.multiple_of(i * lanes, lanes), lanes)] = jnp.zeros((lanes,), I32)

        @pl.loop(0, N_SLOTS // lanes)
        def _(i):
            off = pl.multiple_of(i * lanes, lanes)
            local = pos_v[pl.ds(off, lanes)] - base
            mine = (local >= 0) & (local < per_w)
            slot = off + lax.iota(I32, lanes)
            plsc.store_scatter(tok_v, [jnp.where(mine, local, 0)], slot & (N_TOK - 1), mask=mine)

        @pl.loop(0, per_w // SC_CHUNK)
        def _(j):
            off = pl.multiple_of(j * SC_CHUNK, SC_CHUNK)
            pltpu.async_copy(h_hbm.at[tok_v.at[pl.ds(off, SC_CHUNK)]], rows_v, sem).wait()
            pltpu.sync_copy(rows_v, buf_hbm.at[pl.ds(base + off, SC_CHUNK)])

    return pl.kernel(
        body,
        out_type=jax.ShapeDtypeStruct((P_ROWS, h.shape[1]), h.dtype),
        mesh=mesh,
        scratch_types=[
            pltpu.VMEM((N_SLOTS,), I32),
            pltpu.VMEM((per_w,), I32),
            pltpu.VMEM((SC_CHUNK, h.shape[1]), h.dtype),
            pltpu.SemaphoreType.DMA,
        ],
        compiler_params=pltpu.CompilerParams(needs_layout_passes=False),
        name="sc_dispatch_rows",
    )(h, pos_km)


def _combine_kernel(y0_ref, y1_ref, x_ref, sh_ref, route_ref, mod_ref, nw_ref, o_ref):
    route = route_ref[...]
    y = route[:, 2:3] * y0_ref[...] + route[:, 3:4] * y1_ref[...] + sh_ref[...]
    mod = mod_ref[...]
    o_ref[...] = x_ref[...] + mod[5:6] * (_rms(y) * nw_ref[3:4])


def _moe_combine(pos_km, eo, x1, sh, route, mod_l, nw8):
    tm = TM_MIX
    tiles_per_seq = SEQ // tm
    n_tiles = N_TOK // tm
    y2 = _sc_gather_rows(eo, pos_km)
    return pl.pallas_call(
        _combine_kernel,
        grid=(n_tiles,),
        in_specs=[
            pl.BlockSpec((tm, D), lambda i: (i, 0)),
            pl.BlockSpec((tm, D), lambda i: (i + n_tiles, 0)),
            pl.BlockSpec((tm, D), lambda i: (i, 0)),
            pl.BlockSpec((tm, D), lambda i: (i, 0)),
            pl.BlockSpec((tm, LANES), lambda i: (i, 0)),
            pl.BlockSpec((None, SUBLANES, D), lambda i: (i // tiles_per_seq, 0, 0)),
            pl.BlockSpec((SUBLANES, D), lambda i: (0, 0)),
        ],
        out_specs=pl.BlockSpec((tm, D), lambda i: (i, 0)),
        out_shape=jax.ShapeDtypeStruct((N_TOK, D), F32),
        compiler_params=_cp(("arbitrary",)),
        name="moe_combine",
    )(y2, y2, x1, sh, route, mod_l, nw8)


def _hier_moe_block(x1, mod_l, nw8, layer, w_rg, b_rg, w_re, b_re, w_gate, w_up, w_down,
                    ws_gate, ws_up, ws_down, w_sg):
    h, sh, route, cnt = _moe_pre(x1, mod_l, nw8, w_rg, b_rg, w_re, b_re, ws_gate, ws_up, ws_down, w_sg)
    pos_km, block_e = _moe_pos(cnt, route)
    buf = _sc_dispatch_rows(h, pos_km)
    eo = _moe_experts(block_e, buf, w_gate, w_up, w_down, layer)
    return _moe_combine(pos_km, eo, x1, sh, route, mod_l, nw8)


def _gdn_pre_kernel(x_ref, mod_ref, nw_ref, w_ref, wba_ref, cw_ref, misc_ref,
                    q_ref, k_ref, v_ref, z_ref, gb_ref, gt_ref, cbuf_ref, *, tiles_per_seq):
    tm = x_ref.shape[0]
    qkv_w = 3 * D
    x = x_ref[...]
    mod = mod_ref[...]
    nw = nw_ref[...]
    h = _rms(x) * nw[0:1] * (1.0 + mod[1:2]) + mod[0:1]
    hb = h.astype(BF16)
    proj = _dot(hb, w_ref[...])
    z_ref[...] = proj[:, qkv_w:].astype(BF16)
    pre = proj[:, :qkv_w]

    @pl.when(pl.program_id(0) % tiles_per_seq == 0)
    def _():
        cbuf_ref[0:SUBLANES, :] = jnp.zeros((SUBLANES, qkv_w), F32)

    cbuf_ref[SUBLANES:SUBLANES + tm, :] = pre
    cw = cw_ref[...]
    conv = (cw[0:1] * cbuf_ref[SUBLANES - 3:SUBLANES - 3 + tm, :]
            + cw[1:2] * cbuf_ref[SUBLANES - 2:SUBLANES - 2 + tm, :]
            + cw[2:3] * cbuf_ref[SUBLANES - 1:SUBLANES - 1 + tm, :]
            + cw[3:4] * pre)
    cbuf_ref[0:SUBLANES, :] = cbuf_ref[tm:tm + SUBLANES, :]
    act = _silu(conv)
    for hd in range(HEADS):
        lo = hd * HEAD_DIM
        qh = act[:, lo:lo + HEAD_DIM]
        kh = act[:, D + lo:D + lo + HEAD_DIM]
        qn = qh * lax.rsqrt(jnp.sum(qh * qh, axis=-1, keepdims=True) + EPS) * (HEAD_DIM ** -0.5)
        kn = kh * lax.rsqrt(jnp.sum(kh * kh, axis=-1, keepdims=True) + EPS)
        q_ref[:, lo:lo + HEAD_DIM] = qn.astype(BF16)
        k_ref[:, lo:lo + HEAD_DIM] = kn.astype(BF16)
    v_ref[...] = act[:, 2 * D:].astype(BF16)

    ba = _dot(hb, wba_ref[...])
    misc = misc_ref[...]
    beta = jax.nn.sigmoid(ba)
    sp_in = ba + misc[1:2]
    softplus = jnp.maximum(sp_in, 0.0) + jnp.log(1.0 + jnp.exp(-jnp.abs(sp_in)))
    g = -jnp.exp(misc[0:1]) * softplus
    row = lax.broadcasted_iota(I32, (tm, tm), 0)
    col = lax.broadcasted_iota(I32, (tm, tm), 1)
    tri = jnp.where((col <= row) & ((col >> CHUNK_SHIFT) == (row >> CHUNK_SHIFT)), 1.0, 0.0)
    gc = jnp.dot(tri, g, precision=HIGHEST, preferred_element_type=F32)
    lane = lax.broadcasted_iota(I32, (tm, LANES), 1)
    gb = jnp.where(lane < HEADS, beta, gc)
    gb_ref[...] = gb
    for c in range(tm // CHUNK):
        blk = jnp.concatenate([gb[c * CHUNK:(c + 1) * CHUNK], jnp.zeros((LANES - CHUNK, LANES), F32)], axis=0)
        gt_ref[c] = blk.T[HEADS:2 * HEADS, :]


def _gdn_pre(x2, mod_l, nw8, w_in, conv_w, a_log, dt_bias):
    tm = TM_GDN
    tiles_per_seq = SEQ // tm
    qkvz = 4 * D
    w_main = w_in[:, :qkvz].astype(BF16)
    wba = jnp.pad(w_in[:, qkvz:], ((0, 0), (0, LANES - 2 * HEADS))).astype(BF16)
    cw8 = jnp.pad(conv_w, ((0, SUBLANES - conv_w.shape[0]), (0, 0)))
    misc = jnp.zeros((SUBLANES, LANES), F32)
    misc = misc.at[0, HEADS:2 * HEADS].set(a_log).at[1, HEADS:2 * HEADS].set(dt_bias)
    const = lambda i: (0, 0)
    tok = lambda i: (i, 0)
    return pl.pallas_call(
        functools.partial(_gdn_pre_kernel, tiles_per_seq=tiles_per_seq),
        grid=(N_TOK // tm,),
        in_specs=[
            pl.BlockSpec((tm, D), tok),
            pl.BlockSpec((None, SUBLANES, D), lambda i: (i // tiles_per_seq, 0, 0)),
            pl.BlockSpec((SUBLANES, D), const),
            pl.BlockSpec((D, qkvz), const),
            pl.BlockSpec((D, LANES), const),
            pl.BlockSpec((SUBLANES, 3 * D), const),
            pl.BlockSpec((SUBLANES, LANES), const),
        ],
        out_specs=[
            pl.BlockSpec((tm, D), tok),
            pl.BlockSpec((tm, D), tok),
            pl.BlockSpec((tm, D), tok),
            pl.BlockSpec((tm, D), tok),
            pl.BlockSpec((tm, LANES), tok),
            pl.BlockSpec((tm // CHUNK, HEADS, LANES), lambda i: (i, 0, 0)),
        ],
        out_shape=[
            jax.ShapeDtypeStruct((N_TOK, D), BF16),
            jax.ShapeDtypeStruct((N_TOK, D), BF16),
            jax.ShapeDtypeStruct((N_TOK, D), BF16),
            jax.ShapeDtypeStruct((N_TOK, D), BF16),
            jax.ShapeDtypeStruct((N_TOK, LANES), F32),
            jax.ShapeDtypeStruct((N_TOK // CHUNK, HEADS, LANES), F32),
        ],
        scratch_shapes=[pltpu.VMEM((tm + SUBLANES, 3 * D), F32)],
        compiler_params=_cp(("arbitrary",)),
        name="gdn_pre",
    )(x2, mod_l, nw8, w_main, wba, cw8, misc)


def _dot_nt(a, b):
    return lax.dot_general(a, b, (((1,), (1,)), ((), ())), preferred_element_type=F32)


def _dot_tn(a, b):
    return lax.dot_general(a, b, (((0,), (0,)), ((), ())), preferred_element_type=F32)


def _gdn_chunk_kernel(q_ref, k_ref, v_ref, gb_ref, gt_ref, o_ref, s_ref):
    @pl.when(pl.program_id(1) == 0)
    def _():
        s_ref[...] = jnp.zeros(s_ref.shape, F32)

    c = CHUNK
    nb = q_ref.shape[0]
    row = lax.broadcasted_iota(I32, (c, c), 0)
    col = lax.broadcasted_iota(I32, (c, c), 1)
    causal = col <= row
    strict = col < row
    eye = jnp.where(col == row, 1.0, 0.0)
    chains = [(b, hd) for b in range(nb) for hd in range(HEADS)]
    st = []
    for b, hd in chains:
        lo = hd * HEAD_DIM
        gb = gb_ref[b]
        q = q_ref[b, :, lo:lo + HEAD_DIM].astype(F32)
        k = k_ref[b, :, lo:lo + HEAD_DIM].astype(F32)
        v = v_ref[b, :, lo:lo + HEAD_DIM].astype(F32)
        beta = gb[:, hd:hd + 1]
        gcol = gb[:, HEADS + hd:HEADS + hd + 1]
        grow = gt_ref[b, hd:hd + 1, 0:c]
        glast = gcol[c - 1:c, :]
        egc = jnp.exp(gcol)
        kb = k * beta
        st.append(dict(
            decay=jnp.exp(jnp.where(causal, gcol - grow, -jnp.inf)),
            kq=jnp.concatenate([kb, q], axis=0).astype(BF16),
            kbf=k.astype(BF16),
            rhs=jnp.concatenate([v * beta, kb * egc], axis=-1).astype(BF16),
            qd=(q * egc).astype(BF16),
            kd=(k * jnp.exp(glast - gcol)).astype(BF16),
            eg=jnp.exp(glast)))
    for x in st:
        kk = _dot_nt(x["kq"], x["kbf"])
        x["p"] = jnp.where(strict, kk[:c] * x["decay"], 0.0)
        x["attn"] = jnp.where(causal, kk[c:] * x["decay"], 0.0).astype(BF16)
        x["t"] = eye - x["p"]
    for _ in range(5):
        for x in st:
            pb = x["p"].astype(BF16)
            x["p"] = _dot(pb, pb)
        for x in st:
            x["t"] = x["t"] + _dot(x["t"].astype(BF16), x["p"].astype(BF16))
    for x in st:
        x["uw"] = _dot(x["t"].astype(BF16), x["rhs"])
    for x, (b, hd) in zip(st, chains):
        s = s_ref[b * HEADS + hd]
        x["s"] = s
        ws = _dot(jnp.concatenate([x["uw"][:, HEAD_DIM:].astype(BF16), x["qd"]], axis=0), s.astype(BF16))
        x["vb"] = (x["uw"][:, :HEAD_DIM] - ws[:c]).astype(BF16)
        x["o"] = ws[c:]
    for x, (b, hd) in zip(st, chains):
        lo = hd * HEAD_DIM
        s_ref[b * HEADS + hd] = x["s"] * x["eg"] + _dot_tn(x["kd"], x["vb"])
        o_ref[b, :, lo:lo + HEAD_DIM] = (x["o"] + _dot(x["attn"], x["vb"])).astype(BF16)


def _gdn_chunks(q, k, v, gb, gt):
    nb = GDN_BATCH_PER_STEP
    n_chunks = SEQ // CHUNK
    tok = lambda b, c: (b, c, 0)
    o = pl.pallas_call(
        _gdn_chunk_kernel,
        grid=(BATCH // nb, n_chunks),
        in_specs=[
            pl.BlockSpec((nb, CHUNK, D), tok),
            pl.BlockSpec((nb, CHUNK, D), tok),
            pl.BlockSpec((nb, CHUNK, D), tok),
            pl.BlockSpec((nb, CHUNK, LANES), tok),
            pl.BlockSpec((nb, None, HEADS, LANES), lambda b, c: (b, c, 0, 0)),
        ],
        out_specs=pl.BlockSpec((nb, CHUNK, D), tok),
        out_shape=jax.ShapeDtypeStruct((BATCH, SEQ, D), BF16),
        scratch_shapes=[pltpu.VMEM((nb * HEADS, HEAD_DIM, HEAD_DIM), F32)],
        compiler_params=_cp(("arbitrary", "arbitrary")),
        name="gdn_chunks",
    )(q.reshape(BATCH, SEQ, D), k.reshape(BATCH, SEQ, D), v.reshape(BATCH, SEQ, D),
      gb.reshape(BATCH, SEQ, LANES), gt.reshape(BATCH, n_chunks, HEADS, LANES))
    return o.reshape(N_TOK, D)


def _gdn_post_kernel(o_ref, z_ref, x_ref, mod_ref, nw_ref, gnw_ref, wout_ref, out_ref):
    gnw = gnw_ref[0:1]
    parts = []
    for hd in range(HEADS):
        lo = hd * HEAD_DIM
        oh = o_ref[:, lo:lo + HEAD_DIM].astype(F32)
        zh = z_ref[:, lo:lo + HEAD_DIM].astype(F32)
        parts.append((_rms(oh) * gnw * _silu(zh)).astype(BF16))
    y = _dot(jnp.concatenate(parts, axis=-1), wout_ref[...])
    mod = mod_ref[...]
    out_ref[...] = x_ref[...] + mod[2:3] * (_rms(y) * nw_ref[1:2])


def _gdn_post(o, z, x2, mod_l, nw8, gdn_norm_w, w_out):
    tm = TM_MIX
    tiles_per_seq = SEQ // tm
    gnw8 = jnp.pad(gdn_norm_w.reshape(1, HEAD_DIM), ((0, SUBLANES - 1), (0, 0)))
    tok = lambda i: (i, 0)
    const = lambda i: (0, 0)
    return pl.pallas_call(
        _gdn_post_kernel,
        grid=(N_TOK // tm,),
        in_specs=[
            pl.BlockSpec((tm, D), tok),
            pl.BlockSpec((tm, D), tok),
            pl.BlockSpec((tm, D), tok),
            pl.BlockSpec((None, SUBLANES, D), lambda i: (i // tiles_per_seq, 0, 0)),
            pl.BlockSpec((SUBLANES, D), const),
            pl.BlockSpec((SUBLANES, HEAD_DIM), const),
            pl.BlockSpec((D, D), const),
        ],
        out_specs=pl.BlockSpec((tm, D), tok),
        out_shape=jax.ShapeDtypeStruct((N_TOK, D), F32),
        compiler_params=_cp(("arbitrary",)),
        name="gdn_post",
    )(o, z, x2, mod_l, nw8, gnw8, w_out.astype(BF16))


def kernel(x, c, ada_w, ada_b, norm_w, conv_in_w, conv_w, conv_out_w, gdn_in_w, gdn_conv_w, gdn_a_log,
           gdn_dt_bias, gdn_norm_w, gdn_out_w, moe_group_w, moe_group_b, moe_expert_w, moe_expert_b,
           moe_w_gate, moe_w_up, moe_w_down, shared_w_gate, shared_w_up, shared_w_down, shared_gate_w):
    mod = _ada_mod(c, ada_w, ada_b)
    nw8 = jnp.pad(norm_w, ((0, 0), (0, SUBLANES - norm_w.shape[1]), (0, 0)))
    x2 = x.reshape(N_TOK, D)

    def moe(x1, layer):
        return _hier_moe_block(
            x1, mod[layer], nw8[layer], layer,
            moe_group_w[layer], moe_group_b[layer], moe_expert_w[layer], moe_expert_b[layer],
            moe_w_gate, moe_w_up, moe_w_down,
            shared_w_gate[layer], shared_w_up[layer], shared_w_down[layer], shared_gate_w[layer])

    x2 = _conv_mixer(x2, mod[0], nw8[0], conv_in_w[0], conv_w[0], conv_out_w[0])
    x2 = moe(x2, 0)
    q, k, v, z, gb, gt = _gdn_pre(x2, mod[1], nw8[1], gdn_in_w[0], gdn_conv_w[0], gdn_a_log[0], gdn_dt_bias[0])
    o = _gdn_chunks(q, k, v, gb, gt)
    x2 = _gdn_post(o, z, x2, mod[1], nw8[1], gdn_norm_w[0], gdn_out_w[0])
    x2 = moe(x2, 1)
    return x2.reshape(BATCH, SEQ, D)
```

```python
import functools

import jax
import jax.numpy as jnp
from jax import lax
from jax.experimental import pallas as pl
from jax.experimental.pallas import tpu as pltpu
from jax.experimental.pallas import tpu_sc as plsc

F32 = jnp.float32
BF16 = jnp.bfloat16
I32 = jnp.int32
HIGHEST = lax.Precision.HIGHEST

D = 1024
BATCH = 4
SEQ = 4096
N_TOK = BATCH * SEQ
HEADS = 8
HEAD_DIM = 128
CHUNK = 64
CHUNK_SHIFT = 6
N_GROUPS = 8
N_EXPERTS = 64
D_EXPERT = 256
D_SHARED = 512
EPS = 1e-6
LANES = 128
SUBLANES = 8
EXPERT_LANE0 = N_GROUPS

MOE_BLOCK = 128
N_SLOTS = N_TOK * 2
N_BLOCKS = N_SLOTS // MOE_BLOCK + N_EXPERTS
P_ROWS = N_BLOCKS * MOE_BLOCK

TM_MIX = 512
TM_GDN = 256
TM_DMA = 256
TM_POS = 2048
GDN_BATCH_PER_STEP = 4
VMEM_LIMIT = 56 * 1024 * 1024


def _cp(sem):
    return pltpu.CompilerParams(dimension_semantics=sem, vmem_limit_bytes=VMEM_LIMIT)


def _rms(x):
    return x * lax.rsqrt(jnp.mean(x * x, axis=-1, keepdims=True) + EPS)


def _silu(x):
    return x * jax.nn.sigmoid(x)


def _dot(a, b):
    return jnp.dot(a, b, preferred_element_type=F32)


def _ada_kernel(c_ref, w_ref, b_ref, o_ref):
    cs = _silu(c_ref[...])
    o_ref[0] = _dot(cs.astype(BF16), w_ref[0].astype(BF16)) + b_ref[0]


def _ada_mod(c, ada_w, ada_b):
    depth = ada_w.shape[0]
    tn = 1024
    c8 = jnp.pad(c, ((0, SUBLANES - BATCH), (0, 0)))
    mod = pl.pallas_call(
        _ada_kernel,
        grid=(depth, 6 * D // tn),
        in_specs=[
            pl.BlockSpec((SUBLANES, D), lambda l, j: (0, 0)),
            pl.BlockSpec((1, D, tn), lambda l, j: (l, 0, j)),
            pl.BlockSpec((1, 1, tn), lambda l, j: (l, 0, j)),
        ],
        out_specs=pl.BlockSpec((1, SUBLANES, tn), lambda l, j: (l, 0, j)),
        out_shape=jax.ShapeDtypeStruct((depth, SUBLANES, 6 * D), F32),
        compiler_params=_cp(("arbitrary", "arbitrary")),
        name="ada_mod",
    )(c8, ada_w, ada_b.reshape(depth, 1, 6 * D))
    mod = mod[:, :BATCH].reshape(depth, BATCH, 6, D)
    return jnp.pad(mod, ((0, 0), (0, 0), (0, 2), (0, 0)))


def _conv_mixer_kernel(x_ref, mod_ref, nw_ref, win_ref, cw_ref, wout_ref, o_ref, ubuf_ref, *, tiles_per_seq):
    tm = x_ref.shape[0]
    x = x_ref[...]
    mod = mod_ref[...]
    nw = nw_ref[...]
    h = _rms(x) * nw[0:1] * (1.0 + mod[1:2]) + mod[0:1]
    bcx = _dot(h.astype(BF16), win_ref[...])
    u = bcx[:, D:2 * D] * bcx[:, 2 * D:]

    @pl.when(pl.program_id(0) % tiles_per_seq == 0)
    def _():
        ubuf_ref[0:SUBLANES, :] = jnp.zeros((SUBLANES, D), F32)

    ubuf_ref[SUBLANES:SUBLANES + tm, :] = u
    cw = cw_ref[...]
    conv = (cw[0:1] * ubuf_ref[SUBLANES - 2:SUBLANES - 2 + tm, :]
            + cw[1:2] * ubuf_ref[SUBLANES - 1:SUBLANES - 1 + tm, :]
            + cw[2:3] * u)
    ubuf_ref[0:SUBLANES, :] = ubuf_ref[tm:tm + SUBLANES, :]
    y = _dot((bcx[:, :D] * conv).astype(BF16), wout_ref[...])
    o_ref[...] = x + mod[2:3] * (_rms(y) * nw[1:2])


def _conv_mixer(x2, mod_l, nw8, w_in, conv_w, w_out):
    tm = TM_MIX
    tiles_per_seq = SEQ // tm
    cw8 = jnp.pad(conv_w, ((0, SUBLANES - conv_w.shape[0]), (0, 0)))
    return pl.pallas_call(
        functools.partial(_conv_mixer_kernel, tiles_per_seq=tiles_per_seq),
        grid=(N_TOK // tm,),
        in_specs=[
            pl.BlockSpec((tm, D), lambda i: (i, 0)),
            pl.BlockSpec((None, SUBLANES, D), lambda i: (i // tiles_per_seq, 0, 0)),
            pl.BlockSpec((SUBLANES, D), lambda i: (0, 0)),
            pl.BlockSpec((D, 3 * D), lambda i: (0, 0)),
            pl.BlockSpec((SUBLANES, D), lambda i: (0, 0)),
            pl.BlockSpec((D, D), lambda i: (0, 0)),
        ],
        out_specs=pl.BlockSpec((tm, D), lambda i: (i, 0)),
        out_shape=jax.ShapeDtypeStruct((N_TOK, D), F32),
        scratch_shapes=[pltpu.VMEM((tm + SUBLANES, D), F32)],
        compiler_params=_cp(("arbitrary",)),
        name="conv_mixer",
    )(x2, mod_l, nw8, w_in.astype(BF16), cw8, w_out.astype(BF16))


def _lane_pick(lane, mask_val_pairs):
    out = jnp.zeros(lane.shape, F32)
    for idx, val in mask_val_pairs:
        out = jnp.where(lane == float(idx), val, out)
    return out


def _moe_pre_kernel(x_ref, mod_ref, nw_ref, wr_ref, br_ref, wgu_ref, wd_ref, wsg_ref,
                    h_ref, sh_ref, route_ref, cnt_ref, carry_ref):
    tm = x_ref.shape[0]
    x = x_ref[...]
    mod = mod_ref[...]
    nw = nw_ref[...]
    h = _rms(x) * nw[2:3] * (1.0 + mod[4:5]) + mod[3:4]
    h_ref[...] = h
    hb = h.astype(BF16)

    logits = jnp.dot(h, wr_ref[...], precision=HIGHEST, preferred_element_type=F32) + br_ref[0:1]
    lane_i = lax.broadcasted_iota(I32, (tm, LANES), 1)
    lane = lane_i.astype(F32)
    neg = jnp.float32(-jnp.inf)
    is_group = lane_i < N_GROUPS
    gl = jnp.where(is_group, logits, neg)
    gmax = jnp.max(gl, axis=-1, keepdims=True)
    gsel = jnp.min(jnp.where(gl == gmax, lane, float(LANES)), axis=-1, keepdims=True)
    psel = 1.0 / jnp.sum(jnp.where(is_group, jnp.exp(logits - gmax), 0.0), axis=-1, keepdims=True)
    lane_group = ((lane_i - EXPERT_LANE0) >> 3).astype(F32)
    in_group = (lane_i >= EXPERT_LANE0) & (lane_i < EXPERT_LANE0 + N_EXPERTS) & (lane_group == gsel)
    el = jnp.where(in_group, logits, neg)
    v1 = jnp.max(el, axis=-1, keepdims=True)
    i1 = jnp.min(jnp.where(el == v1, lane, float(LANES)), axis=-1, keepdims=True)
    el2 = jnp.where(lane == i1, neg, el)
    v2 = jnp.max(el2, axis=-1, keepdims=True)
    i2 = jnp.min(jnp.where(el2 == v2, lane, float(LANES)), axis=-1, keepdims=True)
    e2 = jnp.exp(v2 - v1)
    g1 = psel / (1.0 + e2)
    g2 = psel * e2 / (1.0 + e2)

    @pl.when(pl.program_id(0) == 0)
    def _():
        carry_ref[...] = jnp.zeros(carry_ref.shape, F32)

    oh1 = lane == i1
    oh2 = lane == i2
    cnt = jnp.where(oh1 | oh2, 1.0, 0.0).astype(BF16)
    row = lax.broadcasted_iota(I32, (tm, tm), 0)
    col = lax.broadcasted_iota(I32, (tm, tm), 1)
    tri = jnp.where(col < row, 1.0, 0.0).astype(BF16)
    pre = _dot(tri, cnt) + carry_ref[0:1]
    r1 = jnp.sum(jnp.where(oh1, pre, 0.0), axis=-1, keepdims=True)
    r2 = jnp.sum(jnp.where(oh2, pre, 0.0), axis=-1, keepdims=True)
    carry_ref[...] = carry_ref[...] + jnp.sum(cnt.astype(F32), axis=0, keepdims=True)
    cnt_ref[...] = carry_ref[...]
    route_ref[...] = _lane_pick(lane, [
        (0, i1 - EXPERT_LANE0), (1, i2 - EXPERT_LANE0),
        (2, g1), (3, g2), (4, r1), (5, r2)])

    gu = _dot(hb, wgu_ref[...])
    hid = _silu(gu[:, :D_SHARED]) * gu[:, D_SHARED:]
    ys = _dot(hid.astype(BF16), wd_ref[...])
    sg = jax.nn.sigmoid(jnp.sum(h * wsg_ref[0:1], axis=-1, keepdims=True))
    sh_ref[...] = sg * ys


def _moe_pre(x1, mod_l, nw8, w_rg, b_rg, w_re, b_re, ws_gate, ws_up, ws_down, w_sg):
    tm = TM_MIX
    tiles_per_seq = SEQ // tm
    pad_l = LANES - N_GROUPS - N_EXPERTS
    wr = jnp.pad(jnp.concatenate([w_rg, w_re], axis=1), ((0, 0), (0, pad_l)))
    br = jnp.pad(jnp.concatenate([b_rg, b_re])[None, :], ((0, SUBLANES - 1), (0, pad_l)))
    wgu = jnp.concatenate([ws_gate, ws_up], axis=1).astype(BF16)
    wsg8 = jnp.pad(w_sg.reshape(1, D), ((0, SUBLANES - 1), (0, 0)))
    const = lambda i: (0, 0)
    return pl.pallas_call(
        _moe_pre_kernel,
        grid=(N_TOK // tm,),
        in_specs=[
            pl.BlockSpec((tm, D), lambda i: (i, 0)),
            pl.BlockSpec((None, SUBLANES, D), lambda i: (i // tiles_per_seq, 0, 0)),
            pl.BlockSpec((SUBLANES, D), const),
            pl.BlockSpec((D, LANES), const),
            pl.BlockSpec((SUBLANES, LANES), const),
            pl.BlockSpec((D, 2 * D_SHARED), const),
            pl.BlockSpec((D_SHARED, D), const),
            pl.BlockSpec((SUBLANES, D), const),
        ],
        out_specs=[
            pl.BlockSpec((tm, D), lambda i: (i, 0)),
            pl.BlockSpec((tm, D), lambda i: (i, 0)),
            pl.BlockSpec((tm, LANES), lambda i: (i, 0)),
            pl.BlockSpec((SUBLANES, LANES), const),
        ],
        out_shape=[
            jax.ShapeDtypeStruct((N_TOK, D), F32),
            jax.ShapeDtypeStruct((N_TOK, D), F32),
            jax.ShapeDtypeStruct((N_TOK, LANES), F32),
            jax.ShapeDtypeStruct((SUBLANES, LANES), F32),
        ],
        scratch_shapes=[pltpu.VMEM((SUBLANES, LANES), F32)],
        compiler_params=_cp(("arbitrary",)),
        name="moe_pre",
    )(x1, mod_l, nw8, wr, br, wgu, ws_down.astype(BF16), wsg8)


def _n_blocks_per_expert(cnt_row):
    return jnp.floor((cnt_row + (MOE_BLOCK - 1)) * (1.0 / MOE_BLOCK))


def _moe_pos_kernel(cnt_ref, route_ref, pos_ref, seg_ref):
    tm = route_ref.shape[0]
    nb = _n_blocks_per_expert(cnt_ref[...])
    r = lax.broadcasted_iota(I32, (LANES, LANES), 0)
    c = lax.broadcasted_iota(I32, (LANES, LANES), 1)
    excl = jnp.dot(nb, jnp.where(r < c, 1.0, 0.0), precision=HIGHEST, preferred_element_type=F32)
    pstart = excl[0:1] * MOE_BLOCK
    route = route_ref[...]
    lane = lax.broadcasted_iota(I32, (tm, LANES), 1).astype(F32)
    e1 = route[:, 0:1] + EXPERT_LANE0
    e2 = route[:, 1:2] + EXPERT_LANE0
    p1 = jnp.sum(jnp.where(lane == e1, pstart, 0.0), axis=-1, keepdims=True) + route[:, 4:5]
    p2 = jnp.sum(jnp.where(lane == e2, pstart, 0.0), axis=-1, keepdims=True) + route[:, 5:6]
    pos_ref[...] = _lane_pick(lane, [(0, p1), (1, p2)]).astype(I32)

    @pl.when(pl.program_id(0) == 0)
    def _():
        sub = lax.broadcasted_iota(I32, (SUBLANES, LANES), 0)
        seg_ref[...] = jnp.where(sub == 0, nb, excl).astype(I32)


def _moe_pos(cnt, route):
    tm = TM_POS
    pos, seg = pl.pallas_call(
        _moe_pos_kernel,
        grid=(N_TOK // tm,),
        in_specs=[
            pl.BlockSpec((SUBLANES, LANES), lambda i: (0, 0)),
            pl.BlockSpec((tm, LANES), lambda i: (i, 0)),
        ],
        out_specs=[
            pl.BlockSpec((tm, LANES), lambda i: (i, 0)),
            pl.BlockSpec((SUBLANES, LANES), lambda i: (0, 0)),
        ],
        out_shape=[
            jax.ShapeDtypeStruct((N_TOK, LANES), I32),
            jax.ShapeDtypeStruct((SUBLANES, LANES), I32),
        ],
        compiler_params=_cp(("arbitrary",)),
        name="moe_pos",
    )(cnt, route)
    experts = slice(EXPERT_LANE0, EXPERT_LANE0 + N_EXPERTS)
    return pos[:, :2].T.reshape(-1), seg[0, experts], seg[1, experts]


def _expert_kernel(nblk_ref, blk0_ref, buf_ref, wg_ref, wu_ref, wd_ref, eo_ref,
                   xin_ref, out_ref, wgb_ref, wub_ref, wdb_ref, sem_in, sem_out):
    e = pl.program_id(0)
    n = nblk_ref[e]
    blk0 = blk0_ref[e]
    bm = MOE_BLOCK

    def rows(j):
        return pl.ds(pl.multiple_of((blk0 + j) * bm, bm), bm)

    def in_copy(j, slot):
        return pltpu.make_async_copy(buf_ref.at[rows(j)], xin_ref.at[slot], sem_in.at[slot])

    def out_copy(j, slot):
        return pltpu.make_async_copy(out_ref.at[slot], eo_ref.at[rows(j)], sem_out.at[slot])

    @pl.when(n > 0)
    def _():
        in_copy(0, 0).start()
        wgb_ref[...] = wg_ref[...].astype(BF16)
        wub_ref[...] = wu_ref[...].astype(BF16)
        wdb_ref[...] = wd_ref[...].astype(BF16)

    def block(j, carry):
        slot = j & 1

        @pl.when(j + 1 < n)
        def _():
            in_copy(j + 1, 1 - slot).start()

        in_copy(j, slot).wait()

        @pl.when(j >= 2)
        def _():
            out_copy(j - 2, slot).wait()

        xb = xin_ref[slot].astype(BF16)
        hid = _silu(_dot(xb, wgb_ref[...])) * _dot(xb, wub_ref[...])
        out_ref[slot] = _dot(hid.astype(BF16), wdb_ref[...])
        out_copy(j, slot).start()
        return carry

    lax.fori_loop(0, n, block, 0)

    @pl.when(n >= 2)
    def _():
        out_copy(n - 2, n & 1).wait()

    @pl.when(n >= 1)
    def _():
        out_copy(n - 1, (n - 1) & 1).wait()


def _moe_experts(nblk_e, blk0_e, buf, w_gate, w_up, w_down, layer):
    bm = MOE_BLOCK
    wspec = lambda shape: pl.BlockSpec((None, None) + shape, lambda e, nb, b0: (layer, e, 0, 0))
    return pl.pallas_call(
        _expert_kernel,
        grid_spec=pltpu.PrefetchScalarGridSpec(
            num_scalar_prefetch=2,
            grid=(N_EXPERTS,),
            in_specs=[
                pl.BlockSpec(memory_space=pl.ANY),
                wspec((D, D_EXPERT)),
                wspec((D, D_EXPERT)),
                wspec((D_EXPERT, D)),
            ],
            out_specs=pl.BlockSpec(memory_space=pl.ANY),
            scratch_shapes=[
                pltpu.VMEM((2, bm, D), F32),
                pltpu.VMEM((2, bm, D), F32),
                pltpu.VMEM((D, D_EXPERT), BF16),
                pltpu.VMEM((D, D_EXPERT), BF16),
                pltpu.VMEM((D_EXPERT, D), BF16),
                pltpu.SemaphoreType.DMA((2,)),
                pltpu.SemaphoreType.DMA((2,)),
            ],
        ),
        out_shape=jax.ShapeDtypeStruct((P_ROWS, D), F32),
        compiler_params=_cp(("arbitrary",)),
        name="moe_experts",
    )(nblk_e, blk0_e, buf, w_gate, w_up, w_down)


SC_CORES = 2
SC_SUBCORES = 16
SC_WORKERS = SC_CORES * SC_SUBCORES
SC_CHUNK = 32


def _sc_gather_store(table_hbm, idx_v, out_hbm, out_base, n_rows, bufs, sems):
    n_chunks = n_rows // SC_CHUNK
    assert n_chunks % 2 == 0

    def gather(j, b):
        off = pl.multiple_of(j * SC_CHUNK, SC_CHUNK)
        return pltpu.make_async_copy(table_hbm.at[idx_v.at[pl.ds(off, SC_CHUNK)]], bufs[b], sems[b])

    gather(0, 0).start()

    @pl.loop(0, n_chunks, step=2)
    def _(j):
        for b in range(2):
            jj = j + b

            @pl.when(jj + 1 < n_chunks)
            def _():
                gather(jj + 1, 1 - b).start()

            gather(jj, b).wait()
            row0 = pl.multiple_of(out_base + jj * SC_CHUNK, SC_CHUNK)
            pltpu.sync_copy(bufs[b], out_hbm.at[pl.ds(row0, SC_CHUNK)])


def _sc_row_buffers(width, dtype):
    return [pltpu.VMEM((SC_CHUNK, width), dtype), pltpu.VMEM((SC_CHUNK, width), dtype),
            pltpu.SemaphoreType.DMA, pltpu.SemaphoreType.DMA]


def _sc_gather_rows(table, idx):
    n_idx = idx.shape[0]
    width = table.shape[1]
    per_w = n_idx // SC_WORKERS
    mesh = plsc.VectorSubcoreMesh(core_axis_name="c", subcore_axis_name="s")

    def body(table_hbm, idx_hbm, out_hbm, idx_v, buf0, buf1, sem0, sem1):
        wid = lax.axis_index("s") * SC_CORES + lax.axis_index("c")
        base = wid * per_w
        pltpu.sync_copy(idx_hbm.at[pl.ds(base, per_w)], idx_v)
        _sc_gather_store(table_hbm, idx_v, out_hbm, base, per_w, (buf0, buf1), (sem0, sem1))

    return pl.kernel(
        body,
        out_type=jax.ShapeDtypeStruct((n_idx, width), table.dtype),
        mesh=mesh,
        scratch_types=[pltpu.VMEM((per_w,), I32)] + _sc_row_buffers(width, table.dtype),
        name="sc_gather_rows",
    )(table, idx)


def _sc_dispatch_rows(h, pos_km):
    per_w = P_ROWS // SC_WORKERS
    lanes = 16
    mesh = plsc.VectorSubcoreMesh(core_axis_name="c", subcore_axis_name="s")

    def body(h_hbm, pos_hbm, buf_hbm, pos_v, tok_v, buf0, buf1, sem0, sem1):
        wid = lax.axis_index("s") * SC_CORES + lax.axis_index("c")
        base = wid * per_w
        pltpu.sync_copy(pos_hbm, pos_v)

        @pl.loop(0, per_w // lanes)
        def _(i):
            off = pl.multiple_of(i * lanes, lanes)
            tok_v[pl.ds(off, lanes)] = (base + off + lax.iota(I32, lanes)) & (N_TOK - 1)

        @pl.loop(0, N_SLOTS // lanes)
        def _(i):
            off = pl.multiple_of(i * lanes, lanes)
            local = pos_v[pl.ds(off, lanes)] - base
            mine = (local >= 0) & (local < per_w)
            slot = off + lax.iota(I32, lanes)
            plsc.store_scatter(tok_v, [jnp.where(mine, local, 0)], slot & (N_TOK - 1), mask=mine)

        _sc_gather_store(h_hbm, tok_v, buf_hbm, base, per_w, (buf0, buf1), (sem0, sem1))

    return pl.kernel(
        body,
        out_type=jax.ShapeDtypeStruct((P_ROWS, h.shape[1]), h.dtype),
        mesh=mesh,
        scratch_types=[pltpu.VMEM((N_SLOTS,), I32), pltpu.VMEM((per_w,), I32)]
        + _sc_row_buffers(h.shape[1], h.dtype),
        compiler_params=pltpu.CompilerParams(needs_layout_passes=False),
        name="sc_dispatch_rows",
    )(h, pos_km)


def _combine_kernel(y0_ref, y1_ref, x_ref, sh_ref, route_ref, mod_ref, nw_ref, o_ref):
    route = route_ref[...]
    y = route[:, 2:3] * y0_ref[...] + route[:, 3:4] * y1_ref[...] + sh_ref[...]
    mod = mod_ref[...]
    o_ref[...] = x_ref[...] + mod[5:6] * (_rms(y) * nw_ref[3:4])


def _moe_combine(pos_km, eo, x1, sh, route, mod_l, nw8):
    tm = TM_MIX
    tiles_per_seq = SEQ // tm
    n_tiles = N_TOK // tm
    y2 = _sc_gather_rows(eo, pos_km)
    return pl.pallas_call(
        _combine_kernel,
        grid=(n_tiles,),
        in_specs=[
            pl.BlockSpec((tm, D), lambda i: (i, 0)),
            pl.BlockSpec((tm, D), lambda i: (i + n_tiles, 0)),
            pl.BlockSpec((tm, D), lambda i: (i, 0)),
            pl.BlockSpec((tm, D), lambda i: (i, 0)),
            pl.BlockSpec((tm, LANES), lambda i: (i, 0)),
            pl.BlockSpec((None, SUBLANES, D), lambda i: (i // tiles_per_seq, 0, 0)),
            pl.BlockSpec((SUBLANES, D), lambda i: (0, 0)),
        ],
        out_specs=pl.BlockSpec((tm, D), lambda i: (i, 0)),
        out_shape=jax.ShapeDtypeStruct((N_TOK, D), F32),
        compiler_params=_cp(("arbitrary",)),
        name="moe_combine",
    )(y2, y2, x1, sh, route, mod_l, nw8)


def _hier_moe_block(x1, mod_l, nw8, layer, w_rg, b_rg, w_re, b_re, w_gate, w_up, w_down,
                    ws_gate, ws_up, ws_down, w_sg):
    h, sh, route, cnt = _moe_pre(x1, mod_l, nw8, w_rg, b_rg, w_re, b_re, ws_gate, ws_up, ws_down, w_sg)
    pos_km, nblk_e, blk0_e = _moe_pos(cnt, route)
    buf = _sc_dispatch_rows(h, pos_km)
    eo = _moe_experts(nblk_e, blk0_e, buf, w_gate, w_up, w_down, layer)
    return _moe_combine(pos_km, eo, x1, sh, route, mod_l, nw8)


def _gdn_pre_kernel(x_ref, mod_ref, nw_ref, w_ref, wba_ref, cw_ref, misc_ref,
                    q_ref, k_ref, v_ref, z_ref, gb_ref, gt_ref, cbuf_ref, *, tiles_per_seq):
    tm = x_ref.shape[0]
    qkv_w = 3 * D
    x = x_ref[...]
    mod = mod_ref[...]
    nw = nw_ref[...]
    h = _rms(x) * nw[0:1] * (1.0 + mod[1:2]) + mod[0:1]
    hb = h.astype(BF16)
    proj = _dot(hb, w_ref[...])
    z_ref[...] = proj[:, qkv_w:].astype(BF16)
    pre = proj[:, :qkv_w]

    @pl.when(pl.program_id(0) % tiles_per_seq == 0)
    def _():
        cbuf_ref[0:SUBLANES, :] = jnp.zeros((SUBLANES, qkv_w), F32)

    cbuf_ref[SUBLANES:SUBLANES + tm, :] = pre
    cw = cw_ref[...]
    conv = (cw[0:1] * cbuf_ref[SUBLANES - 3:SUBLANES - 3 + tm, :]
            + cw[1:2] * cbuf_ref[SUBLANES - 2:SUBLANES - 2 + tm, :]
            + cw[2:3] * cbuf_ref[SUBLANES - 1:SUBLANES - 1 + tm, :]
            + cw[3:4] * pre)
    cbuf_ref[0:SUBLANES, :] = cbuf_ref[tm:tm + SUBLANES, :]
    act = _silu(conv)
    for hd in range(HEADS):
        lo = hd * HEAD_DIM
        qh = act[:, lo:lo + HEAD_DIM]
        kh = act[:, D + lo:D + lo + HEAD_DIM]
        qn = qh * lax.rsqrt(jnp.sum(qh * qh, axis=-1, keepdims=True) + EPS) * (HEAD_DIM ** -0.5)
        kn = kh * lax.rsqrt(jnp.sum(kh * kh, axis=-1, keepdims=True) + EPS)
        q_ref[:, lo:lo + HEAD_DIM] = qn.astype(BF16)
        k_ref[:, lo:lo + HEAD_DIM] = kn.astype(BF16)
    v_ref[...] = act[:, 2 * D:].astype(BF16)

    ba = _dot(hb, wba_ref[...])
    misc = misc_ref[...]
    beta = jax.nn.sigmoid(ba)
    sp_in = ba + misc[1:2]
    softplus = jnp.maximum(sp_in, 0.0) + jnp.log(1.0 + jnp.exp(-jnp.abs(sp_in)))
    g = -jnp.exp(misc[0:1]) * softplus
    row = lax.broadcasted_iota(I32, (tm, tm), 0)
    col = lax.broadcasted_iota(I32, (tm, tm), 1)
    tri = jnp.where((col <= row) & ((col >> CHUNK_SHIFT) == (row >> CHUNK_SHIFT)), 1.0, 0.0)
    gc = jnp.dot(tri, g, precision=HIGHEST, preferred_element_type=F32)
    lane = lax.broadcasted_iota(I32, (tm, LANES), 1)
    gb = jnp.where(lane < HEADS, beta, gc)
    gb_ref[...] = gb
    for c in range(tm // CHUNK):
        blk = jnp.concatenate([gb[c * CHUNK:(c + 1) * CHUNK], jnp.zeros((LANES - CHUNK, LANES), F32)], axis=0)
        gt_ref[c] = blk.T[HEADS:2 * HEADS, :]


def _gdn_pre(x2, mod_l, nw8, w_in, conv_w, a_log, dt_bias):
    tm = TM_GDN
    tiles_per_seq = SEQ // tm
    qkvz = 4 * D
    w_main = w_in[:, :qkvz].astype(BF16)
    wba = jnp.pad(w_in[:, qkvz:], ((0, 0), (0, LANES - 2 * HEADS))).astype(BF16)
    cw8 = jnp.pad(conv_w, ((0, SUBLANES - conv_w.shape[0]), (0, 0)))
    misc = jnp.zeros((SUBLANES, LANES), F32)
    misc = misc.at[0, HEADS:2 * HEADS].set(a_log).at[1, HEADS:2 * HEADS].set(dt_bias)
    const = lambda i: (0, 0)
    tok = lambda i: (i, 0)
    return pl.pallas_call(
        functools.partial(_gdn_pre_kernel, tiles_per_seq=tiles_per_seq),
        grid=(N_TOK // tm,),
        in_specs=[
            pl.BlockSpec((tm, D), tok),
            pl.BlockSpec((None, SUBLANES, D), lambda i: (i // tiles_per_seq, 0, 0)),
            pl.BlockSpec((SUBLANES, D), const),
            pl.BlockSpec((D, qkvz), const),
            pl.BlockSpec((D, LANES), const),
            pl.BlockSpec((SUBLANES, 3 * D), const),
            pl.BlockSpec((SUBLANES, LANES), const),
        ],
        out_specs=[
            pl.BlockSpec((tm, D), tok),
            pl.BlockSpec((tm, D), tok),
            pl.BlockSpec((tm, D), tok),
            pl.BlockSpec((tm, D), tok),
            pl.BlockSpec((tm, LANES), tok),
            pl.BlockSpec((tm // CHUNK, HEADS, LANES), lambda i: (i, 0, 0)),
        ],
        out_shape=[
            jax.ShapeDtypeStruct((N_TOK, D), BF16),
            jax.ShapeDtypeStruct((N_TOK, D), BF16),
            jax.ShapeDtypeStruct((N_TOK, D), BF16),
            jax.ShapeDtypeStruct((N_TOK, D), BF16),
            jax.ShapeDtypeStruct((N_TOK, LANES), F32),
            jax.ShapeDtypeStruct((N_TOK // CHUNK, HEADS, LANES), F32),
        ],
        scratch_shapes=[pltpu.VMEM((tm + SUBLANES, 3 * D), F32)],
        compiler_params=_cp(("arbitrary",)),
        name="gdn_pre",
    )(x2, mod_l, nw8, w_main, wba, cw8, misc)


def _dot_nt(a, b):
    return lax.dot_general(a, b, (((1,), (1,)), ((), ())), preferred_element_type=F32)


def _dot_tn(a, b):
    return lax.dot_general(a, b, (((0,), (0,)), ((), ())), preferred_element_type=F32)


def _gdn_chunk_kernel(q_ref, k_ref, v_ref, gb_ref, gt_ref, o_ref, s_ref):
    @pl.when(pl.program_id(1) == 0)
    def _():
        s_ref[...] = jnp.zeros(s_ref.shape, F32)

    c = CHUNK
    nb = q_ref.shape[0]
    row = lax.broadcasted_iota(I32, (c, c), 0)
    col = lax.broadcasted_iota(I32, (c, c), 1)
    causal = col <= row
    strict = col < row
    eye = jnp.where(col == row, 1.0, 0.0)
    chains = [(b, hd) for b in range(nb) for hd in range(HEADS)]
    st = []
    for b, hd in chains:
        lo = hd * HEAD_DIM
        gb = gb_ref[b]
        q = q_ref[b, :, lo:lo + HEAD_DIM].astype(F32)
        k = k_ref[b, :, lo:lo + HEAD_DIM].astype(F32)
        v = v_ref[b, :, lo:lo + HEAD_DIM].astype(F32)
        beta = gb[:, hd:hd + 1]
        gcol = gb[:, HEADS + hd:HEADS + hd + 1]
        grow = gt_ref[b, hd:hd + 1, 0:c]
        glast = gcol[c - 1:c, :]
        egc = jnp.exp(gcol)
        kb = k * beta
        st.append(dict(
            decay=jnp.exp(jnp.where(causal, gcol - grow, -jnp.inf)),
            kq=jnp.concatenate([kb, q], axis=0).astype(BF16),
            kbf=k.astype(BF16),
            rhs=jnp.concatenate([v * beta, kb * egc], axis=-1).astype(BF16),
            qd=(q * egc).astype(BF16),
            kd=(k * jnp.exp(glast - gcol)).astype(BF16),
            eg=jnp.exp(glast)))
    for x in st:
        kk = _dot_nt(x["kq"], x["kbf"])
        x["p"] = jnp.where(strict, kk[:c] * x["decay"], 0.0)
        x["attn"] = jnp.where(causal, kk[c:] * x["decay"], 0.0).astype(BF16)
        x["t"] = eye - x["p"]
    for _ in range(5):
        for x in st:
            pb = x["p"].astype(BF16)
            x["p"] = _dot(pb, pb)
        for x in st:
            x["t"] = x["t"] + _dot(x["t"].astype(BF16), x["p"].astype(BF16))
    for x in st:
        x["uw"] = _dot(x["t"].astype(BF16), x["rhs"])
    for x, (b, hd) in zip(st, chains):
        s = s_ref[b * HEADS + hd]
        x["s"] = s
        ws = _dot(jnp.concatenate([x["uw"][:, HEAD_DIM:].astype(BF16), x["qd"]], axis=0), s.astype(BF16))
        x["vb"] = (x["uw"][:, :HEAD_DIM] - ws[:c]).astype(BF16)
        x["o"] = ws[c:]
    for x, (b, hd) in zip(st, chains):
        lo = hd * HEAD_DIM
        s_ref[b * HEADS + hd] = x["s"] * x["eg"] + _dot_tn(x["kd"], x["vb"])
        o_ref[b, :, lo:lo + HEAD_DIM] = (x["o"] + _dot(x["attn"], x["vb"])).astype(BF16)


def _gdn_chunks(q, k, v, gb, gt):
    nb = GDN_BATCH_PER_STEP
    n_chunks = SEQ // CHUNK
    tok = lambda b, c: (b, c, 0)
    o = pl.pallas_call(
        _gdn_chunk_kernel,
        grid=(BATCH // nb, n_chunks),
        in_specs=[
            pl.BlockSpec((nb, CHUNK, D), tok),
            pl.BlockSpec((nb, CHUNK, D), tok),
            pl.BlockSpec((nb, CHUNK, D), tok),
            pl.BlockSpec((nb, CHUNK, LANES), tok),
            pl.BlockSpec((nb, None, HEADS, LANES), lambda b, c: (b, c, 0, 0)),
        ],
        out_specs=pl.BlockSpec((nb, CHUNK, D), tok),
        out_shape=jax.ShapeDtypeStruct((BATCH, SEQ, D), BF16),
        scratch_shapes=[pltpu.VMEM((nb * HEADS, HEAD_DIM, HEAD_DIM), F32)],
        compiler_params=_cp(("arbitrary", "arbitrary")),
        name="gdn_chunks",
    )(q.reshape(BATCH, SEQ, D), k.reshape(BATCH, SEQ, D), v.reshape(BATCH, SEQ, D),
      gb.reshape(BATCH, SEQ, LANES), gt.reshape(BATCH, n_chunks, HEADS, LANES))
    return o.reshape(N_TOK, D)


def _gdn_post_kernel(o_ref, z_ref, x_ref, mod_ref, nw_ref, gnw_ref, wout_ref, out_ref):
    gnw = gnw_ref[0:1]
    parts = []
    for hd in range(HEADS):
        lo = hd * HEAD_DIM
        oh = o_ref[:, lo:lo + HEAD_DIM].astype(F32)
        zh = z_ref[:, lo:lo + HEAD_DIM].astype(F32)
        parts.append((_rms(oh) * gnw * _silu(zh)).astype(BF16))
    y = _dot(jnp.concatenate(parts, axis=-1), wout_ref[...])
    mod = mod_ref[...]
    out_ref[...] = x_ref[...] + mod[2:3] * (_rms(y) * nw_ref[1:2])


def _gdn_post(o, z, x2, mod_l, nw8, gdn_norm_w, w_out):
    tm = TM_MIX
    tiles_per_seq = SEQ // tm
    gnw8 = jnp.pad(gdn_norm_w.reshape(1, HEAD_DIM), ((0, SUBLANES - 1), (0, 0)))
    tok = lambda i: (i, 0)
    const = lambda i: (0, 0)
    return pl.pallas_call(
        _gdn_post_kernel,
        grid=(N_TOK // tm,),
        in_specs=[
            pl.BlockSpec((tm, D), tok),
            pl.BlockSpec((tm, D), tok),
            pl.BlockSpec((tm, D), tok),
            pl.BlockSpec((None, SUBLANES, D), lambda i: (i // tiles_per_seq, 0, 0)),
            pl.BlockSpec((SUBLANES, D), const),
            pl.BlockSpec((SUBLANES, HEAD_DIM), const),
            pl.BlockSpec((D, D), const),
        ],
        out_specs=pl.BlockSpec((tm, D), tok),
        out_shape=jax.ShapeDtypeStruct((N_TOK, D), F32),
        compiler_params=_cp(("arbitrary",)),
        name="gdn_post",
    )(o, z, x2, mod_l, nw8, gnw8, w_out.astype(BF16))


def kernel(x, c, ada_w, ada_b, norm_w, conv_in_w, conv_w, conv_out_w, gdn_in_w, gdn_conv_w, gdn_a_log,
           gdn_dt_bias, gdn_norm_w, gdn_out_w, moe_group_w, moe_group_b, moe_expert_w, moe_expert_b,
           moe_w_gate, moe_w_up, moe_w_down, shared_w_gate, shared_w_up, shared_w_down, shared_gate_w):
    mod = _ada_mod(c, ada_w, ada_b)
    nw8 = jnp.pad(norm_w, ((0, 0), (0, SUBLANES - norm_w.shape[1]), (0, 0)))
    x2 = x.reshape(N_TOK, D)

    def moe(x1, layer):
        return _hier_moe_block(
            x1, mod[layer], nw8[layer], layer,
            moe_group_w[layer], moe_group_b[layer], moe_expert_w[layer], moe_expert_b[layer],
            moe_w_gate, moe_w_up, moe_w_down,
            shared_w_gate[layer], shared_w_up[layer], shared_w_down[layer], shared_gate_w[layer])

    x2 = _conv_mixer(x2, mod[0], nw8[0], conv_in_w[0], conv_w[0], conv_out_w[0])
    x2 = moe(x2, 0)
    q, k, v, z, gb, gt = _gdn_pre(x2, mod[1], nw8[1], gdn_in_w[0], gdn_conv_w[0], gdn_a_log[0], gdn_dt_bias[0])
    o = _gdn_chunks(q, k, v, gb, gt)
    x2 = _gdn_post(o, z, x2, mod[1], nw8[1], gdn_norm_w[0], gdn_out_w[0])
    x2 = moe(x2, 1)
    return x2.reshape(BATCH, SEQ, D)
```

```python
import functools

import jax
import jax.numpy as jnp
from jax import lax
from jax.experimental import pallas as pl
from jax.experimental.pallas import tpu as pltpu
from jax.experimental.pallas import tpu_sc as plsc

F32 = jnp.float32
BF16 = jnp.bfloat16
I32 = jnp.int32
HIGHEST = lax.Precision.HIGHEST

D = 1024
BATCH = 4
SEQ = 4096
N_TOK = BATCH * SEQ
HEADS = 8
HEAD_DIM = 128
CHUNK = 64
CHUNK_SHIFT = 6
N_GROUPS = 8
N_EXPERTS = 64
D_EXPERT = 256
D_SHARED = 512
EPS = 1e-6
LANES = 128
SUBLANES = 8
EXPERT_LANE0 = N_GROUPS

MOE_BLOCK = 128
N_SLOTS = N_TOK * 2
N_BLOCKS = N_SLOTS // MOE_BLOCK + N_EXPERTS
P_ROWS = N_BLOCKS * MOE_BLOCK

TM_MIX = 512
TM_GDN = 256
TM_DMA = 256
TM_POS = 2048
GDN_BATCH_PER_STEP = 4
VMEM_LIMIT = 56 * 1024 * 1024


def _cp(sem):
    return pltpu.CompilerParams(dimension_semantics=sem, vmem_limit_bytes=VMEM_LIMIT)


def _rms(x):
    return x * lax.rsqrt(jnp.mean(x * x, axis=-1, keepdims=True) + EPS)


def _silu(x):
    return x * jax.nn.sigmoid(x)


def _dot(a, b):
    return jnp.dot(a, b, preferred_element_type=F32)


def _causal_taps(x, tail_ref, taps, first_of_seq):
    @pl.when(first_of_seq)
    def _():
        tail_ref[...] = jnp.zeros(tail_ref.shape, F32)

    tail = tail_ref[...]
    sub = lax.broadcasted_iota(I32, tail.shape, 0)
    out = []
    for s in taps:
        rolled = pltpu.roll(x, s, axis=0)
        head = jnp.where(sub < s, pltpu.roll(tail, s, axis=0), rolled[0:SUBLANES])
        out.append(jnp.concatenate([head, rolled[SUBLANES:]], axis=0))
    tail_ref[...] = x[x.shape[0] - SUBLANES:]
    return out


def _ada_kernel(c_ref, w_ref, b_ref, o_ref):
    cs = _silu(c_ref[...])
    o_ref[0] = _dot(cs.astype(BF16), w_ref[0].astype(BF16)) + b_ref[0]


def _ada_mod(c, ada_w, ada_b):
    depth = ada_w.shape[0]
    tn = 1024
    c8 = jnp.pad(c, ((0, SUBLANES - BATCH), (0, 0)))
    mod = pl.pallas_call(
        _ada_kernel,
        grid=(depth, 6 * D // tn),
        in_specs=[
            pl.BlockSpec((SUBLANES, D), lambda l, j: (0, 0)),
            pl.BlockSpec((1, D, tn), lambda l, j: (l, 0, j)),
            pl.BlockSpec((1, 1, tn), lambda l, j: (l, 0, j)),
        ],
        out_specs=pl.BlockSpec((1, SUBLANES, tn), lambda l, j: (l, 0, j)),
        out_shape=jax.ShapeDtypeStruct((depth, SUBLANES, 6 * D), F32),
        compiler_params=_cp(("arbitrary", "arbitrary")),
        name="ada_mod",
    )(c8, ada_w, ada_b.reshape(depth, 1, 6 * D))
    mod = mod[:, :BATCH].reshape(depth, BATCH, 6, D)
    return jnp.pad(mod, ((0, 0), (0, 0), (0, 2), (0, 0)))


def _conv_mixer_kernel(x_ref, mod_ref, nw_ref, win_ref, cw_ref, wout_ref, o_ref, tail_ref, *, tiles_per_seq):
    x = x_ref[...]
    mod = mod_ref[...]
    nw = nw_ref[...]
    h = _rms(x) * nw[0:1] * (1.0 + mod[1:2]) + mod[0:1]
    bcx = _dot(h.astype(BF16), win_ref[...])
    u = bcx[:, D:2 * D] * bcx[:, 2 * D:]
    u2, u1 = _causal_taps(u, tail_ref, (2, 1), pl.program_id(0) % tiles_per_seq == 0)
    cw = cw_ref[...]
    conv = cw[0:1] * u2 + cw[1:2] * u1 + cw[2:3] * u
    y = _dot((bcx[:, :D] * conv).astype(BF16), wout_ref[...])
    o_ref[...] = x + mod[2:3] * (_rms(y) * nw[1:2])


def _conv_mixer(x2, mod_l, nw8, w_in, conv_w, w_out):
    tm = TM_MIX
    tiles_per_seq = SEQ // tm
    cw8 = jnp.pad(conv_w, ((0, SUBLANES - conv_w.shape[0]), (0, 0)))
    return pl.pallas_call(
        functools.partial(_conv_mixer_kernel, tiles_per_seq=tiles_per_seq),
        grid=(N_TOK // tm,),
        in_specs=[
            pl.BlockSpec((tm, D), lambda i: (i, 0)),
            pl.BlockSpec((None, SUBLANES, D), lambda i: (i // tiles_per_seq, 0, 0)),
            pl.BlockSpec((SUBLANES, D), lambda i: (0, 0)),
            pl.BlockSpec((D, 3 * D), lambda i: (0, 0)),
            pl.BlockSpec((SUBLANES, D), lambda i: (0, 0)),
            pl.BlockSpec((D, D), lambda i: (0, 0)),
        ],
        out_specs=pl.BlockSpec((tm, D), lambda i: (i, 0)),
        out_shape=jax.ShapeDtypeStruct((N_TOK, D), F32),
        scratch_shapes=[pltpu.VMEM((SUBLANES, D), F32)],
        compiler_params=_cp(("arbitrary",)),
        name="conv_mixer",
    )(x2, mod_l, nw8, w_in.astype(BF16), cw8, w_out.astype(BF16))


def _lane_pick(lane, mask_val_pairs):
    out = jnp.zeros(lane.shape, F32)
    for idx, val in mask_val_pairs:
        out = jnp.where(lane == float(idx), val, out)
    return out


def _moe_pre_kernel(x_ref, mod_ref, nw_ref, wr_ref, br_ref, wgu_ref, wd_ref, wsg_ref,
                    h_ref, sh_ref, route_ref, cnt_ref, carry_ref):
    tm = x_ref.shape[0]
    x = x_ref[...]
    mod = mod_ref[...]
    nw = nw_ref[...]
    h = _rms(x) * nw[2:3] * (1.0 + mod[4:5]) + mod[3:4]
    h_ref[...] = h
    hb = h.astype(BF16)

    h_lo = (h - hb.astype(F32)).astype(BF16)
    hw = _dot(hb, wr_ref[...]) + _dot(h_lo, wr_ref[...])
    logits = hw[:, :LANES] + hw[:, LANES:] + br_ref[0:1]
    lane_i = lax.broadcasted_iota(I32, (tm, LANES), 1)
    lane = lane_i.astype(F32)
    neg = jnp.float32(-jnp.inf)
    is_group = lane_i < N_GROUPS
    gl = jnp.where(is_group, logits, neg)
    gmax = jnp.max(gl, axis=-1, keepdims=True)
    gsel = jnp.min(jnp.where(gl == gmax, lane, float(LANES)), axis=-1, keepdims=True)
    psel = 1.0 / jnp.sum(jnp.where(is_group, jnp.exp(logits - gmax), 0.0), axis=-1, keepdims=True)
    lane_group = ((lane_i - EXPERT_LANE0) >> 3).astype(F32)
    in_group = (lane_i >= EXPERT_LANE0) & (lane_i < EXPERT_LANE0 + N_EXPERTS) & (lane_group == gsel)
    el = jnp.where(in_group, logits, neg)
    v1 = jnp.max(el, axis=-1, keepdims=True)
    i1 = jnp.min(jnp.where(el == v1, lane, float(LANES)), axis=-1, keepdims=True)
    el2 = jnp.where(lane == i1, neg, el)
    v2 = jnp.max(el2, axis=-1, keepdims=True)
    i2 = jnp.min(jnp.where(el2 == v2, lane, float(LANES)), axis=-1, keepdims=True)
    e2 = jnp.exp(v2 - v1)
    g1 = psel / (1.0 + e2)
    g2 = psel * e2 / (1.0 + e2)

    @pl.when(pl.program_id(0) == 0)
    def _():
        carry_ref[...] = jnp.zeros(carry_ref.shape, F32)

    oh1 = lane == i1
    oh2 = lane == i2
    cnt = jnp.where(oh1 | oh2, 1.0, 0.0).astype(BF16)
    row = lax.broadcasted_iota(I32, (tm, tm), 0)
    col = lax.broadcasted_iota(I32, (tm, tm), 1)
    tri = jnp.where(col < row, 1.0, 0.0).astype(BF16)
    pre = _dot(tri, cnt) + carry_ref[0:1]
    r1 = jnp.sum(jnp.where(oh1, pre, 0.0), axis=-1, keepdims=True)
    r2 = jnp.sum(jnp.where(oh2, pre, 0.0), axis=-1, keepdims=True)
    carry_ref[...] = carry_ref[...] + jnp.sum(cnt.astype(F32), axis=0, keepdims=True)
    cnt_ref[...] = carry_ref[...]
    route_ref[...] = _lane_pick(lane, [
        (0, i1 - EXPERT_LANE0), (1, i2 - EXPERT_LANE0),
        (2, g1), (3, g2), (4, r1), (5, r2)])

    gu = _dot(hb, wgu_ref[...])
    hid = _silu(gu[:, :D_SHARED]) * gu[:, D_SHARED:]
    ys = _dot(hid.astype(BF16), wd_ref[...])
    sg = jax.nn.sigmoid(jnp.sum(h * wsg_ref[0:1], axis=-1, keepdims=True))
    sh_ref[...] = sg * ys


def _moe_pre(x1, mod_l, nw8, w_rg, b_rg, w_re, b_re, ws_gate, ws_up, ws_down, w_sg):
    tm = TM_MIX
    tiles_per_seq = SEQ // tm
    pad_l = LANES - N_GROUPS - N_EXPERTS
    wr = jnp.pad(jnp.concatenate([w_rg, w_re], axis=1), ((0, 0), (0, pad_l)))
    wr_hi = wr.astype(BF16)
    wr = jnp.concatenate([wr_hi, (wr - wr_hi.astype(F32)).astype(BF16)], axis=1)
    br = jnp.pad(jnp.concatenate([b_rg, b_re])[None, :], ((0, SUBLANES - 1), (0, pad_l)))
    wgu = jnp.concatenate([ws_gate, ws_up], axis=1).astype(BF16)
    wsg8 = jnp.pad(w_sg.reshape(1, D), ((0, SUBLANES - 1), (0, 0)))
    const = lambda i: (0, 0)
    return pl.pallas_call(
        _moe_pre_kernel,
        grid=(N_TOK // tm,),
        in_specs=[
            pl.BlockSpec((tm, D), lambda i: (i, 0)),
            pl.BlockSpec((None, SUBLANES, D), lambda i: (i // tiles_per_seq, 0, 0)),
            pl.BlockSpec((SUBLANES, D), const),
            pl.BlockSpec((D, 2 * LANES), const),
            pl.BlockSpec((SUBLANES, LANES), const),
            pl.BlockSpec((D, 2 * D_SHARED), const),
            pl.BlockSpec((D_SHARED, D), const),
            pl.BlockSpec((SUBLANES, D), const),
        ],
        out_specs=[
            pl.BlockSpec((tm, D), lambda i: (i, 0)),
            pl.BlockSpec((tm, D), lambda i: (i, 0)),
            pl.BlockSpec((tm, LANES), lambda i: (i, 0)),
            pl.BlockSpec((SUBLANES, LANES), const),
        ],
        out_shape=[
            jax.ShapeDtypeStruct((N_TOK, D), F32),
            jax.ShapeDtypeStruct((N_TOK, D), F32),
            jax.ShapeDtypeStruct((N_TOK, LANES), F32),
            jax.ShapeDtypeStruct((SUBLANES, LANES), F32),
        ],
        scratch_shapes=[pltpu.VMEM((SUBLANES, LANES), F32)],
        compiler_params=_cp(("arbitrary",)),
        name="moe_pre",
    )(x1, mod_l, nw8, wr, br, wgu, ws_down.astype(BF16), wsg8)


def _n_blocks_per_expert(cnt_row):
    return jnp.floor((cnt_row + (MOE_BLOCK - 1)) * (1.0 / MOE_BLOCK))


def _moe_pos_kernel(cnt_ref, route_ref, pos_ref, seg_ref):
    tm = route_ref.shape[0]
    nb = _n_blocks_per_expert(cnt_ref[...])
    r = lax.broadcasted_iota(I32, (LANES, LANES), 0)
    c = lax.broadcasted_iota(I32, (LANES, LANES), 1)
    excl = jnp.dot(nb, jnp.where(r < c, 1.0, 0.0), precision=HIGHEST, preferred_element_type=F32)
    pstart = excl[0:1] * MOE_BLOCK
    route = route_ref[...]
    lane = lax.broadcasted_iota(I32, (tm, LANES), 1).astype(F32)
    e1 = route[:, 0:1] + EXPERT_LANE0
    e2 = route[:, 1:2] + EXPERT_LANE0
    p1 = jnp.sum(jnp.where(lane == e1, pstart, 0.0), axis=-1, keepdims=True) + route[:, 4:5]
    p2 = jnp.sum(jnp.where(lane == e2, pstart, 0.0), axis=-1, keepdims=True) + route[:, 5:6]
    pos_ref[...] = _lane_pick(lane, [(0, p1), (1, p2)]).astype(I32)

    @pl.when(pl.program_id(0) == 0)
    def _():
        sub = lax.broadcasted_iota(I32, (SUBLANES, LANES), 0)
        seg_ref[...] = jnp.where(sub == 0, nb, excl).astype(I32)


def _moe_pos(cnt, route):
    tm = TM_POS
    pos, seg = pl.pallas_call(
        _moe_pos_kernel,
        grid=(N_TOK // tm,),
        in_specs=[
            pl.BlockSpec((SUBLANES, LANES), lambda i: (0, 0)),
            pl.BlockSpec((tm, LANES), lambda i: (i, 0)),
        ],
        out_specs=[
            pl.BlockSpec((tm, LANES), lambda i: (i, 0)),
            pl.BlockSpec((SUBLANES, LANES), lambda i: (0, 0)),
        ],
        out_shape=[
            jax.ShapeDtypeStruct((N_TOK, LANES), I32),
            jax.ShapeDtypeStruct((SUBLANES, LANES), I32),
        ],
        compiler_params=_cp(("arbitrary",)),
        name="moe_pos",
    )(cnt, route)
    experts = slice(EXPERT_LANE0, EXPERT_LANE0 + N_EXPERTS)
    return pos[:, :2].T.reshape(-1), seg[0, experts], seg[1, experts]


def _expert_kernel(nblk_ref, blk0_ref, buf_ref, wg_ref, wu_ref, wd_ref, eo_ref,
                   xin_ref, out_ref, wgb_ref, wub_ref, wdb_ref, sem_in, sem_out):
    e = pl.program_id(0)
    last = pl.num_programs(0) - 1
    n = nblk_ref[e]
    blk0 = blk0_ref[e]
    total = blk0_ref[last] + nblk_ref[last]
    bm = MOE_BLOCK

    def rows(g):
        return pl.ds(pl.multiple_of(g * bm, bm), bm)

    def in_copy(g):
        return pltpu.make_async_copy(buf_ref.at[rows(g)], xin_ref.at[g & 1], sem_in.at[g & 1])

    def out_copy(g):
        return pltpu.make_async_copy(out_ref.at[g & 1], eo_ref.at[rows(g)], sem_out.at[g & 1])

    @pl.when(n > 0)
    def _():
        wgb_ref[...] = wg_ref[...].astype(BF16)
        wub_ref[...] = wu_ref[...].astype(BF16)
        wdb_ref[...] = wd_ref[...].astype(BF16)

    def block(j, carry):
        g = blk0 + j
        slot = g & 1

        @pl.when(g == 0)
        def _():
            in_copy(g).start()

        @pl.when(g + 1 < total)
        def _():
            in_copy(g + 1).start()

        in_copy(g).wait()

        @pl.when(g >= 2)
        def _():
            out_copy(g - 2).wait()

        xb = xin_ref[slot].astype(BF16)
        hid = _silu(_dot(xb, wgb_ref[...])) * _dot(xb, wub_ref[...])
        out_ref[slot] = _dot(hid.astype(BF16), wdb_ref[...])
        out_copy(g).start()
        return carry

    lax.fori_loop(0, n, block, 0)

    @pl.when((e == last) & (total >= 2))
    def _():
        out_copy(total - 2).wait()

    @pl.when((e == last) & (total >= 1))
    def _():
        out_copy(total - 1).wait()


def _moe_experts(nblk_e, blk0_e, buf, w_gate, w_up, w_down, layer):
    bm = MOE_BLOCK
    wspec = lambda shape: pl.BlockSpec((None, None) + shape, lambda e, nb, b0: (layer, e, 0, 0))
    return pl.pallas_call(
        _expert_kernel,
        grid_spec=pltpu.PrefetchScalarGridSpec(
            num_scalar_prefetch=2,
            grid=(N_EXPERTS,),
            in_specs=[
                pl.BlockSpec(memory_space=pl.ANY),
                wspec((D, D_EXPERT)),
                wspec((D, D_EXPERT)),
                wspec((D_EXPERT, D)),
            ],
            out_specs=pl.BlockSpec(memory_space=pl.ANY),
            scratch_shapes=[
                pltpu.VMEM((2, bm, D), F32),
                pltpu.VMEM((2, bm, D), F32),
                pltpu.VMEM((D, D_EXPERT), BF16),
                pltpu.VMEM((D, D_EXPERT), BF16),
                pltpu.VMEM((D_EXPERT, D), BF16),
                pltpu.SemaphoreType.DMA((2,)),
                pltpu.SemaphoreType.DMA((2,)),
            ],
        ),
        out_shape=jax.ShapeDtypeStruct((P_ROWS, D), F32),
        compiler_params=_cp(("arbitrary",)),
        name="moe_experts",
    )(nblk_e, blk0_e, buf, w_gate, w_up, w_down)


SC_CORES = 2
SC_SUBCORES = 16
SC_WORKERS = SC_CORES * SC_SUBCORES
SC_CHUNK = 32


def _sc_gather_store(table_hbm, idx_v, out_hbm, out_base, n_rows, bufs, sems):
    n_chunks = n_rows // SC_CHUNK
    assert n_chunks % 2 == 0

    def gather(j, b):
        off = pl.multiple_of(j * SC_CHUNK, SC_CHUNK)
        return pltpu.make_async_copy(table_hbm.at[idx_v.at[pl.ds(off, SC_CHUNK)]], bufs[b], sems[b])

    gather(0, 0).start()

    @pl.loop(0, n_chunks, step=2)
    def _(j):
        for b in range(2):
            jj = j + b

            @pl.when(jj + 1 < n_chunks)
            def _():
                gather(jj + 1, 1 - b).start()

            gather(jj, b).wait()
            row0 = pl.multiple_of(out_base + jj * SC_CHUNK, SC_CHUNK)
            pltpu.sync_copy(bufs[b], out_hbm.at[pl.ds(row0, SC_CHUNK)])


def _sc_row_buffers(width, dtype):
    return [pltpu.VMEM((SC_CHUNK, width), dtype), pltpu.VMEM((SC_CHUNK, width), dtype),
            pltpu.SemaphoreType.DMA, pltpu.SemaphoreType.DMA]


def _sc_gather_rows(table, idx):
    n_idx = idx.shape[0]
    width = table.shape[1]
    per_w = n_idx // SC_WORKERS
    mesh = plsc.VectorSubcoreMesh(core_axis_name="c", subcore_axis_name="s")

    def body(table_hbm, idx_hbm, out_hbm, idx_v, buf0, buf1, sem0, sem1):
        wid = lax.axis_index("s") * SC_CORES + lax.axis_index("c")
        base = wid * per_w
        pltpu.sync_copy(idx_hbm.at[pl.ds(base, per_w)], idx_v)
        _sc_gather_store(table_hbm, idx_v, out_hbm, base, per_w, (buf0, buf1), (sem0, sem1))

    return pl.kernel(
        body,
        out_type=jax.ShapeDtypeStruct((n_idx, width), table.dtype),
        mesh=mesh,
        scratch_types=[pltpu.VMEM((per_w,), I32)] + _sc_row_buffers(width, table.dtype),
        name="sc_gather_rows",
    )(table, idx)


def _sc_dispatch_rows(h, pos_km):
    per_w = P_ROWS // SC_WORKERS
    lanes = 16
    mesh = plsc.VectorSubcoreMesh(core_axis_name="c", subcore_axis_name="s")

    def body(h_hbm, pos_hbm, buf_hbm, pos_v, tok_v, buf0, buf1, sem0, sem1):
        wid = lax.axis_index("s") * SC_CORES + lax.axis_index("c")
        base = wid * per_w
        pltpu.sync_copy(pos_hbm, pos_v)

        @pl.loop(0, per_w // lanes)
        def _(i):
            off = pl.multiple_of(i * lanes, lanes)
            tok_v[pl.ds(off, lanes)] = (base + off + lax.iota(I32, lanes)) & (N_TOK - 1)

        @pl.loop(0, N_SLOTS // lanes)
        def _(i):
            off = pl.multiple_of(i * lanes, lanes)
            local = pos_v[pl.ds(off, lanes)] - base
            mine = (local >= 0) & (local < per_w)
            slot = off + lax.iota(I32, lanes)
            plsc.store_scatter(tok_v, [jnp.where(mine, local, 0)], slot & (N_TOK - 1), mask=mine)

        _sc_gather_store(h_hbm, tok_v, buf_hbm, base, per_w, (buf0, buf1), (sem0, sem1))

    return pl.kernel(
        body,
        out_type=jax.ShapeDtypeStruct((P_ROWS, h.shape[1]), h.dtype),
        mesh=mesh,
        scratch_types=[pltpu.VMEM((N_SLOTS,), I32), pltpu.VMEM((per_w,), I32)]
        + _sc_row_buffers(h.shape[1], h.dtype),
        compiler_params=pltpu.CompilerParams(needs_layout_passes=False),
        name="sc_dispatch_rows",
    )(h, pos_km)


def _combine_kernel(y0_ref, y1_ref, x_ref, sh_ref, route_ref, mod_ref, nw_ref, o_ref):
    route = route_ref[...]
    y = route[:, 2:3] * y0_ref[...] + route[:, 3:4] * y1_ref[...] + sh_ref[...]
    mod = mod_ref[...]
    o_ref[...] = x_ref[...] + mod[5:6] * (_rms(y) * nw_ref[3:4])


def _moe_combine(pos_km, eo, x1, sh, route, mod_l, nw8):
    tm = TM_MIX
    tiles_per_seq = SEQ // tm
    n_tiles = N_TOK // tm
    y2 = _sc_gather_rows(eo, pos_km)
    return pl.pallas_call(
        _combine_kernel,
        grid=(n_tiles,),
        in_specs=[
            pl.BlockSpec((tm, D), lambda i: (i, 0)),
            pl.BlockSpec((tm, D), lambda i: (i + n_tiles, 0)),
            pl.BlockSpec((tm, D), lambda i: (i, 0)),
            pl.BlockSpec((tm, D), lambda i: (i, 0)),
            pl.BlockSpec((tm, LANES), lambda i: (i, 0)),
            pl.BlockSpec((None, SUBLANES, D), lambda i: (i // tiles_per_seq, 0, 0)),
            pl.BlockSpec((SUBLANES, D), lambda i: (0, 0)),
        ],
        out_specs=pl.BlockSpec((tm, D), lambda i: (i, 0)),
        out_shape=jax.ShapeDtypeStruct((N_TOK, D), F32),
        compiler_params=_cp(("arbitrary",)),
        name="moe_combine",
    )(y2, y2, x1, sh, route, mod_l, nw8)


def _hier_moe_block(x1, mod_l, nw8, layer, w_rg, b_rg, w_re, b_re, w_gate, w_up, w_down,
                    ws_gate, ws_up, ws_down, w_sg):
    h, sh, route, cnt = _moe_pre(x1, mod_l, nw8, w_rg, b_rg, w_re, b_re, ws_gate, ws_up, ws_down, w_sg)
    pos_km, nblk_e, blk0_e = _moe_pos(cnt, route)
    buf = _sc_dispatch_rows(h, pos_km)
    eo = _moe_experts(nblk_e, blk0_e, buf, w_gate, w_up, w_down, layer)
    return _moe_combine(pos_km, eo, x1, sh, route, mod_l, nw8)


def _gdn_pre_kernel(x_ref, mod_ref, nw_ref, w_ref, wba_ref, cw_ref, misc_ref,
                    q_ref, k_ref, v_ref, z_ref, gb_ref, gt_ref, tail_ref, *, tiles_per_seq):
    tm = x_ref.shape[0]
    qkv_w = 3 * D
    x = x_ref[...]
    mod = mod_ref[...]
    nw = nw_ref[...]
    h = _rms(x) * nw[0:1] * (1.0 + mod[1:2]) + mod[0:1]
    hb = h.astype(BF16)
    proj = _dot(hb, w_ref[...])
    z_ref[...] = proj[:, qkv_w:].astype(BF16)
    pre = proj[:, :qkv_w]
    p3, p2, p1 = _causal_taps(pre, tail_ref, (3, 2, 1), pl.program_id(0) % tiles_per_seq == 0)
    cw = cw_ref[...]
    conv = cw[0:1] * p3 + cw[1:2] * p2 + cw[2:3] * p1 + cw[3:4] * pre
    act = _silu(conv)
    for hd in range(HEADS):
        lo = hd * HEAD_DIM
        qh = act[:, lo:lo + HEAD_DIM]
        kh = act[:, D + lo:D + lo + HEAD_DIM]
        qn = qh * lax.rsqrt(jnp.sum(qh * qh, axis=-1, keepdims=True) + EPS) * (HEAD_DIM ** -0.5)
        kn = kh * lax.rsqrt(jnp.sum(kh * kh, axis=-1, keepdims=True) + EPS)
        q_ref[:, lo:lo + HEAD_DIM] = qn.astype(BF16)
        k_ref[:, lo:lo + HEAD_DIM] = kn.astype(BF16)
    v_ref[...] = act[:, 2 * D:].astype(BF16)

    ba = _dot(hb, wba_ref[...])
    misc = misc_ref[...]
    beta = jax.nn.sigmoid(ba)
    sp_in = ba + misc[1:2]
    softplus = jnp.maximum(sp_in, 0.0) + jnp.log(1.0 + jnp.exp(-jnp.abs(sp_in)))
    g = -jnp.exp(misc[0:1]) * softplus
    row = lax.broadcasted_iota(I32, (tm, tm), 0)
    col = lax.broadcasted_iota(I32, (tm, tm), 1)
    tri = jnp.where((col <= row) & ((col >> CHUNK_SHIFT) == (row >> CHUNK_SHIFT)), 1.0, 0.0)
    gc = jnp.dot(tri, g, precision=HIGHEST, preferred_element_type=F32)
    lane = lax.broadcasted_iota(I32, (tm, LANES), 1)
    gb = jnp.where(lane < HEADS, beta, gc)
    gb_ref[...] = gb
    for c in range(tm // CHUNK):
        blk = jnp.concatenate([gb[c * CHUNK:(c + 1) * CHUNK], jnp.zeros((LANES - CHUNK, LANES), F32)], axis=0)
        gt_ref[c] = blk.T[HEADS:2 * HEADS, :]


def _gdn_pre(x2, mod_l, nw8, w_in, conv_w, a_log, dt_bias):
    tm = TM_GDN
    tiles_per_seq = SEQ // tm
    qkvz = 4 * D
    w_main = w_in[:, :qkvz].astype(BF16)
    wba = jnp.pad(w_in[:, qkvz:], ((0, 0), (0, LANES - 2 * HEADS))).astype(BF16)
    cw8 = jnp.pad(conv_w, ((0, SUBLANES - conv_w.shape[0]), (0, 0)))
    misc = jnp.zeros((SUBLANES, LANES), F32)
    misc = misc.at[0, HEADS:2 * HEADS].set(a_log).at[1, HEADS:2 * HEADS].set(dt_bias)
    const = lambda i: (0, 0)
    tok = lambda i: (i, 0)
    return pl.pallas_call(
        functools.partial(_gdn_pre_kernel, tiles_per_seq=tiles_per_seq),
        grid=(N_TOK // tm,),
        in_specs=[
            pl.BlockSpec((tm, D), tok),
            pl.BlockSpec((None, SUBLANES, D), lambda i: (i // tiles_per_seq, 0, 0)),
            pl.BlockSpec((SUBLANES, D), const),
            pl.BlockSpec((D, qkvz), const),
            pl.BlockSpec((D, LANES), const),
            pl.BlockSpec((SUBLANES, 3 * D), const),
            pl.BlockSpec((SUBLANES, LANES), const),
        ],
        out_specs=[
            pl.BlockSpec((tm, D), tok),
            pl.BlockSpec((tm, D), tok),
            pl.BlockSpec((tm, D), tok),
            pl.BlockSpec((tm, D), tok),
            pl.BlockSpec((tm, LANES), tok),
            pl.BlockSpec((tm // CHUNK, HEADS, LANES), lambda i: (i, 0, 0)),
        ],
        out_shape=[
            jax.ShapeDtypeStruct((N_TOK, D), BF16),
            jax.ShapeDtypeStruct((N_TOK, D), BF16),
            jax.ShapeDtypeStruct((N_TOK, D), BF16),
            jax.ShapeDtypeStruct((N_TOK, D), BF16),
            jax.ShapeDtypeStruct((N_TOK, LANES), F32),
            jax.ShapeDtypeStruct((N_TOK // CHUNK, HEADS, LANES), F32),
        ],
        scratch_shapes=[pltpu.VMEM((SUBLANES, 3 * D), F32)],
        compiler_params=_cp(("arbitrary",)),
        name="gdn_pre",
    )(x2, mod_l, nw8, w_main, wba, cw8, misc)


def _dot_nt(a, b):
    return lax.dot_general(a, b, (((1,), (1,)), ((), ())), preferred_element_type=F32)


def _dot_tn(a, b):
    return lax.dot_general(a, b, (((0,), (0,)), ((), ())), preferred_element_type=F32)


def _gdn_chunk_kernel(q_ref, k_ref, v_ref, gb_ref, gt_ref, o_ref, s_ref):
    @pl.when(pl.program_id(1) == 0)
    def _():
        s_ref[...] = jnp.zeros(s_ref.shape, F32)

    c = CHUNK
    nb = q_ref.shape[0]
    row = lax.broadcasted_iota(I32, (c, c), 0)
    col = lax.broadcasted_iota(I32, (c, c), 1)
    causal = col <= row
    strict = col < row
    eye = jnp.where(col == row, 1.0, 0.0)
    chains = [(b, hd) for b in range(nb) for hd in range(HEADS)]
    st = []
    for b, hd in chains:
        lo = hd * HEAD_DIM
        gb = gb_ref[b]
        q = q_ref[b, :, lo:lo + HEAD_DIM].astype(F32)
        k = k_ref[b, :, lo:lo + HEAD_DIM].astype(F32)
        v = v_ref[b, :, lo:lo + HEAD_DIM].astype(F32)
        beta = gb[:, hd:hd + 1]
        gcol = gb[:, HEADS + hd:HEADS + hd + 1]
        grow = gt_ref[b, hd:hd + 1, 0:c]
        glast = gcol[c - 1:c, :]
        egc = jnp.exp(gcol)
        kb = k * beta
        st.append(dict(
            decay=jnp.exp(jnp.where(causal, gcol - grow, -jnp.inf)),
            kq=jnp.concatenate([kb, q], axis=0).astype(BF16),
            kbf=k.astype(BF16),
            rhs=jnp.concatenate([v * beta, kb * egc], axis=-1).astype(BF16),
            qd=(q * egc).astype(BF16),
            kd=(k * jnp.exp(glast - gcol)).astype(BF16),
            eg=jnp.exp(glast)))
    for x in st:
        kk = _dot_nt(x["kq"], x["kbf"])
        x["p"] = jnp.where(strict, kk[:c] * x["decay"], 0.0)
        x["attn"] = jnp.where(causal, kk[c:] * x["decay"], 0.0).astype(BF16)
        x["t"] = eye - x["p"]
    for _ in range(5):
        for x in st:
            pb = x["p"].astype(BF16)
            x["p"] = _dot(pb, pb)
        for x in st:
            x["t"] = x["t"] + _dot(x["t"].astype(BF16), x["p"].astype(BF16))
    for x in st:
        x["uw"] = _dot(x["t"].astype(BF16), x["rhs"])
    for x, (b, hd) in zip(st, chains):
        s = s_ref[b * HEADS + hd]
        x["s"] = s
        ws = _dot(jnp.concatenate([x["uw"][:, HEAD_DIM:].astype(BF16), x["qd"]], axis=0), s.astype(BF16))
        x["vb"] = (x["uw"][:, :HEAD_DIM] - ws[:c]).astype(BF16)
        x["o"] = ws[c:]
    for x, (b, hd) in zip(st, chains):
        lo = hd * HEAD_DIM
        s_ref[b * HEADS + hd] = x["s"] * x["eg"] + _dot_tn(x["kd"], x["vb"])
        o_ref[b, :, lo:lo + HEAD_DIM] = (x["o"] + _dot(x["attn"], x["vb"])).astype(BF16)


def _gdn_chunks(q, k, v, gb, gt):
    nb = GDN_BATCH_PER_STEP
    n_chunks = SEQ // CHUNK
    tok = lambda b, c: (b, c, 0)
    o = pl.pallas_call(
        _gdn_chunk_kernel,
        grid=(BATCH // nb, n_chunks),
        in_specs=[
            pl.BlockSpec((nb, CHUNK, D), tok),
            pl.BlockSpec((nb, CHUNK, D), tok),
            pl.BlockSpec((nb, CHUNK, D), tok),
            pl.BlockSpec((nb, CHUNK, LANES), tok),
            pl.BlockSpec((nb, None, HEADS, LANES), lambda b, c: (b, c, 0, 0)),
        ],
        out_specs=pl.BlockSpec((nb, CHUNK, D), tok),
        out_shape=jax.ShapeDtypeStruct((BATCH, SEQ, D), BF16),
        scratch_shapes=[pltpu.VMEM((nb * HEADS, HEAD_DIM, HEAD_DIM), F32)],
        compiler_params=_cp(("arbitrary", "arbitrary")),
        name="gdn_chunks",
    )(q.reshape(BATCH, SEQ, D), k.reshape(BATCH, SEQ, D), v.reshape(BATCH, SEQ, D),
      gb.reshape(BATCH, SEQ, LANES), gt.reshape(BATCH, n_chunks, HEADS, LANES))
    return o.reshape(N_TOK, D)


def _gdn_post_kernel(o_ref, z_ref, x_ref, mod_ref, nw_ref, gnw_ref, wout_ref, out_ref):
    gnw = gnw_ref[0:1]
    parts = []
    for hd in range(HEADS):
        lo = hd * HEAD_DIM
        oh = o_ref[:, lo:lo + HEAD_DIM].astype(F32)
        zh = z_ref[:, lo:lo + HEAD_DIM].astype(F32)
        parts.append((_rms(oh) * gnw * _silu(zh)).astype(BF16))
    y = _dot(jnp.concatenate(parts, axis=-1), wout_ref[...])
    mod = mod_ref[...]
    out_ref[...] = x_ref[...] + mod[2:3] * (_rms(y) * nw_ref[1:2])


def _gdn_post(o, z, x2, mod_l, nw8, gdn_norm_w, w_out):
    tm = TM_MIX
    tiles_per_seq = SEQ // tm
    gnw8 = jnp.pad(gdn_norm_w.reshape(1, HEAD_DIM), ((0, SUBLANES - 1), (0, 0)))
    tok = lambda i: (i, 0)
    const = lambda i: (0, 0)
    return pl.pallas_call(
        _gdn_post_kernel,
        grid=(N_TOK // tm,),
        in_specs=[
            pl.BlockSpec((tm, D), tok),
            pl.BlockSpec((tm, D), tok),
            pl.BlockSpec((tm, D), tok),
            pl.BlockSpec((None, SUBLANES, D), lambda i: (i // tiles_per_seq, 0, 0)),
            pl.BlockSpec((SUBLANES, D), const),
            pl.BlockSpec((SUBLANES, HEAD_DIM), const),
            pl.BlockSpec((D, D), const),
        ],
        out_specs=pl.BlockSpec((tm, D), tok),
        out_shape=jax.ShapeDtypeStruct((N_TOK, D), F32),
        compiler_params=_cp(("arbitrary",)),
        name="gdn_post",
    )(o, z, x2, mod_l, nw8, gnw8, w_out.astype(BF16))


def kernel(x, c, ada_w, ada_b, norm_w, conv_in_w, conv_w, conv_out_w, gdn_in_w, gdn_conv_w, gdn_a_log,
           gdn_dt_bias, gdn_norm_w, gdn_out_w, moe_group_w, moe_group_b, moe_expert_w, moe_expert_b,
           moe_w_gate, moe_w_up, moe_w_down, shared_w_gate, shared_w_up, shared_w_down, shared_gate_w):
    mod = _ada_mod(c, ada_w, ada_b)
    nw8 = jnp.pad(norm_w, ((0, 0), (0, SUBLANES - norm_w.shape[1]), (0, 0)))
    x2 = x.reshape(N_TOK, D)

    def moe(x1, layer):
        return _hier_moe_block(
            x1, mod[layer], nw8[layer], layer,
            moe_group_w[layer], moe_group_b[layer], moe_expert_w[layer], moe_expert_b[layer],
            moe_w_gate, moe_w_up, moe_w_down,
            shared_w_gate[layer], shared_w_up[layer], shared_w_down[layer], shared_gate_w[layer])

    x2 = _conv_mixer(x2, mod[0], nw8[0], conv_in_w[0], conv_w[0], conv_out_w[0])
    x2 = moe(x2, 0)
    q, k, v, z, gb, gt = _gdn_pre(x2, mod[1], nw8[1], gdn_in_w[0], gdn_conv_w[0], gdn_a_log[0], gdn_dt_bias[0])
    o = _gdn_chunks(q, k, v, gb, gt)
    x2 = _gdn_post(o, z, x2, mod[1], nw8[1], gdn_norm_w[0], gdn_out_w[0])
    x2 = moe(x2, 1)
    return x2.reshape(BATCH, SEQ, D)
```

```python
import functools

import jax
import jax.numpy as jnp
from jax import lax
from jax.experimental import pallas as pl
from jax.experimental.pallas import tpu as pltpu
from jax.experimental.pallas import tpu_sc as plsc

F32 = jnp.float32
BF16 = jnp.bfloat16
I32 = jnp.int32
HIGHEST = lax.Precision.HIGHEST

D = 1024
BATCH = 4
SEQ = 4096
N_TOK = BATCH * SEQ
HEADS = 8
HEAD_DIM = 128
CHUNK = 64
CHUNK_SHIFT = 6
N_GROUPS = 8
N_EXPERTS = 64
D_EXPERT = 256
D_SHARED = 512
EPS = 1e-6
LANES = 128
SUBLANES = 8
EXPERT_LANE0 = N_GROUPS

MOE_BLOCK = 256
N_SLOTS = N_TOK * 2
N_BLOCKS = N_SLOTS // MOE_BLOCK + N_EXPERTS
P_ROWS = N_BLOCKS * MOE_BLOCK

TM_MIX = 512
TM_GDN = 256
TM_DMA = 256
TM_POS = 2048
GDN_BATCH_PER_STEP = 4
VMEM_LIMIT = 56 * 1024 * 1024


def _cp(sem):
    return pltpu.CompilerParams(dimension_semantics=sem, vmem_limit_bytes=VMEM_LIMIT)


def _rms(x):
    return x * lax.rsqrt(jnp.mean(x * x, axis=-1, keepdims=True) + EPS)


def _silu(x):
    return x * jax.nn.sigmoid(x)


def _dot(a, b):
    return jnp.dot(a, b, preferred_element_type=F32)


def _causal_taps(x, tail_ref, taps, first_of_seq):
    @pl.when(first_of_seq)
    def _():
        tail_ref[...] = jnp.zeros(tail_ref.shape, F32)

    tail = tail_ref[...]
    sub = lax.broadcasted_iota(I32, tail.shape, 0)
    out = []
    for s in taps:
        rolled = pltpu.roll(x, s, axis=0)
        head = jnp.where(sub < s, pltpu.roll(tail, s, axis=0), rolled[0:SUBLANES])
        out.append(jnp.concatenate([head, rolled[SUBLANES:]], axis=0))
    tail_ref[...] = x[x.shape[0] - SUBLANES:]
    return out


def _ada_kernel(c_ref, w_ref, b_ref, o_ref):
    cs = _silu(c_ref[...])
    o_ref[0] = _dot(cs.astype(BF16), w_ref[0].astype(BF16)) + b_ref[0]


def _ada_mod(c, ada_w, ada_b):
    depth = ada_w.shape[0]
    tn = 1024
    c8 = jnp.pad(c, ((0, SUBLANES - BATCH), (0, 0)))
    mod = pl.pallas_call(
        _ada_kernel,
        grid=(depth, 6 * D // tn),
        in_specs=[
            pl.BlockSpec((SUBLANES, D), lambda l, j: (0, 0)),
            pl.BlockSpec((1, D, tn), lambda l, j: (l, 0, j)),
            pl.BlockSpec((1, 1, tn), lambda l, j: (l, 0, j)),
        ],
        out_specs=pl.BlockSpec((1, SUBLANES, tn), lambda l, j: (l, 0, j)),
        out_shape=jax.ShapeDtypeStruct((depth, SUBLANES, 6 * D), F32),
        compiler_params=_cp(("arbitrary", "arbitrary")),
        name="ada_mod",
    )(c8, ada_w, ada_b.reshape(depth, 1, 6 * D))
    mod = mod[:, :BATCH].reshape(depth, BATCH, 6, D)
    return jnp.pad(mod, ((0, 0), (0, 0), (0, 2), (0, 0)))


def _conv_mixer_kernel(x_ref, mod_ref, nw_ref, win_ref, cw_ref, wout_ref, o_ref, tail_ref, *, tiles_per_seq):
    x = x_ref[...]
    mod = mod_ref[...]
    nw = nw_ref[...]
    h = _rms(x) * nw[0:1] * (1.0 + mod[1:2]) + mod[0:1]
    bcx = _dot(h.astype(BF16), win_ref[...])
    u = bcx[:, D:2 * D] * bcx[:, 2 * D:]
    u2, u1 = _causal_taps(u, tail_ref, (2, 1), pl.program_id(0) % tiles_per_seq == 0)
    cw = cw_ref[...]
    conv = cw[0:1] * u2 + cw[1:2] * u1 + cw[2:3] * u
    y = _dot((bcx[:, :D] * conv).astype(BF16), wout_ref[...])
    o_ref[...] = x + mod[2:3] * (_rms(y) * nw[1:2])


def _conv_mixer(x2, mod_l, nw8, w_in, conv_w, w_out):
    tm = TM_MIX
    tiles_per_seq = SEQ // tm
    cw8 = jnp.pad(conv_w, ((0, SUBLANES - conv_w.shape[0]), (0, 0)))
    return pl.pallas_call(
        functools.partial(_conv_mixer_kernel, tiles_per_seq=tiles_per_seq),
        grid=(N_TOK // tm,),
        in_specs=[
            pl.BlockSpec((tm, D), lambda i: (i, 0)),
            pl.BlockSpec((None, SUBLANES, D), lambda i: (i // tiles_per_seq, 0, 0)),
            pl.BlockSpec((SUBLANES, D), lambda i: (0, 0)),
            pl.BlockSpec((D, 3 * D), lambda i: (0, 0)),
            pl.BlockSpec((SUBLANES, D), lambda i: (0, 0)),
            pl.BlockSpec((D, D), lambda i: (0, 0)),
        ],
        out_specs=pl.BlockSpec((tm, D), lambda i: (i, 0)),
        out_shape=jax.ShapeDtypeStruct((N_TOK, D), F32),
        scratch_shapes=[pltpu.VMEM((SUBLANES, D), F32)],
        compiler_params=_cp(("arbitrary",)),
        name="conv_mixer",
    )(x2, mod_l, nw8, w_in.astype(BF16), cw8, w_out.astype(BF16))


def _lane_pick(lane, mask_val_pairs):
    out = jnp.zeros(lane.shape, F32)
    for idx, val in mask_val_pairs:
        out = jnp.where(lane == float(idx), val, out)
    return out


def _moe_pre_kernel(x_ref, mod_ref, nw_ref, wr_ref, br_ref, wgu_ref, wd_ref, wsg_ref,
                    h_ref, sh_ref, route_ref, cnt_ref, carry_ref):
    tm = x_ref.shape[0]
    x = x_ref[...]
    mod = mod_ref[...]
    nw = nw_ref[...]
    h = _rms(x) * nw[2:3] * (1.0 + mod[4:5]) + mod[3:4]
    h_ref[...] = h
    hb = h.astype(BF16)

    h_lo = (h - hb.astype(F32)).astype(BF16)
    hw = _dot(hb, wr_ref[...]) + _dot(h_lo, wr_ref[...])
    logits = hw[:, :LANES] + hw[:, LANES:] + br_ref[0:1]
    lane_i = lax.broadcasted_iota(I32, (tm, LANES), 1)
    lane = lane_i.astype(F32)
    neg = jnp.float32(-jnp.inf)
    is_group = lane_i < N_GROUPS
    gl = jnp.where(is_group, logits, neg)
    gmax = jnp.max(gl, axis=-1, keepdims=True)
    gsel = jnp.min(jnp.where(gl == gmax, lane, float(LANES)), axis=-1, keepdims=True)
    psel = 1.0 / jnp.sum(jnp.where(is_group, jnp.exp(logits - gmax), 0.0), axis=-1, keepdims=True)
    lane_group = ((lane_i - EXPERT_LANE0) >> 3).astype(F32)
    in_group = (lane_i >= EXPERT_LANE0) & (lane_i < EXPERT_LANE0 + N_EXPERTS) & (lane_group == gsel)
    el = jnp.where(in_group, logits, neg)
    v1 = jnp.max(el, axis=-1, keepdims=True)
    i1 = jnp.min(jnp.where(el == v1, lane, float(LANES)), axis=-1, keepdims=True)
    el2 = jnp.where(lane == i1, neg, el)
    v2 = jnp.max(el2, axis=-1, keepdims=True)
    i2 = jnp.min(jnp.where(el2 == v2, lane, float(LANES)), axis=-1, keepdims=True)
    e2 = jnp.exp(v2 - v1)
    g1 = psel / (1.0 + e2)
    g2 = psel * e2 / (1.0 + e2)

    @pl.when(pl.program_id(0) == 0)
    def _():
        carry_ref[...] = jnp.zeros(carry_ref.shape, F32)

    oh1 = lane == i1
    oh2 = lane == i2
    cnt = jnp.where(oh1 | oh2, 1.0, 0.0).astype(BF16)
    row = lax.broadcasted_iota(I32, (tm, tm), 0)
    col = lax.broadcasted_iota(I32, (tm, tm), 1)
    tri = jnp.where(col < row, 1.0, 0.0).astype(BF16)
    pre = _dot(tri, cnt) + carry_ref[0:1]
    r1 = jnp.sum(jnp.where(oh1, pre, 0.0), axis=-1, keepdims=True)
    r2 = jnp.sum(jnp.where(oh2, pre, 0.0), axis=-1, keepdims=True)
    carry_ref[...] = carry_ref[...] + jnp.sum(cnt.astype(F32), axis=0, keepdims=True)
    cnt_ref[...] = carry_ref[...]
    route_ref[...] = _lane_pick(lane, [
        (0, i1 - EXPERT_LANE0), (1, i2 - EXPERT_LANE0),
        (2, g1), (3, g2), (4, r1), (5, r2)])

    gu = _dot(hb, wgu_ref[...])
    hid = _silu(gu[:, :D_SHARED]) * gu[:, D_SHARED:]
    ys = _dot(hid.astype(BF16), wd_ref[...])
    sg = jax.nn.sigmoid(jnp.sum(h * wsg_ref[0:1], axis=-1, keepdims=True))
    sh_ref[...] = sg * ys


def _moe_pre(x1, mod_l, nw8, w_rg, b_rg, w_re, b_re, ws_gate, ws_up, ws_down, w_sg):
    tm = TM_MIX
    tiles_per_seq = SEQ // tm
    pad_l = LANES - N_GROUPS - N_EXPERTS
    wr = jnp.pad(jnp.concatenate([w_rg, w_re], axis=1), ((0, 0), (0, pad_l)))
    wr_hi = wr.astype(BF16)
    wr = jnp.concatenate([wr_hi, (wr - wr_hi.astype(F32)).astype(BF16)], axis=1)
    br = jnp.pad(jnp.concatenate([b_rg, b_re])[None, :], ((0, SUBLANES - 1), (0, pad_l)))
    wgu = jnp.concatenate([ws_gate, ws_up], axis=1).astype(BF16)
    wsg8 = jnp.pad(w_sg.reshape(1, D), ((0, SUBLANES - 1), (0, 0)))
    const = lambda i: (0, 0)
    return pl.pallas_call(
        _moe_pre_kernel,
        grid=(N_TOK // tm,),
        in_specs=[
            pl.BlockSpec((tm, D), lambda i: (i, 0)),
            pl.BlockSpec((None, SUBLANES, D), lambda i: (i // tiles_per_seq, 0, 0)),
            pl.BlockSpec((SUBLANES, D), const),
            pl.BlockSpec((D, 2 * LANES), const),
            pl.BlockSpec((SUBLANES, LANES), const),
            pl.BlockSpec((D, 2 * D_SHARED), const),
            pl.BlockSpec((D_SHARED, D), const),
            pl.BlockSpec((SUBLANES, D), const),
        ],
        out_specs=[
            pl.BlockSpec((tm, D), lambda i: (i, 0)),
            pl.BlockSpec((tm, D), lambda i: (i, 0)),
            pl.BlockSpec((tm, LANES), lambda i: (i, 0)),
            pl.BlockSpec((SUBLANES, LANES), const),
        ],
        out_shape=[
            jax.ShapeDtypeStruct((N_TOK, D), F32),
            jax.ShapeDtypeStruct((N_TOK, D), F32),
            jax.ShapeDtypeStruct((N_TOK, LANES), F32),
            jax.ShapeDtypeStruct((SUBLANES, LANES), F32),
        ],
        scratch_shapes=[pltpu.VMEM((SUBLANES, LANES), F32)],
        compiler_params=_cp(("arbitrary",)),
        name="moe_pre",
    )(x1, mod_l, nw8, wr, br, wgu, ws_down.astype(BF16), wsg8)


def _n_blocks_per_expert(cnt_row):
    return jnp.floor((cnt_row + (MOE_BLOCK - 1)) * (1.0 / MOE_BLOCK))


def _moe_pos_kernel(cnt_ref, route_ref, pos_ref, seg_ref):
    tm = route_ref.shape[0]
    nb = _n_blocks_per_expert(cnt_ref[...])
    r = lax.broadcasted_iota(I32, (LANES, LANES), 0)
    c = lax.broadcasted_iota(I32, (LANES, LANES), 1)
    excl = jnp.dot(nb, jnp.where(r < c, 1.0, 0.0), precision=HIGHEST, preferred_element_type=F32)
    pstart = excl[0:1] * MOE_BLOCK
    route = route_ref[...]
    lane = lax.broadcasted_iota(I32, (tm, LANES), 1).astype(F32)
    e1 = route[:, 0:1] + EXPERT_LANE0
    e2 = route[:, 1:2] + EXPERT_LANE0
    p1 = jnp.sum(jnp.where(lane == e1, pstart, 0.0), axis=-1, keepdims=True) + route[:, 4:5]
    p2 = jnp.sum(jnp.where(lane == e2, pstart, 0.0), axis=-1, keepdims=True) + route[:, 5:6]
    pos_ref[...] = _lane_pick(lane, [(0, p1), (1, p2)]).astype(I32)

    @pl.when(pl.program_id(0) == 0)
    def _():
        sub = lax.broadcasted_iota(I32, (SUBLANES, LANES), 0)
        seg_ref[...] = jnp.where(sub == 0, nb, excl).astype(I32)


def _moe_pos(cnt, route):
    tm = TM_POS
    pos, seg = pl.pallas_call(
        _moe_pos_kernel,
        grid=(N_TOK // tm,),
        in_specs=[
            pl.BlockSpec((SUBLANES, LANES), lambda i: (0, 0)),
            pl.BlockSpec((tm, LANES), lambda i: (i, 0)),
        ],
        out_specs=[
            pl.BlockSpec((tm, LANES), lambda i: (i, 0)),
            pl.BlockSpec((SUBLANES, LANES), lambda i: (0, 0)),
        ],
        out_shape=[
            jax.ShapeDtypeStruct((N_TOK, LANES), I32),
            jax.ShapeDtypeStruct((SUBLANES, LANES), I32),
        ],
        compiler_params=_cp(("arbitrary",)),
        name="moe_pos",
    )(cnt, route)
    experts = slice(EXPERT_LANE0, EXPERT_LANE0 + N_EXPERTS)
    return pos[:, :2].T.reshape(-1), seg[0, experts], seg[1, experts]


def _expert_kernel(nblk_ref, blk0_ref, buf_ref, wg_ref, wu_ref, wd_ref, eo_ref,
                   xin_ref, out_ref, wgb_ref, wub_ref, wdb_ref, sem_in, sem_out):
    e = pl.program_id(0)
    last = pl.num_programs(0) - 1
    n = nblk_ref[e]
    blk0 = blk0_ref[e]
    total = blk0_ref[last] + nblk_ref[last]
    bm = MOE_BLOCK
    n_in = xin_ref.shape[0]

    def rows(g):
        return pl.ds(pl.multiple_of(g * bm, bm), bm)

    def in_copy(g):
        slot = lax.rem(g, n_in)
        return pltpu.make_async_copy(buf_ref.at[rows(g)], xin_ref.at[slot], sem_in.at[slot])

    def out_copy(g):
        return pltpu.make_async_copy(out_ref.at[g & 1], eo_ref.at[rows(g)], sem_out.at[g & 1])

    @pl.when(n > 0)
    def _():
        wgb_ref[...] = wg_ref[...].astype(BF16)
        wub_ref[...] = wu_ref[...].astype(BF16)
        wdb_ref[...] = wd_ref[...].astype(BF16)

    def block(j, carry):
        g = blk0 + j

        @pl.when(g == 0)
        def _():
            in_copy(g).start()

            @pl.when(total > 1)
            def _():
                in_copy(g + 1).start()

        @pl.when(g + 2 < total)
        def _():
            in_copy(g + 2).start()

        in_copy(g).wait()

        @pl.when(g >= 2)
        def _():
            out_copy(g - 2).wait()

        xb = xin_ref[lax.rem(g, n_in)].astype(BF16)
        hid = _silu(_dot(xb, wgb_ref[...])) * _dot(xb, wub_ref[...])
        out_ref[g & 1] = _dot(hid.astype(BF16), wdb_ref[...])
        out_copy(g).start()
        return carry

    lax.fori_loop(0, n, block, 0)

    @pl.when((e == last) & (total >= 2))
    def _():
        out_copy(total - 2).wait()

    @pl.when((e == last) & (total >= 1))
    def _():
        out_copy(total - 1).wait()


def _moe_experts(nblk_e, blk0_e, buf, w_gate, w_up, w_down, layer):
    bm = MOE_BLOCK
    wspec = lambda shape: pl.BlockSpec((None, None) + shape, lambda e, nb, b0: (layer, e, 0, 0))
    return pl.pallas_call(
        _expert_kernel,
        grid_spec=pltpu.PrefetchScalarGridSpec(
            num_scalar_prefetch=2,
            grid=(N_EXPERTS,),
            in_specs=[
                pl.BlockSpec(memory_space=pl.ANY),
                wspec((D, D_EXPERT)),
                wspec((D, D_EXPERT)),
                wspec((D_EXPERT, D)),
            ],
            out_specs=pl.BlockSpec(memory_space=pl.ANY),
            scratch_shapes=[
                pltpu.VMEM((3, bm, D), F32),
                pltpu.VMEM((2, bm, D), F32),
                pltpu.VMEM((D, D_EXPERT), BF16),
                pltpu.VMEM((D, D_EXPERT), BF16),
                pltpu.VMEM((D_EXPERT, D), BF16),
                pltpu.SemaphoreType.DMA((3,)),
                pltpu.SemaphoreType.DMA((2,)),
            ],
        ),
        out_shape=jax.ShapeDtypeStruct((P_ROWS, D), F32),
        compiler_params=_cp(("arbitrary",)),
        name="moe_experts",
    )(nblk_e, blk0_e, buf, w_gate, w_up, w_down)


SC_CORES = 2
SC_SUBCORES = 16
SC_WORKERS = SC_CORES * SC_SUBCORES
SC_CHUNK = 32


def _sc_gather_store(table_hbm, idx_v, out_hbm, out_base, n_rows, bufs, sems):
    n_chunks = n_rows // SC_CHUNK
    assert n_chunks % 2 == 0

    def gather(j, b):
        off = pl.multiple_of(j * SC_CHUNK, SC_CHUNK)
        return pltpu.make_async_copy(table_hbm.at[idx_v.at[pl.ds(off, SC_CHUNK)]], bufs[b], sems[b])

    gather(0, 0).start()

    @pl.loop(0, n_chunks, step=2)
    def _(j):
        for b in range(2):
            jj = j + b

            @pl.when(jj + 1 < n_chunks)
            def _():
                gather(jj + 1, 1 - b).start()

            gather(jj, b).wait()
            row0 = pl.multiple_of(out_base + jj * SC_CHUNK, SC_CHUNK)
            pltpu.sync_copy(bufs[b], out_hbm.at[pl.ds(row0, SC_CHUNK)])


def _sc_row_buffers(width, dtype):
    return [pltpu.VMEM((SC_CHUNK, width), dtype), pltpu.VMEM((SC_CHUNK, width), dtype),
            pltpu.SemaphoreType.DMA, pltpu.SemaphoreType.DMA]


def _sc_gather_rows(table, idx):
    n_idx = idx.shape[0]
    width = table.shape[1]
    per_w = n_idx // SC_WORKERS
    mesh = plsc.VectorSubcoreMesh(core_axis_name="c", subcore_axis_name="s")

    def body(table_hbm, idx_hbm, out_hbm, idx_v, buf0, buf1, sem0, sem1):
        wid = lax.axis_index("s") * SC_CORES + lax.axis_index("c")
        base = wid * per_w
        pltpu.sync_copy(idx_hbm.at[pl.ds(base, per_w)], idx_v)
        _sc_gather_store(table_hbm, idx_v, out_hbm, base, per_w, (buf0, buf1), (sem0, sem1))

    return pl.kernel(
        body,
        out_type=jax.ShapeDtypeStruct((n_idx, width), table.dtype),
        mesh=mesh,
        scratch_types=[pltpu.VMEM((per_w,), I32)] + _sc_row_buffers(width, table.dtype),
        name="sc_gather_rows",
    )(table, idx)


def _sc_dispatch_rows(h, pos_km):
    per_w = P_ROWS // SC_WORKERS
    lanes = 16
    mesh = plsc.VectorSubcoreMesh(core_axis_name="c", subcore_axis_name="s")

    def body(h_hbm, pos_hbm, buf_hbm, pos_v, tok_v, buf0, buf1, sem0, sem1):
        wid = lax.axis_index("s") * SC_CORES + lax.axis_index("c")
        base = wid * per_w
        pltpu.sync_copy(pos_hbm, pos_v)

        @pl.loop(0, per_w // lanes)
        def _(i):
            off = pl.multiple_of(i * lanes, lanes)
            tok_v[pl.ds(off, lanes)] = (base + off + lax.iota(I32, lanes)) & (N_TOK - 1)

        @pl.loop(0, N_SLOTS // lanes)
        def _(i):
            off = pl.multiple_of(i * lanes, lanes)
            local = pos_v[pl.ds(off, lanes)] - base
            mine = (local >= 0) & (local < per_w)
            slot = off + lax.iota(I32, lanes)
            plsc.store_scatter(tok_v, [jnp.where(mine, local, 0)], slot & (N_TOK - 1), mask=mine)

        _sc_gather_store(h_hbm, tok_v, buf_hbm, base, per_w, (buf0, buf1), (sem0, sem1))

    return pl.kernel(
        body,
        out_type=jax.ShapeDtypeStruct((P_ROWS, h.shape[1]), h.dtype),
        mesh=mesh,
        scratch_types=[pltpu.VMEM((N_SLOTS,), I32), pltpu.VMEM((per_w,), I32)]
        + _sc_row_buffers(h.shape[1], h.dtype),
        compiler_params=pltpu.CompilerParams(needs_layout_passes=False),
        name="sc_dispatch_rows",
    )(h, pos_km)


def _combine_kernel(y0_ref, y1_ref, x_ref, sh_ref, route_ref, mod_ref, nw_ref, o_ref):
    route = route_ref[...]
    y = route[:, 2:3] * y0_ref[...] + route[:, 3:4] * y1_ref[...] + sh_ref[...]
    mod = mod_ref[...]
    o_ref[...] = x_ref[...] + mod[5:6] * (_rms(y) * nw_ref[3:4])


def _moe_combine(pos_km, eo, x1, sh, route, mod_l, nw8):
    tm = TM_MIX
    tiles_per_seq = SEQ // tm
    n_tiles = N_TOK // tm
    y2 = _sc_gather_rows(eo, pos_km)
    return pl.pallas_call(
        _combine_kernel,
        grid=(n_tiles,),
        in_specs=[
            pl.BlockSpec((tm, D), lambda i: (i, 0)),
            pl.BlockSpec((tm, D), lambda i: (i + n_tiles, 0)),
            pl.BlockSpec((tm, D), lambda i: (i, 0)),
            pl.BlockSpec((tm, D), lambda i: (i, 0)),
            pl.BlockSpec((tm, LANES), lambda i: (i, 0)),
            pl.BlockSpec((None, SUBLANES, D), lambda i: (i // tiles_per_seq, 0, 0)),
            pl.BlockSpec((SUBLANES, D), lambda i: (0, 0)),
        ],
        out_specs=pl.BlockSpec((tm, D), lambda i: (i, 0)),
        out_shape=jax.ShapeDtypeStruct((N_TOK, D), F32),
        compiler_params=_cp(("arbitrary",)),
        name="moe_combine",
    )(y2, y2, x1, sh, route, mod_l, nw8)


def _hier_moe_block(x1, mod_l, nw8, layer, w_rg, b_rg, w_re, b_re, w_gate, w_up, w_down,
                    ws_gate, ws_up, ws_down, w_sg):
    h, sh, route, cnt = _moe_pre(x1, mod_l, nw8, w_rg, b_rg, w_re, b_re, ws_gate, ws_up, ws_down, w_sg)
    pos_km, nblk_e, blk0_e = _moe_pos(cnt, route)
    buf = _sc_dispatch_rows(h, pos_km)
    eo = _moe_experts(nblk_e, blk0_e, buf, w_gate, w_up, w_down, layer)
    return _moe_combine(pos_km, eo, x1, sh, route, mod_l, nw8)


def _gdn_pre_kernel(x_ref, mod_ref, nw_ref, w_ref, wba_ref, cw_ref, misc_ref,
                    q_ref, k_ref, v_ref, z_ref, gb_ref, gt_ref, tail_ref, *, tiles_per_seq):
    tm = x_ref.shape[0]
    qkv_w = 3 * D
    x = x_ref[...]
    mod = mod_ref[...]
    nw = nw_ref[...]
    h = _rms(x) * nw[0:1] * (1.0 + mod[1:2]) + mod[0:1]
    hb = h.astype(BF16)
    proj = _dot(hb, w_ref[...])
    z_ref[...] = proj[:, qkv_w:].astype(BF16)
    pre = proj[:, :qkv_w]
    p3, p2, p1 = _causal_taps(pre, tail_ref, (3, 2, 1), pl.program_id(0) % tiles_per_seq == 0)
    cw = cw_ref[...]
    conv = cw[0:1] * p3 + cw[1:2] * p2 + cw[2:3] * p1 + cw[3:4] * pre
    act = _silu(conv)
    for hd in range(HEADS):
        lo = hd * HEAD_DIM
        qh = act[:, lo:lo + HEAD_DIM]
        kh = act[:, D + lo:D + lo + HEAD_DIM]
        qn = qh * lax.rsqrt(jnp.sum(qh * qh, axis=-1, keepdims=True) + EPS) * (HEAD_DIM ** -0.5)
        kn = kh * lax.rsqrt(jnp.sum(kh * kh, axis=-1, keepdims=True) + EPS)
        q_ref[:, lo:lo + HEAD_DIM] = qn.astype(BF16)
        k_ref[:, lo:lo + HEAD_DIM] = kn.astype(BF16)
    v_ref[...] = act[:, 2 * D:].astype(BF16)

    ba = _dot(hb, wba_ref[...])
    misc = misc_ref[...]
    beta = jax.nn.sigmoid(ba)
    sp_in = ba + misc[1:2]
    softplus = jnp.maximum(sp_in, 0.0) + jnp.log(1.0 + jnp.exp(-jnp.abs(sp_in)))
    g = -jnp.exp(misc[0:1]) * softplus
    row = lax.broadcasted_iota(I32, (tm, tm), 0)
    col = lax.broadcasted_iota(I32, (tm, tm), 1)
    tri = jnp.where((col <= row) & ((col >> CHUNK_SHIFT) == (row >> CHUNK_SHIFT)), 1.0, 0.0)
    gc = jnp.dot(tri, g, precision=HIGHEST, preferred_element_type=F32)
    lane = lax.broadcasted_iota(I32, (tm, LANES), 1)
    gb = jnp.where(lane < HEADS, beta, gc)
    gb_ref[...] = gb
    for c in range(tm // CHUNK):
        blk = jnp.concatenate([gb[c * CHUNK:(c + 1) * CHUNK], jnp.zeros((LANES - CHUNK, LANES), F32)], axis=0)
        gt_ref[c] = blk.T[HEADS:2 * HEADS, :]


def _gdn_pre(x2, mod_l, nw8, w_in, conv_w, a_log, dt_bias):
    tm = TM_GDN
    tiles_per_seq = SEQ // tm
    qkvz = 4 * D
    w_main = w_in[:, :qkvz].astype(BF16)
    wba = jnp.pad(w_in[:, qkvz:], ((0, 0), (0, LANES - 2 * HEADS))).astype(BF16)
    cw8 = jnp.pad(conv_w, ((0, SUBLANES - conv_w.shape[0]), (0, 0)))
    misc = jnp.zeros((SUBLANES, LANES), F32)
    misc = misc.at[0, HEADS:2 * HEADS].set(a_log).at[1, HEADS:2 * HEADS].set(dt_bias)
    const = lambda i: (0, 0)
    tok = lambda i: (i, 0)
    return pl.pallas_call(
        functools.partial(_gdn_pre_kernel, tiles_per_seq=tiles_per_seq),
        grid=(N_TOK // tm,),
        in_specs=[
            pl.BlockSpec((tm, D), tok),
            pl.BlockSpec((None, SUBLANES, D), lambda i: (i // tiles_per_seq, 0, 0)),
            pl.BlockSpec((SUBLANES, D), const),
            pl.BlockSpec((D, qkvz), const),
            pl.BlockSpec((D, LANES), const),
            pl.BlockSpec((SUBLANES, 3 * D), const),
            pl.BlockSpec((SUBLANES, LANES), const),
        ],
        out_specs=[
            pl.BlockSpec((tm, D), tok),
            pl.BlockSpec((tm, D), tok),
            pl.BlockSpec((tm, D), tok),
            pl.BlockSpec((tm, D), tok),
            pl.BlockSpec((tm, LANES), tok),
            pl.BlockSpec((tm // CHUNK, HEADS, LANES), lambda i: (i, 0, 0)),
        ],
        out_shape=[
            jax.ShapeDtypeStruct((N_TOK, D), BF16),
            jax.ShapeDtypeStruct((N_TOK, D), BF16),
            jax.ShapeDtypeStruct((N_TOK, D), BF16),
            jax.ShapeDtypeStruct((N_TOK, D), BF16),
            jax.ShapeDtypeStruct((N_TOK, LANES), F32),
            jax.ShapeDtypeStruct((N_TOK // CHUNK, HEADS, LANES), F32),
        ],
        scratch_shapes=[pltpu.VMEM((SUBLANES, 3 * D), F32)],
        compiler_params=_cp(("arbitrary",)),
        name="gdn_pre",
    )(x2, mod_l, nw8, w_main, wba, cw8, misc)


def _dot_nt(a, b):
    return lax.dot_general(a, b, (((1,), (1,)), ((), ())), preferred_element_type=F32)


def _dot_tn(a, b):
    return lax.dot_general(a, b, (((0,), (0,)), ((), ())), preferred_element_type=F32)


def _gdn_chunk_kernel(q_ref, k_ref, v_ref, gb_ref, gt_ref, o_ref, s_ref):
    @pl.when(pl.program_id(1) == 0)
    def _():
        s_ref[...] = jnp.zeros(s_ref.shape, F32)

    c = CHUNK
    nb = q_ref.shape[0]
    row = lax.broadcasted_iota(I32, (c, c), 0)
    col = lax.broadcasted_iota(I32, (c, c), 1)
    causal = col <= row
    strict = col < row
    eye = jnp.where(col == row, 1.0, 0.0)
    chains = [(b, hd) for b in range(nb) for hd in range(HEADS)]
    st = []
    for b, hd in chains:
        lo = hd * HEAD_DIM
        gb = gb_ref[b]
        q = q_ref[b, :, lo:lo + HEAD_DIM].astype(F32)
        k = k_ref[b, :, lo:lo + HEAD_DIM].astype(F32)
        v = v_ref[b, :, lo:lo + HEAD_DIM].astype(F32)
        beta = gb[:, hd:hd + 1]
        gcol = gb[:, HEADS + hd:HEADS + hd + 1]
        grow = gt_ref[b, hd:hd + 1, 0:c]
        glast = gcol[c - 1:c, :]
        egc = jnp.exp(gcol)
        kb = k * beta
        st.append(dict(
            decay=jnp.exp(jnp.where(causal, gcol - grow, -jnp.inf)),
            kq=jnp.concatenate([kb, q], axis=0).astype(BF16),
            kbf=k.astype(BF16),
            rhs=jnp.concatenate([v * beta, kb * egc], axis=-1).astype(BF16),
            qd=(q * egc).astype(BF16),
            kd=(k * jnp.exp(glast - gcol)).astype(BF16),
            eg=jnp.exp(glast)))
    for x in st:
        kk = _dot_nt(x["kq"], x["kbf"])
        x["p"] = jnp.where(strict, kk[:c] * x["decay"], 0.0)
        x["attn"] = jnp.where(causal, kk[c:] * x["decay"], 0.0).astype(BF16)
        x["t"] = eye - x["p"]
    for _ in range(5):
        for x in st:
            pb = x["p"].astype(BF16)
            x["p"] = _dot(pb, pb)
        for x in st:
            x["t"] = x["t"] + _dot(x["t"].astype(BF16), x["p"].astype(BF16))
    for x in st:
        x["uw"] = _dot(x["t"].astype(BF16), x["rhs"])
    for x, (b, hd) in zip(st, chains):
        s = s_ref[b * HEADS + hd]
        x["s"] = s
        ws = _dot(jnp.concatenate([x["uw"][:, HEAD_DIM:].astype(BF16), x["qd"]], axis=0), s.astype(BF16))
        x["vb"] = (x["uw"][:, :HEAD_DIM] - ws[:c]).astype(BF16)
        x["o"] = ws[c:]
    for x, (b, hd) in zip(st, chains):
        lo = hd * HEAD_DIM
        s_ref[b * HEADS + hd] = x["s"] * x["eg"] + _dot_tn(x["kd"], x["vb"])
        o_ref[b, :, lo:lo + HEAD_DIM] = (x["o"] + _dot(x["attn"], x["vb"])).astype(BF16)


def _gdn_chunks(q, k, v, gb, gt):
    nb = GDN_BATCH_PER_STEP
    n_chunks = SEQ // CHUNK
    tok = lambda b, c: (b, c, 0)
    o = pl.pallas_call(
        _gdn_chunk_kernel,
        grid=(BATCH // nb, n_chunks),
        in_specs=[
            pl.BlockSpec((nb, CHUNK, D), tok),
            pl.BlockSpec((nb, CHUNK, D), tok),
            pl.BlockSpec((nb, CHUNK, D), tok),
            pl.BlockSpec((nb, CHUNK, LANES), tok),
            pl.BlockSpec((nb, None, HEADS, LANES), lambda b, c: (b, c, 0, 0)),
        ],
        out_specs=pl.BlockSpec((nb, CHUNK, D), tok),
        out_shape=jax.ShapeDtypeStruct((BATCH, SEQ, D), BF16),
        scratch_shapes=[pltpu.VMEM((nb * HEADS, HEAD_DIM, HEAD_DIM), F32)],
        compiler_params=_cp(("arbitrary", "arbitrary")),
        name="gdn_chunks",
    )(q.reshape(BATCH, SEQ, D), k.reshape(BATCH, SEQ, D), v.reshape(BATCH, SEQ, D),
      gb.reshape(BATCH, SEQ, LANES), gt.reshape(BATCH, n_chunks, HEADS, LANES))
    return o.reshape(N_TOK, D)


def _gdn_post_kernel(o_ref, z_ref, x_ref, mod_ref, nw_ref, gnw_ref, wout_ref, out_ref):
    gnw = gnw_ref[0:1]
    parts = []
    for hd in range(HEADS):
        lo = hd * HEAD_DIM
        oh = o_ref[:, lo:lo + HEAD_DIM].astype(F32)
        zh = z_ref[:, lo:lo + HEAD_DIM].astype(F32)
        parts.append((_rms(oh) * gnw * _silu(zh)).astype(BF16))
    y = _dot(jnp.concatenate(parts, axis=-1), wout_ref[...])
    mod = mod_ref[...]
    out_ref[...] = x_ref[...] + mod[2:3] * (_rms(y) * nw_ref[1:2])


def _gdn_post(o, z, x2, mod_l, nw8, gdn_norm_w, w_out):
    tm = TM_MIX
    tiles_per_seq = SEQ // tm
    gnw8 = jnp.pad(gdn_norm_w.reshape(1, HEAD_DIM), ((0, SUBLANES - 1), (0, 0)))
    tok = lambda i: (i, 0)
    const = lambda i: (0, 0)
    return pl.pallas_call(
        _gdn_post_kernel,
        grid=(N_TOK // tm,),
        in_specs=[
            pl.BlockSpec((tm, D), tok),
            pl.BlockSpec((tm, D), tok),
            pl.BlockSpec((tm, D), tok),
            pl.BlockSpec((None, SUBLANES, D), lambda i: (i // tiles_per_seq, 0, 0)),
            pl.BlockSpec((SUBLANES, D), const),
            pl.BlockSpec((SUBLANES, HEAD_DIM), const),
            pl.BlockSpec((D, D), const),
        ],
        out_specs=pl.BlockSpec((tm, D), tok),
        out_shape=jax.ShapeDtypeStruct((N_TOK, D), F32),
        compiler_params=_cp(("arbitrary",)),
        name="gdn_post",
    )(o, z, x2, mod_l, nw8, gnw8, w_out.astype(BF16))


def kernel(x, c, ada_w, ada_b, norm_w, conv_in_w, conv_w, conv_out_w, gdn_in_w, gdn_conv_w, gdn_a_log,
           gdn_dt_bias, gdn_norm_w, gdn_out_w, moe_group_w, moe_group_b, moe_expert_w, moe_expert_b,
           moe_w_gate, moe_w_up, moe_w_down, shared_w_gate, shared_w_up, shared_w_down, shared_gate_w):
    mod = _ada_mod(c, ada_w, ada_b)
    nw8 = jnp.pad(norm_w, ((0, 0), (0, SUBLANES - norm_w.shape[1]), (0, 0)))
    x2 = x.reshape(N_TOK, D)

    def moe(x1, layer):
        return _hier_moe_block(
            x1, mod[layer], nw8[layer], layer,
            moe_group_w[layer], moe_group_b[layer], moe_expert_w[layer], moe_expert_b[layer],
            moe_w_gate, moe_w_up, moe_w_down,
            shared_w_gate[layer], shared_w_up[layer], shared_w_down[layer], shared_gate_w[layer])

    x2 = _conv_mixer(x2, mod[0], nw8[0], conv_in_w[0], conv_w[0], conv_out_w[0])
    x2 = moe(x2, 0)
    q, k, v, z, gb, gt = _gdn_pre(x2, mod[1], nw8[1], gdn_in_w[0], gdn_conv_w[0], gdn_a_log[0], gdn_dt_bias[0])
    o = _gdn_chunks(q, k, v, gb, gt)
    x2 = _gdn_post(o, z, x2, mod[1], nw8[1], gdn_norm_w[0], gdn_out_w[0])
    x2 = moe(x2, 1)
    return x2.reshape(BATCH, SEQ, D)
```

```python
import functools

import jax
import jax.numpy as jnp
from jax import lax
from jax.experimental import pallas as pl
from jax.experimental.pallas import tpu as pltpu
from jax.experimental.pallas import tpu_sc as plsc

F32 = jnp.float32
BF16 = jnp.bfloat16
I32 = jnp.int32
HIGHEST = lax.Precision.HIGHEST

D = 1024
D_PACKED = D // 2
BATCH = 4
SEQ = 4096
N_TOK = BATCH * SEQ
HEADS = 8
HEAD_DIM = 128
CHUNK = 64
CHUNK_SHIFT = 6
N_GROUPS = 8
N_EXPERTS = 64
D_EXPERT = 256
D_SHARED = 512
EPS = 1e-6
LANES = 128
SUBLANES = 8
EXPERT_LANE0 = N_GROUPS

MOE_BLOCK = 256
N_SLOTS = N_TOK * 2
N_BLOCKS = N_SLOTS // MOE_BLOCK + N_EXPERTS
P_ROWS = N_BLOCKS * MOE_BLOCK

TM_MIX = 512
TM_GDN = 256
TM_DMA = 256
TM_POS = 2048
GDN_BATCH_PER_STEP = 4
VMEM_LIMIT = 56 * 1024 * 1024


def _cp(sem):
    return pltpu.CompilerParams(dimension_semantics=sem, vmem_limit_bytes=VMEM_LIMIT)


def _rms(x):
    return x * lax.rsqrt(jnp.mean(x * x, axis=-1, keepdims=True) + EPS)


def _silu(x):
    return x * jax.nn.sigmoid(x)


def _dot(a, b):
    return jnp.dot(a, b, preferred_element_type=F32)


U32 = jnp.uint32
HI16 = 0xFFFF0000


def _pack_bf16_pairs(xb):
    half = xb.shape[1] // 2
    bits = lax.bitcast_convert_type(xb.astype(F32), U32)
    return lax.bitcast_convert_type(bits[:, :half] | (bits[:, half:] >> 16), I32)


def _unpack_bf16_pairs(words):
    bits = lax.bitcast_convert_type(words, U32)
    hi = lax.bitcast_convert_type(bits & jnp.uint32(HI16), F32)
    lo = lax.bitcast_convert_type(bits << 16, F32)
    return jnp.concatenate([hi, lo], axis=1)


def _causal_taps(x, tail_ref, taps, first_of_seq):
    @pl.when(first_of_seq)
    def _():
        tail_ref[...] = jnp.zeros(tail_ref.shape, F32)

    tail = tail_ref[...]
    sub = lax.broadcasted_iota(I32, tail.shape, 0)
    out = []
    for s in taps:
        rolled = pltpu.roll(x, s, axis=0)
        head = jnp.where(sub < s, pltpu.roll(tail, s, axis=0), rolled[0:SUBLANES])
        out.append(jnp.concatenate([head, rolled[SUBLANES:]], axis=0))
    tail_ref[...] = x[x.shape[0] - SUBLANES:]
    return out


def _ada_kernel(c_ref, w_ref, b_ref, o_ref):
    cs = _silu(c_ref[...])
    o_ref[0] = _dot(cs.astype(BF16), w_ref[0].astype(BF16)) + b_ref[0]


def _ada_mod(c, ada_w, ada_b):
    depth = ada_w.shape[0]
    tn = 1024
    c8 = jnp.pad(c, ((0, SUBLANES - BATCH), (0, 0)))
    mod = pl.pallas_call(
        _ada_kernel,
        grid=(depth, 6 * D // tn),
        in_specs=[
            pl.BlockSpec((SUBLANES, D), lambda l, j: (0, 0)),
            pl.BlockSpec((1, D, tn), lambda l, j: (l, 0, j)),
            pl.BlockSpec((1, 1, tn), lambda l, j: (l, 0, j)),
        ],
        out_specs=pl.BlockSpec((1, SUBLANES, tn), lambda l, j: (l, 0, j)),
        out_shape=jax.ShapeDtypeStruct((depth, SUBLANES, 6 * D), F32),
        compiler_params=_cp(("arbitrary", "arbitrary")),
        name="ada_mod",
    )(c8, ada_w, ada_b.reshape(depth, 1, 6 * D))
    mod = mod[:, :BATCH].reshape(depth, BATCH, 6, D)
    return jnp.pad(mod, ((0, 0), (0, 0), (0, 2), (0, 0)))


def _conv_mixer_kernel(x_ref, mod_ref, nw_ref, win_ref, cw_ref, wout_ref, o_ref, tail_ref, *, tiles_per_seq):
    x = x_ref[...]
    mod = mod_ref[...]
    nw = nw_ref[...]
    h = _rms(x) * nw[0:1] * (1.0 + mod[1:2]) + mod[0:1]
    bcx = _dot(h.astype(BF16), win_ref[...])
    u = bcx[:, D:2 * D] * bcx[:, 2 * D:]
    u2, u1 = _causal_taps(u, tail_ref, (2, 1), pl.program_id(0) % tiles_per_seq == 0)
    cw = cw_ref[...]
    conv = cw[0:1] * u2 + cw[1:2] * u1 + cw[2:3] * u
    y = _dot((bcx[:, :D] * conv).astype(BF16), wout_ref[...])
    o_ref[...] = x + mod[2:3] * (_rms(y) * nw[1:2])


def _conv_mixer(x2, mod_l, nw8, w_in, conv_w, w_out):
    tm = TM_MIX
    tiles_per_seq = SEQ // tm
    cw8 = jnp.pad(conv_w, ((0, SUBLANES - conv_w.shape[0]), (0, 0)))
    return pl.pallas_call(
        functools.partial(_conv_mixer_kernel, tiles_per_seq=tiles_per_seq),
        grid=(N_TOK // tm,),
        in_specs=[
            pl.BlockSpec((tm, D), lambda i: (i, 0)),
            pl.BlockSpec((None, SUBLANES, D), lambda i: (i // tiles_per_seq, 0, 0)),
            pl.BlockSpec((SUBLANES, D), lambda i: (0, 0)),
            pl.BlockSpec((D, 3 * D), lambda i: (0, 0)),
            pl.BlockSpec((SUBLANES, D), lambda i: (0, 0)),
            pl.BlockSpec((D, D), lambda i: (0, 0)),
        ],
        out_specs=pl.BlockSpec((tm, D), lambda i: (i, 0)),
        out_shape=jax.ShapeDtypeStruct((N_TOK, D), F32),
        scratch_shapes=[pltpu.VMEM((SUBLANES, D), F32)],
        compiler_params=_cp(("arbitrary",)),
        name="conv_mixer",
    )(x2, mod_l, nw8, w_in.astype(BF16), cw8, w_out.astype(BF16))


def _lane_pick(lane, mask_val_pairs):
    out = jnp.zeros(lane.shape, F32)
    for idx, val in mask_val_pairs:
        out = jnp.where(lane == float(idx), val, out)
    return out


def _moe_pre_kernel(x_ref, mod_ref, nw_ref, wr_ref, br_ref, wgu_ref, wd_ref, wsg_ref,
                    h_ref, sh_ref, route_ref, cnt_ref, carry_ref):
    tm = x_ref.shape[0]
    x = x_ref[...]
    mod = mod_ref[...]
    nw = nw_ref[...]
    h = _rms(x) * nw[2:3] * (1.0 + mod[4:5]) + mod[3:4]
    hb = h.astype(BF16)
    h_ref[...] = _pack_bf16_pairs(hb)

    h_lo = (h - hb.astype(F32)).astype(BF16)
    hw = _dot(hb, wr_ref[...]) + _dot(h_lo, wr_ref[...])
    logits = hw[:, :LANES] + hw[:, LANES:] + br_ref[0:1]
    lane_i = lax.broadcasted_iota(I32, (tm, LANES), 1)
    lane = lane_i.astype(F32)
    neg = jnp.float32(-jnp.inf)
    is_group = lane_i < N_GROUPS
    gl = jnp.where(is_group, logits, neg)
    gmax = jnp.max(gl, axis=-1, keepdims=True)
    gsel = jnp.min(jnp.where(gl == gmax, lane, float(LANES)), axis=-1, keepdims=True)
    psel = 1.0 / jnp.sum(jnp.where(is_group, jnp.exp(logits - gmax), 0.0), axis=-1, keepdims=True)
    lane_group = ((lane_i - EXPERT_LANE0) >> 3).astype(F32)
    in_group = (lane_i >= EXPERT_LANE0) & (lane_i < EXPERT_LANE0 + N_EXPERTS) & (lane_group == gsel)
    el = jnp.where(in_group, logits, neg)
    v1 = jnp.max(el, axis=-1, keepdims=True)
    i1 = jnp.min(jnp.where(el == v1, lane, float(LANES)), axis=-1, keepdims=True)
    el2 = jnp.where(lane == i1, neg, el)
    v2 = jnp.max(el2, axis=-1, keepdims=True)
    i2 = jnp.min(jnp.where(el2 == v2, lane, float(LANES)), axis=-1, keepdims=True)
    e2 = jnp.exp(v2 - v1)
    g1 = psel / (1.0 + e2)
    g2 = psel * e2 / (1.0 + e2)

    @pl.when(pl.program_id(0) == 0)
    def _():
        carry_ref[...] = jnp.zeros(carry_ref.shape, F32)

    oh1 = lane == i1
    oh2 = lane == i2
    cnt = jnp.where(oh1 | oh2, 1.0, 0.0).astype(BF16)
    row = lax.broadcasted_iota(I32, (tm, tm), 0)
    col = lax.broadcasted_iota(I32, (tm, tm), 1)
    tri = jnp.where(col < row, 1.0, 0.0).astype(BF16)
    pre = _dot(tri, cnt) + carry_ref[0:1]
    r1 = jnp.sum(jnp.where(oh1, pre, 0.0), axis=-1, keepdims=True)
    r2 = jnp.sum(jnp.where(oh2, pre, 0.0), axis=-1, keepdims=True)
    carry_ref[...] = carry_ref[...] + jnp.sum(cnt.astype(F32), axis=0, keepdims=True)
    cnt_ref[...] = carry_ref[...]
    route_ref[...] = _lane_pick(lane, [
        (0, i1 - EXPERT_LANE0), (1, i2 - EXPERT_LANE0),
        (2, g1), (3, g2), (4, r1), (5, r2)])

    gu = _dot(hb, wgu_ref[...])
    hid = _silu(gu[:, :D_SHARED]) * gu[:, D_SHARED:]
    ys = _dot(hid.astype(BF16), wd_ref[...])
    sg = jax.nn.sigmoid(jnp.sum(h * wsg_ref[0:1], axis=-1, keepdims=True))
    sh_ref[...] = sg * ys


def _moe_pre(x1, mod_l, nw8, w_rg, b_rg, w_re, b_re, ws_gate, ws_up, ws_down, w_sg):
    tm = TM_MIX
    tiles_per_seq = SEQ // tm
    pad_l = LANES - N_GROUPS - N_EXPERTS
    wr = jnp.pad(jnp.concatenate([w_rg, w_re], axis=1), ((0, 0), (0, pad_l)))
    wr_hi = wr.astype(BF16)
    wr = jnp.concatenate([wr_hi, (wr - wr_hi.astype(F32)).astype(BF16)], axis=1)
    br = jnp.pad(jnp.concatenate([b_rg, b_re])[None, :], ((0, SUBLANES - 1), (0, pad_l)))
    wgu = jnp.concatenate([ws_gate, ws_up], axis=1).astype(BF16)
    wsg8 = jnp.pad(w_sg.reshape(1, D), ((0, SUBLANES - 1), (0, 0)))
    const = lambda i: (0, 0)
    return pl.pallas_call(
        _moe_pre_kernel,
        grid=(N_TOK // tm,),
        in_specs=[
            pl.BlockSpec((tm, D), lambda i: (i, 0)),
            pl.BlockSpec((None, SUBLANES, D), lambda i: (i // tiles_per_seq, 0, 0)),
            pl.BlockSpec((SUBLANES, D), const),
            pl.BlockSpec((D, 2 * LANES), const),
            pl.BlockSpec((SUBLANES, LANES), const),
            pl.BlockSpec((D, 2 * D_SHARED), const),
            pl.BlockSpec((D_SHARED, D), const),
            pl.BlockSpec((SUBLANES, D), const),
        ],
        out_specs=[
            pl.BlockSpec((tm, D_PACKED), lambda i: (i, 0)),
            pl.BlockSpec((tm, D), lambda i: (i, 0)),
            pl.BlockSpec((tm, LANES), lambda i: (i, 0)),
            pl.BlockSpec((SUBLANES, LANES), const),
        ],
        out_shape=[
            jax.ShapeDtypeStruct((N_TOK, D_PACKED), I32),
            jax.ShapeDtypeStruct((N_TOK, D), F32),
            jax.ShapeDtypeStruct((N_TOK, LANES), F32),
            jax.ShapeDtypeStruct((SUBLANES, LANES), F32),
        ],
        scratch_shapes=[pltpu.VMEM((SUBLANES, LANES), F32)],
        compiler_params=_cp(("arbitrary",)),
        name="moe_pre",
    )(x1, mod_l, nw8, wr, br, wgu, ws_down.astype(BF16), wsg8)


def _n_blocks_per_expert(cnt_row):
    return jnp.floor((cnt_row + (MOE_BLOCK - 1)) * (1.0 / MOE_BLOCK))


def _moe_pos_kernel(cnt_ref, route_ref, pos_ref, seg_ref):
    tm = route_ref.shape[0]
    nb = _n_blocks_per_expert(cnt_ref[...])
    r = lax.broadcasted_iota(I32, (LANES, LANES), 0)
    c = lax.broadcasted_iota(I32, (LANES, LANES), 1)
    excl = jnp.dot(nb, jnp.where(r < c, 1.0, 0.0), precision=HIGHEST, preferred_element_type=F32)
    pstart = excl[0:1] * MOE_BLOCK
    route = route_ref[...]
    lane = lax.broadcasted_iota(I32, (tm, LANES), 1).astype(F32)
    e1 = route[:, 0:1] + EXPERT_LANE0
    e2 = route[:, 1:2] + EXPERT_LANE0
    p1 = jnp.sum(jnp.where(lane == e1, pstart, 0.0), axis=-1, keepdims=True) + route[:, 4:5]
    p2 = jnp.sum(jnp.where(lane == e2, pstart, 0.0), axis=-1, keepdims=True) + route[:, 5:6]
    pos_ref[...] = _lane_pick(lane, [(0, p1), (1, p2)]).astype(I32)

    @pl.when(pl.program_id(0) == 0)
    def _():
        sub = lax.broadcasted_iota(I32, (SUBLANES, LANES), 0)
        seg_ref[...] = jnp.where(sub == 0, nb, excl).astype(I32)


def _moe_pos(cnt, route):
    tm = TM_POS
    pos, seg = pl.pallas_call(
        _moe_pos_kernel,
        grid=(N_TOK // tm,),
        in_specs=[
            pl.BlockSpec((SUBLANES, LANES), lambda i: (0, 0)),
            pl.BlockSpec((tm, LANES), lambda i: (i, 0)),
        ],
        out_specs=[
            pl.BlockSpec((tm, LANES), lambda i: (i, 0)),
            pl.BlockSpec((SUBLANES, LANES), lambda i: (0, 0)),
        ],
        out_shape=[
            jax.ShapeDtypeStruct((N_TOK, LANES), I32),
            jax.ShapeDtypeStruct((SUBLANES, LANES), I32),
        ],
        compiler_params=_cp(("arbitrary",)),
        name="moe_pos",
    )(cnt, route)
    experts = slice(EXPERT_LANE0, EXPERT_LANE0 + N_EXPERTS)
    return pos[:, :2].T.reshape(-1), seg[0, experts], seg[1, experts]


def _expert_kernel(nblk_ref, blk0_ref, buf_ref, wg_ref, wu_ref, wd_ref, eo_ref,
                   xin_ref, out_ref, wgb_ref, wub_ref, wdb_ref, sem_in, sem_out):
    e = pl.program_id(0)
    last = pl.num_programs(0) - 1
    n = nblk_ref[e]
    blk0 = blk0_ref[e]
    total = blk0_ref[last] + nblk_ref[last]
    bm = MOE_BLOCK
    n_in = xin_ref.shape[0]

    def rows(g):
        return pl.ds(pl.multiple_of(g * bm, bm), bm)

    def in_copy(g):
        slot = lax.rem(g, n_in)
        return pltpu.make_async_copy(buf_ref.at[rows(g)], xin_ref.at[slot], sem_in.at[slot])

    def out_copy(g):
        return pltpu.make_async_copy(out_ref.at[g & 1], eo_ref.at[rows(g)], sem_out.at[g & 1])

    @pl.when(n > 0)
    def _():
        wgb_ref[...] = wg_ref[...].astype(BF16)
        wub_ref[...] = wu_ref[...].astype(BF16)
        wdb_ref[...] = wd_ref[...].astype(BF16)

    def block(j, carry):
        g = blk0 + j

        @pl.when(g == 0)
        def _():
            in_copy(g).start()

            @pl.when(total > 1)
            def _():
                in_copy(g + 1).start()

        @pl.when(g + 2 < total)
        def _():
            in_copy(g + 2).start()

        in_copy(g).wait()

        @pl.when(g >= 2)
        def _():
            out_copy(g - 2).wait()

        xb = _unpack_bf16_pairs(xin_ref[lax.rem(g, n_in)]).astype(BF16)
        hid = _silu(_dot(xb, wgb_ref[...])) * _dot(xb, wub_ref[...])
        out_ref[g & 1] = _pack_bf16_pairs(_dot(hid.astype(BF16), wdb_ref[...]).astype(BF16))
        out_copy(g).start()
        return carry

    lax.fori_loop(0, n, block, 0)

    @pl.when((e == last) & (total >= 2))
    def _():
        out_copy(total - 2).wait()

    @pl.when((e == last) & (total >= 1))
    def _():
        out_copy(total - 1).wait()


def _moe_experts(nblk_e, blk0_e, buf, w_gate, w_up, w_down, layer):
    bm = MOE_BLOCK
    wspec = lambda shape: pl.BlockSpec((None, None) + shape, lambda e, nb, b0: (layer, e, 0, 0))
    return pl.pallas_call(
        _expert_kernel,
        grid_spec=pltpu.PrefetchScalarGridSpec(
            num_scalar_prefetch=2,
            grid=(N_EXPERTS,),
            in_specs=[
                pl.BlockSpec(memory_space=pl.ANY),
                wspec((D, D_EXPERT)),
                wspec((D, D_EXPERT)),
                wspec((D_EXPERT, D)),
            ],
            out_specs=pl.BlockSpec(memory_space=pl.ANY),
            scratch_shapes=[
                pltpu.VMEM((3, bm, D_PACKED), I32),
                pltpu.VMEM((2, bm, D_PACKED), I32),
                pltpu.VMEM((D, D_EXPERT), BF16),
                pltpu.VMEM((D, D_EXPERT), BF16),
                pltpu.VMEM((D_EXPERT, D), BF16),
                pltpu.SemaphoreType.DMA((3,)),
                pltpu.SemaphoreType.DMA((2,)),
            ],
        ),
        out_shape=jax.ShapeDtypeStruct((P_ROWS, D_PACKED), I32),
        compiler_params=_cp(("arbitrary",)),
        name="moe_experts",
    )(nblk_e, blk0_e, buf, w_gate, w_up, w_down)


SC_CORES = 2
SC_SUBCORES = 16
SC_WORKERS = SC_CORES * SC_SUBCORES
SC_CHUNK = 64


def _sc_gather_store(table_hbm, idx_v, out_hbm, out_base, n_rows, bufs, sems):
    n_chunks = n_rows // SC_CHUNK
    assert n_chunks % 2 == 0

    def gather(j, b):
        off = pl.multiple_of(j * SC_CHUNK, SC_CHUNK)
        return pltpu.make_async_copy(table_hbm.at[idx_v.at[pl.ds(off, SC_CHUNK)]], bufs[b], sems[b])

    gather(0, 0).start()

    @pl.loop(0, n_chunks, step=2)
    def _(j):
        for b in range(2):
            jj = j + b

            @pl.when(jj + 1 < n_chunks)
            def _():
                gather(jj + 1, 1 - b).start()

            gather(jj, b).wait()
            row0 = pl.multiple_of(out_base + jj * SC_CHUNK, SC_CHUNK)
            pltpu.sync_copy(bufs[b], out_hbm.at[pl.ds(row0, SC_CHUNK)])


def _sc_row_buffers(width, dtype):
    return [pltpu.VMEM((SC_CHUNK, width), dtype), pltpu.VMEM((SC_CHUNK, width), dtype),
            pltpu.SemaphoreType.DMA, pltpu.SemaphoreType.DMA]


def _sc_gather_rows(table, idx):
    n_idx = idx.shape[0]
    width = table.shape[1]
    per_w = n_idx // SC_WORKERS
    mesh = plsc.VectorSubcoreMesh(core_axis_name="c", subcore_axis_name="s")

    def body(table_hbm, idx_hbm, out_hbm, idx_v, buf0, buf1, sem0, sem1):
        wid = lax.axis_index("s") * SC_CORES + lax.axis_index("c")
        base = wid * per_w
        pltpu.sync_copy(idx_hbm.at[pl.ds(base, per_w)], idx_v)
        _sc_gather_store(table_hbm, idx_v, out_hbm, base, per_w, (buf0, buf1), (sem0, sem1))

    return pl.kernel(
        body,
        out_type=jax.ShapeDtypeStruct((n_idx, width), table.dtype),
        mesh=mesh,
        scratch_types=[pltpu.VMEM((per_w,), I32)] + _sc_row_buffers(width, table.dtype),
        name="sc_gather_rows",
    )(table, idx)


def _sc_dispatch_rows(h, pos_km):
    per_w = P_ROWS // SC_WORKERS
    lanes = 16
    mesh = plsc.VectorSubcoreMesh(core_axis_name="c", subcore_axis_name="s")

    def body(h_hbm, pos_hbm, buf_hbm, pos_v, tok_v, buf0, buf1, sem0, sem1):
        wid = lax.axis_index("s") * SC_CORES + lax.axis_index("c")
        base = wid * per_w
        pltpu.sync_copy(pos_hbm, pos_v)

        @pl.loop(0, per_w // lanes)
        def _(i):
            off = pl.multiple_of(i * lanes, lanes)
            tok_v[pl.ds(off, lanes)] = (base + off + lax.iota(I32, lanes)) & (N_TOK - 1)

        @pl.loop(0, N_SLOTS // lanes)
        def _(i):
            off = pl.multiple_of(i * lanes, lanes)
            local = pos_v[pl.ds(off, lanes)] - base
            mine = (local >= 0) & (local < per_w)
            slot = off + lax.iota(I32, lanes)
            plsc.store_scatter(tok_v, [jnp.where(mine, local, 0)], slot & (N_TOK - 1), mask=mine)

        _sc_gather_store(h_hbm, tok_v, buf_hbm, base, per_w, (buf0, buf1), (sem0, sem1))

    return pl.kernel(
        body,
        out_type=jax.ShapeDtypeStruct((P_ROWS, h.shape[1]), h.dtype),
        mesh=mesh,
        scratch_types=[pltpu.VMEM((N_SLOTS,), I32), pltpu.VMEM((per_w,), I32)]
        + _sc_row_buffers(h.shape[1], h.dtype),
        compiler_params=pltpu.CompilerParams(needs_layout_passes=False),
        name="sc_dispatch_rows",
    )(h, pos_km)


def _combine_kernel(y0_ref, y1_ref, x_ref, sh_ref, route_ref, mod_ref, nw_ref, o_ref):
    route = route_ref[...]
    y0 = _unpack_bf16_pairs(y0_ref[...])
    y1 = _unpack_bf16_pairs(y1_ref[...])
    y = route[:, 2:3] * y0 + route[:, 3:4] * y1 + sh_ref[...]
    mod = mod_ref[...]
    o_ref[...] = x_ref[...] + mod[5:6] * (_rms(y) * nw_ref[3:4])


def _moe_combine(pos_km, eo, x1, sh, route, mod_l, nw8):
    tm = TM_MIX
    tiles_per_seq = SEQ // tm
    n_tiles = N_TOK // tm
    y2 = _sc_gather_rows(eo, pos_km)
    return pl.pallas_call(
        _combine_kernel,
        grid=(n_tiles,),
        in_specs=[
            pl.BlockSpec((tm, D_PACKED), lambda i: (i, 0)),
            pl.BlockSpec((tm, D_PACKED), lambda i: (i + n_tiles, 0)),
            pl.BlockSpec((tm, D), lambda i: (i, 0)),
            pl.BlockSpec((tm, D), lambda i: (i, 0)),
            pl.BlockSpec((tm, LANES), lambda i: (i, 0)),
            pl.BlockSpec((None, SUBLANES, D), lambda i: (i // tiles_per_seq, 0, 0)),
            pl.BlockSpec((SUBLANES, D), lambda i: (0, 0)),
        ],
        out_specs=pl.BlockSpec((tm, D), lambda i: (i, 0)),
        out_shape=jax.ShapeDtypeStruct((N_TOK, D), F32),
        compiler_params=_cp(("arbitrary",)),
        name="moe_combine",
    )(y2, y2, x1, sh, route, mod_l, nw8)


def _hier_moe_block(x1, mod_l, nw8, layer, w_rg, b_rg, w_re, b_re, w_gate, w_up, w_down,
                    ws_gate, ws_up, ws_down, w_sg):
    h, sh, route, cnt = _moe_pre(x1, mod_l, nw8, w_rg, b_rg, w_re, b_re, ws_gate, ws_up, ws_down, w_sg)
    pos_km, nblk_e, blk0_e = _moe_pos(cnt, route)
    buf = _sc_dispatch_rows(h, pos_km)
    eo = _moe_experts(nblk_e, blk0_e, buf, w_gate, w_up, w_down, layer)
    return _moe_combine(pos_km, eo, x1, sh, route, mod_l, nw8)


def _gdn_pre_kernel(x_ref, mod_ref, nw_ref, w_ref, wba_ref, cw_ref, misc_ref,
                    q_ref, k_ref, v_ref, z_ref, gb_ref, gt_ref, tail_ref, *, tiles_per_seq):
    tm = x_ref.shape[0]
    qkv_w = 3 * D
    x = x_ref[...]
    mod = mod_ref[...]
    nw = nw_ref[...]
    h = _rms(x) * nw[0:1] * (1.0 + mod[1:2]) + mod[0:1]
    hb = h.astype(BF16)
    proj = _dot(hb, w_ref[...])
    z_ref[...] = proj[:, qkv_w:].astype(BF16)
    pre = proj[:, :qkv_w]
    p3, p2, p1 = _causal_taps(pre, tail_ref, (3, 2, 1), pl.program_id(0) % tiles_per_seq == 0)
    cw = cw_ref[...]
    conv = cw[0:1] * p3 + cw[1:2] * p2 + cw[2:3] * p1 + cw[3:4] * pre
    act = _silu(conv)
    for hd in range(HEADS):
        lo = hd * HEAD_DIM
        qh = act[:, lo:lo + HEAD_DIM]
        kh = act[:, D + lo:D + lo + HEAD_DIM]
        qn = qh * lax.rsqrt(jnp.sum(qh * qh, axis=-1, keepdims=True) + EPS) * (HEAD_DIM ** -0.5)
        kn = kh * lax.rsqrt(jnp.sum(kh * kh, axis=-1, keepdims=True) + EPS)
        q_ref[:, lo:lo + HEAD_DIM] = qn.astype(BF16)
        k_ref[:, lo:lo + HEAD_DIM] = kn.astype(BF16)
    v_ref[...] = act[:, 2 * D:].astype(BF16)

    ba = _dot(hb, wba_ref[...])
    misc = misc_ref[...]
    beta = jax.nn.sigmoid(ba)
    sp_in = ba + misc[1:2]
    softplus = jnp.maximum(sp_in, 0.0) + jnp.log(1.0 + jnp.exp(-jnp.abs(sp_in)))
    g = -jnp.exp(misc[0:1]) * softplus
    row = lax.broadcasted_iota(I32, (tm, tm), 0)
    col = lax.broadcasted_iota(I32, (tm, tm), 1)
    tri = jnp.where((col <= row) & ((col >> CHUNK_SHIFT) == (row >> CHUNK_SHIFT)), 1.0, 0.0)
    gc = jnp.dot(tri, g, precision=HIGHEST, preferred_element_type=F32)
    lane = lax.broadcasted_iota(I32, (tm, LANES), 1)
    gb = jnp.where(lane < HEADS, beta, gc)
    gb_ref[...] = gb
    for c in range(tm // CHUNK):
        blk = jnp.concatenate([gb[c * CHUNK:(c + 1) * CHUNK], jnp.zeros((LANES - CHUNK, LANES), F32)], axis=0)
        gt_ref[c] = blk.T[HEADS:2 * HEADS, :]


def _gdn_pre(x2, mod_l, nw8, w_in, conv_w, a_log, dt_bias):
    tm = TM_GDN
    tiles_per_seq = SEQ // tm
    qkvz = 4 * D
    w_main = w_in[:, :qkvz].astype(BF16)
    wba = jnp.pad(w_in[:, qkvz:], ((0, 0), (0, LANES - 2 * HEADS))).astype(BF16)
    cw8 = jnp.pad(conv_w, ((0, SUBLANES - conv_w.shape[0]), (0, 0)))
    misc = jnp.zeros((SUBLANES, LANES), F32)
    misc = misc.at[0, HEADS:2 * HEADS].set(a_log).at[1, HEADS:2 * HEADS].set(dt_bias)
    const = lambda i: (0, 0)
    tok = lambda i: (i, 0)
    return pl.pallas_call(
        functools.partial(_gdn_pre_kernel, tiles_per_seq=tiles_per_seq),
        grid=(N_TOK // tm,),
        in_specs=[
            pl.BlockSpec((tm, D), tok),
            pl.BlockSpec((None, SUBLANES, D), lambda i: (i // tiles_per_seq, 0, 0)),
            pl.BlockSpec((SUBLANES, D), const),
            pl.BlockSpec((D, qkvz), const),
            pl.BlockSpec((D, LANES), const),
            pl.BlockSpec((SUBLANES, 3 * D), const),
            pl.BlockSpec((SUBLANES, LANES), const),
        ],
        out_specs=[
            pl.BlockSpec((tm, D), tok),
            pl.BlockSpec((tm, D), tok),
            pl.BlockSpec((tm, D), tok),
            pl.BlockSpec((tm, D), tok),
            pl.BlockSpec((tm, LANES), tok),
            pl.BlockSpec((tm // CHUNK, HEADS, LANES), lambda i: (i, 0, 0)),
        ],
        out_shape=[
            jax.ShapeDtypeStruct((N_TOK, D), BF16),
            jax.ShapeDtypeStruct((N_TOK, D), BF16),
            jax.ShapeDtypeStruct((N_TOK, D), BF16),
            jax.ShapeDtypeStruct((N_TOK, D), BF16),
            jax.ShapeDtypeStruct((N_TOK, LANES), F32),
            jax.ShapeDtypeStruct((N_TOK // CHUNK, HEADS, LANES), F32),
        ],
        scratch_shapes=[pltpu.VMEM((SUBLANES, 3 * D), F32)],
        compiler_params=_cp(("arbitrary",)),
        name="gdn_pre",
    )(x2, mod_l, nw8, w_main, wba, cw8, misc)


def _dot_nt(a, b):
    return lax.dot_general(a, b, (((1,), (1,)), ((), ())), preferred_element_type=F32)


def _dot_tn(a, b):
    return lax.dot_general(a, b, (((0,), (0,)), ((), ())), preferred_element_type=F32)


def _gdn_chunk_kernel(q_ref, k_ref, v_ref, gb_ref, gt_ref, o_ref, s_ref):
    @pl.when(pl.program_id(1) == 0)
    def _():
        s_ref[...] = jnp.zeros(s_ref.shape, F32)

    c = CHUNK
    nb = q_ref.shape[0]
    row = lax.broadcasted_iota(I32, (c, c), 0)
    col = lax.broadcasted_iota(I32, (c, c), 1)
    causal = col <= row
    strict = col < row
    eye = jnp.where(col == row, 1.0, 0.0)
    chains = [(b, hd) for b in range(nb) for hd in range(HEADS)]
    st = []
    for b, hd in chains:
        lo = hd * HEAD_DIM
        gb = gb_ref[b]
        q = q_ref[b, :, lo:lo + HEAD_DIM].astype(F32)
        k = k_ref[b, :, lo:lo + HEAD_DIM].astype(F32)
        v = v_ref[b, :, lo:lo + HEAD_DIM].astype(F32)
        beta = gb[:, hd:hd + 1]
        gcol = gb[:, HEADS + hd:HEADS + hd + 1]
        grow = gt_ref[b, hd:hd + 1, 0:c]
        glast = gcol[c - 1:c, :]
        egc = jnp.exp(gcol)
        kb = k * beta
        st.append(dict(
            decay=jnp.exp(jnp.where(causal, gcol - grow, -jnp.inf)),
            kq=jnp.concatenate([kb, q], axis=0).astype(BF16),
            kbf=k.astype(BF16),
            rhs=jnp.concatenate([v * beta, kb * egc], axis=-1).astype(BF16),
            qd=(q * egc).astype(BF16),
            kd=(k * jnp.exp(glast - gcol)).astype(BF16),
            eg=jnp.exp(glast)))
    for x in st:
        kk = _dot_nt(x["kq"], x["kbf"])
        x["p"] = jnp.where(strict, kk[:c] * x["decay"], 0.0)
        x["attn"] = jnp.where(causal, kk[c:] * x["decay"], 0.0).astype(BF16)
        x["t"] = eye - x["p"]
    for _ in range(5):
        for x in st:
            pb = x["p"].astype(BF16)
            x["p"] = _dot(pb, pb)
        for x in st:
            x["t"] = x["t"] + _dot(x["t"].astype(BF16), x["p"].astype(BF16))
    for x in st:
        x["uw"] = _dot(x["t"].astype(BF16), x["rhs"])
    for x, (b, hd) in zip(st, chains):
        s = s_ref[b * HEADS + hd]
        x["s"] = s
        ws = _dot(jnp.concatenate([x["uw"][:, HEAD_DIM:].astype(BF16), x["qd"]], axis=0), s.astype(BF16))
        x["vb"] = (x["uw"][:, :HEAD_DIM] - ws[:c]).astype(BF16)
        x["o"] = ws[c:]
    for x, (b, hd) in zip(st, chains):
        lo = hd * HEAD_DIM
        s_ref[b * HEADS + hd] = x["s"] * x["eg"] + _dot_tn(x["kd"], x["vb"])
        o_ref[b, :, lo:lo + HEAD_DIM] = (x["o"] + _dot(x["attn"], x["vb"])).astype(BF16)


def _gdn_chunks(q, k, v, gb, gt):
    nb = GDN_BATCH_PER_STEP
    n_chunks = SEQ // CHUNK
    tok = lambda b, c: (b, c, 0)
    o = pl.pallas_call(
        _gdn_chunk_kernel,
        grid=(BATCH // nb, n_chunks),
        in_specs=[
            pl.BlockSpec((nb, CHUNK, D), tok),
            pl.BlockSpec((nb, CHUNK, D), tok),
            pl.BlockSpec((nb, CHUNK, D), tok),
            pl.BlockSpec((nb, CHUNK, LANES), tok),
            pl.BlockSpec((nb, None, HEADS, LANES), lambda b, c: (b, c, 0, 0)),
        ],
        out_specs=pl.BlockSpec((nb, CHUNK, D), tok),
        out_shape=jax.ShapeDtypeStruct((BATCH, SEQ, D), BF16),
        scratch_shapes=[pltpu.VMEM((nb * HEADS, HEAD_DIM, HEAD_DIM), F32)],
        compiler_params=_cp(("arbitrary", "arbitrary")),
        name="gdn_chunks",
    )(q.reshape(BATCH, SEQ, D), k.reshape(BATCH, SEQ, D), v.reshape(BATCH, SEQ, D),
      gb.reshape(BATCH, SEQ, LANES), gt.reshape(BATCH, n_chunks, HEADS, LANES))
    return o.reshape(N_TOK, D)


def _gdn_post_kernel(o_ref, z_ref, x_ref, mod_ref, nw_ref, gnw_ref, wout_ref, out_ref):
    gnw = gnw_ref[0:1]
    parts = []
    for hd in range(HEADS):
        lo = hd * HEAD_DIM
        oh = o_ref[:, lo:lo + HEAD_DIM].astype(F32)
        zh = z_ref[:, lo:lo + HEAD_DIM].astype(F32)
        parts.append((_rms(oh) * gnw * _silu(zh)).astype(BF16))
    y = _dot(jnp.concatenate(parts, axis=-1), wout_ref[...])
    mod = mod_ref[...]
    out_ref[...] = x_ref[...] + mod[2:3] * (_rms(y) * nw_ref[1:2])


def _gdn_post(o, z, x2, mod_l, nw8, gdn_norm_w, w_out):
    tm = TM_MIX
    tiles_per_seq = SEQ // tm
    gnw8 = jnp.pad(gdn_norm_w.reshape(1, HEAD_DIM), ((0, SUBLANES - 1), (0, 0)))
    tok = lambda i: (i, 0)
    const = lambda i: (0, 0)
    return pl.pallas_call(
        _gdn_post_kernel,
        grid=(N_TOK // tm,),
        in_specs=[
            pl.BlockSpec((tm, D), tok),
            pl.BlockSpec((tm, D), tok),
            pl.BlockSpec((tm, D), tok),
            pl.BlockSpec((None, SUBLANES, D), lambda i: (i // tiles_per_seq, 0, 0)),
            pl.BlockSpec((SUBLANES, D), const),
            pl.BlockSpec((SUBLANES, HEAD_DIM), const),
            pl.BlockSpec((D, D), const),
        ],
        out_specs=pl.BlockSpec((tm, D), tok),
        out_shape=jax.ShapeDtypeStruct((N_TOK, D), F32),
        compiler_params=_cp(("arbitrary",)),
        name="gdn_post",
    )(o, z, x2, mod_l, nw8, gnw8, w_out.astype(BF16))


def kernel(x, c, ada_w, ada_b, norm_w, conv_in_w, conv_w, conv_out_w, gdn_in_w, gdn_conv_w, gdn_a_log,
           gdn_dt_bias, gdn_norm_w, gdn_out_w, moe_group_w, moe_group_b, moe_expert_w, moe_expert_b,
           moe_w_gate, moe_w_up, moe_w_down, shared_w_gate, shared_w_up, shared_w_down, shared_gate_w):
    mod = _ada_mod(c, ada_w, ada_b)
    nw8 = jnp.pad(norm_w, ((0, 0), (0, SUBLANES - norm_w.shape[1]), (0, 0)))
    x2 = x.reshape(N_TOK, D)

    def moe(x1, layer):
        return _hier_moe_block(
            x1, mod[layer], nw8[layer], layer,
            moe_group_w[layer], moe_group_b[layer], moe_expert_w[layer], moe_expert_b[layer],
            moe_w_gate, moe_w_up, moe_w_down,
            shared_w_gate[layer], shared_w_up[layer], shared_w_down[layer], shared_gate_w[layer])

    x2 = _conv_mixer(x2, mod[0], nw8[0], conv_in_w[0], conv_w[0], conv_out_w[0])
    x2 = moe(x2, 0)
    q, k, v, z, gb, gt = _gdn_pre(x2, mod[1], nw8[1], gdn_in_w[0], gdn_conv_w[0], gdn_a_log[0], gdn_dt_bias[0])
    o = _gdn_chunks(q, k, v, gb, gt)
    x2 = _gdn_post(o, z, x2, mod[1], nw8[1], gdn_norm_w[0], gdn_out_w[0])
    x2 = moe(x2, 1)
    return x2.reshape(BATCH, SEQ, D)
```

```python
import functools

import jax
import jax.numpy as jnp
from jax import lax
from jax.experimental import pallas as pl
from jax.experimental.pallas import tpu as pltpu
from jax.experimental.pallas import tpu_sc as plsc

F32 = jnp.float32
BF16 = jnp.bfloat16
I32 = jnp.int32
HIGHEST = lax.Precision.HIGHEST

D = 1024
D_PACKED = D // 2
BATCH = 4
SEQ = 4096
N_TOK = BATCH * SEQ
HEADS = 8
HEAD_DIM = 128
CHUNK = 64
CHUNK_SHIFT = 6
N_GROUPS = 8
N_EXPERTS = 64
D_EXPERT = 256
D_SHARED = 512
EPS = 1e-6
LANES = 128
SUBLANES = 8
EXPERT_LANE0 = N_GROUPS

MOE_BLOCK = 256
N_SLOTS = N_TOK * 2
N_BLOCKS = N_SLOTS // MOE_BLOCK + N_EXPERTS
P_ROWS = N_BLOCKS * MOE_BLOCK

TM_MIX = 512
TM_GDN = 256
TM_DMA = 256
TM_POS = 2048
GDN_BATCH_PER_STEP = 4
VMEM_LIMIT = 56 * 1024 * 1024


def _cp(sem):
    return pltpu.CompilerParams(dimension_semantics=sem, vmem_limit_bytes=VMEM_LIMIT)


def _rms(x):
    return x * lax.rsqrt(jnp.mean(x * x, axis=-1, keepdims=True) + EPS)


def _silu(x):
    return x * jax.nn.sigmoid(x)


def _dot(a, b):
    return jnp.dot(a, b, preferred_element_type=F32)


U32 = jnp.uint32
HI16 = 0xFFFF0000


def _pack_bf16_pairs(xb):
    half = xb.shape[1] // 2
    bits = lax.bitcast_convert_type(xb.astype(F32), U32)
    return lax.bitcast_convert_type(bits[:, :half] | (bits[:, half:] >> 16), I32)


def _unpack_bf16_pairs(words):
    bits = lax.bitcast_convert_type(words, U32)
    hi = lax.bitcast_convert_type(bits & jnp.uint32(HI16), F32)
    lo = lax.bitcast_convert_type(bits << 16, F32)
    return jnp.concatenate([hi, lo], axis=1)


def _causal_taps(x, tail_ref, taps, first_of_seq):
    tail = jnp.where(first_of_seq, 0.0, tail_ref[...])
    sub = lax.broadcasted_iota(I32, tail.shape, 0)
    out = []
    for s in taps:
        rolled = pltpu.roll(x, s, axis=0)
        head = jnp.where(sub < s, pltpu.roll(tail, s, axis=0), rolled[0:SUBLANES])
        out.append(jnp.concatenate([head, rolled[SUBLANES:]], axis=0))
    tail_ref[...] = x[x.shape[0] - SUBLANES:]
    return out


def _ada_kernel(c_ref, w_ref, b_ref, o_ref):
    cs = _silu(c_ref[...])
    o_ref[0] = _dot(cs.astype(BF16), w_ref[0].astype(BF16)) + b_ref[0]


def _ada_mod(c, ada_w, ada_b):
    depth = ada_w.shape[0]
    tn = 1024
    c8 = jnp.pad(c, ((0, SUBLANES - BATCH), (0, 0)))
    mod = pl.pallas_call(
        _ada_kernel,
        grid=(depth, 6 * D // tn),
        in_specs=[
            pl.BlockSpec((SUBLANES, D), lambda l, j: (0, 0)),
            pl.BlockSpec((1, D, tn), lambda l, j: (l, 0, j)),
            pl.BlockSpec((1, 1, tn), lambda l, j: (l, 0, j)),
        ],
        out_specs=pl.BlockSpec((1, SUBLANES, tn), lambda l, j: (l, 0, j)),
        out_shape=jax.ShapeDtypeStruct((depth, SUBLANES, 6 * D), F32),
        compiler_params=_cp(("arbitrary", "arbitrary")),
        name="ada_mod",
    )(c8, ada_w, ada_b.reshape(depth, 1, 6 * D))
    mod = mod[:, :BATCH].reshape(depth, BATCH, 6, D)
    return jnp.pad(mod, ((0, 0), (0, 0), (0, 2), (0, 0)))


def _conv_mixer_kernel(x_ref, mod_ref, nw_ref, win_ref, cw_ref, wout_ref, o_ref, tail_ref, *, tiles_per_seq):
    @pl.when(pl.program_id(0) == 0)
    def _():
        tail_ref[...] = jnp.zeros(tail_ref.shape, F32)

    x = x_ref[...]
    mod = mod_ref[...]
    nw = nw_ref[...]
    h = _rms(x) * nw[0:1] * (1.0 + mod[1:2]) + mod[0:1]
    bcx = _dot(h.astype(BF16), win_ref[...])
    u = bcx[:, D:2 * D] * bcx[:, 2 * D:]
    u2, u1 = _causal_taps(u, tail_ref, (2, 1), pl.program_id(0) % tiles_per_seq == 0)
    cw = cw_ref[...]
    conv = cw[0:1] * u2 + cw[1:2] * u1 + cw[2:3] * u
    y = _dot((bcx[:, :D] * conv).astype(BF16), wout_ref[...])
    o_ref[...] = x + mod[2:3] * (_rms(y) * nw[1:2])


def _conv_mixer(x2, mod_l, nw8, w_in, conv_w, w_out):
    tm = TM_MIX
    tiles_per_seq = SEQ // tm
    cw8 = jnp.pad(conv_w, ((0, SUBLANES - conv_w.shape[0]), (0, 0)))
    return pl.pallas_call(
        functools.partial(_conv_mixer_kernel, tiles_per_seq=tiles_per_seq),
        grid=(N_TOK // tm,),
        in_specs=[
            pl.BlockSpec((tm, D), lambda i: (i, 0)),
            pl.BlockSpec((None, SUBLANES, D), lambda i: (i // tiles_per_seq, 0, 0)),
            pl.BlockSpec((SUBLANES, D), lambda i: (0, 0)),
            pl.BlockSpec((D, 3 * D), lambda i: (0, 0)),
            pl.BlockSpec((SUBLANES, D), lambda i: (0, 0)),
            pl.BlockSpec((D, D), lambda i: (0, 0)),
        ],
        out_specs=pl.BlockSpec((tm, D), lambda i: (i, 0)),
        out_shape=jax.ShapeDtypeStruct((N_TOK, D), F32),
        scratch_shapes=[pltpu.VMEM((SUBLANES, D), F32)],
        compiler_params=_cp(("arbitrary",)),
        name="conv_mixer",
    )(x2, mod_l, nw8, w_in.astype(BF16), cw8, w_out.astype(BF16))


def _lane_pick(lane, mask_val_pairs):
    out = jnp.zeros(lane.shape, F32)
    for idx, val in mask_val_pairs:
        out = jnp.where(lane == float(idx), val, out)
    return out


def _moe_pre_kernel(x_ref, mod_ref, nw_ref, wr_ref, br_ref, wgu_ref, wd_ref, wsg_ref,
                    h_ref, sh_ref, route_ref, cnt_ref, carry_ref):
    tm = x_ref.shape[0]

    @pl.when(pl.program_id(0) == 0)
    def _():
        carry_ref[...] = jnp.zeros(carry_ref.shape, F32)

    x = x_ref[...]
    mod = mod_ref[...]
    nw = nw_ref[...]
    h = _rms(x) * nw[2:3] * (1.0 + mod[4:5]) + mod[3:4]
    hb = h.astype(BF16)
    h_ref[...] = _pack_bf16_pairs(hb)
    h_lo = (h - hb.astype(F32)).astype(BF16)
    hw = _dot(hb, wr_ref[...]) + _dot(h_lo, wr_ref[...])
    logits = hw[:, :LANES] + hw[:, LANES:] + br_ref[0:1]
    gu = _dot(hb, wgu_ref[...])
    lane_i = lax.broadcasted_iota(I32, (tm, LANES), 1)
    lane = lane_i.astype(F32)
    neg = jnp.float32(-jnp.inf)
    is_group = lane_i < N_GROUPS
    gl = jnp.where(is_group, logits, neg)
    gmax = jnp.max(gl, axis=-1, keepdims=True)
    gsel = jnp.min(jnp.where(gl == gmax, lane, float(LANES)), axis=-1, keepdims=True)
    psel = 1.0 / jnp.sum(jnp.where(is_group, jnp.exp(logits - gmax), 0.0), axis=-1, keepdims=True)
    lane_group = ((lane_i - EXPERT_LANE0) >> 3).astype(F32)
    in_group = (lane_i >= EXPERT_LANE0) & (lane_i < EXPERT_LANE0 + N_EXPERTS) & (lane_group == gsel)
    el = jnp.where(in_group, logits, neg)
    v1 = jnp.max(el, axis=-1, keepdims=True)
    i1 = jnp.min(jnp.where(el == v1, lane, float(LANES)), axis=-1, keepdims=True)
    el2 = jnp.where(lane == i1, neg, el)
    v2 = jnp.max(el2, axis=-1, keepdims=True)
    i2 = jnp.min(jnp.where(el2 == v2, lane, float(LANES)), axis=-1, keepdims=True)
    e2 = jnp.exp(v2 - v1)
    g1 = psel / (1.0 + e2)
    g2 = psel * e2 / (1.0 + e2)

    oh1 = lane == i1
    oh2 = lane == i2
    cnt = jnp.where(oh1 | oh2, 1.0, 0.0).astype(BF16)
    row = lax.broadcasted_iota(I32, (tm, tm), 0)
    col = lax.broadcasted_iota(I32, (tm, tm), 1)
    tri = jnp.where(col < row, 1.0, 0.0).astype(BF16)
    pre = _dot(tri, cnt) + carry_ref[0:1]
    r1 = jnp.sum(jnp.where(oh1, pre, 0.0), axis=-1, keepdims=True)
    r2 = jnp.sum(jnp.where(oh2, pre, 0.0), axis=-1, keepdims=True)
    carry_ref[...] = carry_ref[...] + jnp.sum(cnt.astype(F32), axis=0, keepdims=True)
    cnt_ref[...] = carry_ref[...]
    route_ref[...] = _lane_pick(lane, [
        (0, i1 - EXPERT_LANE0), (1, i2 - EXPERT_LANE0),
        (2, g1), (3, g2), (4, r1), (5, r2)])

    hid = _silu(gu[:, :D_SHARED]) * gu[:, D_SHARED:]
    ys = _dot(hid.astype(BF16), wd_ref[...])
    sg = jax.nn.sigmoid(jnp.sum(h * wsg_ref[0:1], axis=-1, keepdims=True))
    sh_ref[...] = sg * ys


def _moe_pre(x1, mod_l, nw8, w_rg, b_rg, w_re, b_re, ws_gate, ws_up, ws_down, w_sg):
    tm = TM_MIX
    tiles_per_seq = SEQ // tm
    pad_l = LANES - N_GROUPS - N_EXPERTS
    wr = jnp.pad(jnp.concatenate([w_rg, w_re], axis=1), ((0, 0), (0, pad_l)))
    wr_hi = wr.astype(BF16)
    wr = jnp.concatenate([wr_hi, (wr - wr_hi.astype(F32)).astype(BF16)], axis=1)
    br = jnp.pad(jnp.concatenate([b_rg, b_re])[None, :], ((0, SUBLANES - 1), (0, pad_l)))
    wgu = jnp.concatenate([ws_gate, ws_up], axis=1).astype(BF16)
    wsg8 = jnp.pad(w_sg.reshape(1, D), ((0, SUBLANES - 1), (0, 0)))
    const = lambda i: (0, 0)
    return pl.pallas_call(
        _moe_pre_kernel,
        grid=(N_TOK // tm,),
        in_specs=[
            pl.BlockSpec((tm, D), lambda i: (i, 0)),
            pl.BlockSpec((None, SUBLANES, D), lambda i: (i // tiles_per_seq, 0, 0)),
            pl.BlockSpec((SUBLANES, D), const),
            pl.BlockSpec((D, 2 * LANES), const),
            pl.BlockSpec((SUBLANES, LANES), const),
            pl.BlockSpec((D, 2 * D_SHARED), const),
            pl.BlockSpec((D_SHARED, D), const),
            pl.BlockSpec((SUBLANES, D), const),
        ],
        out_specs=[
            pl.BlockSpec((tm, D_PACKED), lambda i: (i, 0)),
            pl.BlockSpec((tm, D), lambda i: (i, 0)),
            pl.BlockSpec((tm, LANES), lambda i: (i, 0)),
            pl.BlockSpec((SUBLANES, LANES), const),
        ],
        out_shape=[
            jax.ShapeDtypeStruct((N_TOK, D_PACKED), I32),
            jax.ShapeDtypeStruct((N_TOK, D), F32),
            jax.ShapeDtypeStruct((N_TOK, LANES), F32),
            jax.ShapeDtypeStruct((SUBLANES, LANES), F32),
        ],
        scratch_shapes=[pltpu.VMEM((SUBLANES, LANES), F32)],
        compiler_params=_cp(("arbitrary",)),
        name="moe_pre",
    )(x1, mod_l, nw8, wr, br, wgu, ws_down.astype(BF16), wsg8)


def _n_blocks_per_expert(cnt_row):
    return jnp.floor((cnt_row + (MOE_BLOCK - 1)) * (1.0 / MOE_BLOCK))


def _moe_pos_kernel(cnt_ref, route_ref, pos_ref, seg_ref):
    tm = route_ref.shape[0]
    nb = _n_blocks_per_expert(cnt_ref[...])
    r = lax.broadcasted_iota(I32, (LANES, LANES), 0)
    c = lax.broadcasted_iota(I32, (LANES, LANES), 1)
    excl = jnp.dot(nb, jnp.where(r < c, 1.0, 0.0), precision=HIGHEST, preferred_element_type=F32)
    pstart = excl[0:1] * MOE_BLOCK
    route = route_ref[...]
    lane = lax.broadcasted_iota(I32, (tm, LANES), 1).astype(F32)
    e1 = route[:, 0:1] + EXPERT_LANE0
    e2 = route[:, 1:2] + EXPERT_LANE0
    p1 = jnp.sum(jnp.where(lane == e1, pstart, 0.0), axis=-1, keepdims=True) + route[:, 4:5]
    p2 = jnp.sum(jnp.where(lane == e2, pstart, 0.0), axis=-1, keepdims=True) + route[:, 5:6]
    pos_ref[...] = _lane_pick(lane, [(0, p1), (1, p2)]).astype(I32)

    @pl.when(pl.program_id(0) == 0)
    def _():
        sub = lax.broadcasted_iota(I32, (SUBLANES, LANES), 0)
        seg_ref[...] = jnp.where(sub == 0, nb, excl).astype(I32)


def _moe_pos(cnt, route):
    tm = TM_POS
    pos, seg = pl.pallas_call(
        _moe_pos_kernel,
        grid=(N_TOK // tm,),
        in_specs=[
            pl.BlockSpec((SUBLANES, LANES), lambda i: (0, 0)),
            pl.BlockSpec((tm, LANES), lambda i: (i, 0)),
        ],
        out_specs=[
            pl.BlockSpec((tm, LANES), lambda i: (i, 0)),
            pl.BlockSpec((SUBLANES, LANES), lambda i: (0, 0)),
        ],
        out_shape=[
            jax.ShapeDtypeStruct((N_TOK, LANES), I32),
            jax.ShapeDtypeStruct((SUBLANES, LANES), I32),
        ],
        compiler_params=_cp(("arbitrary",)),
        name="moe_pos",
    )(cnt, route)
    experts = slice(EXPERT_LANE0, EXPERT_LANE0 + N_EXPERTS)
    return pos[:, :2].T.reshape(-1), seg[0, experts], seg[1, experts]


def _expert_kernel(nblk_ref, blk0_ref, buf_ref, wg_ref, wu_ref, wd_ref, eo_ref,
                   xin_ref, out_ref, wgb_ref, wub_ref, wdb_ref, sem_in, sem_out):
    e = pl.program_id(0)
    last = pl.num_programs(0) - 1
    n = nblk_ref[e]
    blk0 = blk0_ref[e]
    total = blk0_ref[last] + nblk_ref[last]
    bm = MOE_BLOCK
    n_in = xin_ref.shape[0]

    def rows(g):
        return pl.ds(pl.multiple_of(g * bm, bm), bm)

    def in_copy(g):
        slot = lax.rem(g, n_in)
        return pltpu.make_async_copy(buf_ref.at[rows(g)], xin_ref.at[slot], sem_in.at[slot])

    def out_copy(g):
        return pltpu.make_async_copy(out_ref.at[g & 1], eo_ref.at[rows(g)], sem_out.at[g & 1])

    @pl.when(n > 0)
    def _():
        wgb_ref[...] = wg_ref[...].astype(BF16)
        wub_ref[...] = wu_ref[...].astype(BF16)
        wdb_ref[...] = wd_ref[...].astype(BF16)

    def block(j, carry):
        g = blk0 + j

        @pl.when(g == 0)
        def _():
            in_copy(g).start()

            @pl.when(total > 1)
            def _():
                in_copy(g + 1).start()

        @pl.when(g + 2 < total)
        def _():
            in_copy(g + 2).start()

        in_copy(g).wait()

        @pl.when(g >= 2)
        def _():
            out_copy(g - 2).wait()

        xb = _unpack_bf16_pairs(xin_ref[lax.rem(g, n_in)]).astype(BF16)
        hid = _silu(_dot(xb, wgb_ref[...])) * _dot(xb, wub_ref[...])
        out_ref[g & 1] = _pack_bf16_pairs(_dot(hid.astype(BF16), wdb_ref[...]).astype(BF16))
        out_copy(g).start()
        return carry

    lax.fori_loop(0, n, block, 0)

    @pl.when((e == last) & (total >= 2))
    def _():
        out_copy(total - 2).wait()

    @pl.when((e == last) & (total >= 1))
    def _():
        out_copy(total - 1).wait()


def _moe_experts(nblk_e, blk0_e, buf, w_gate, w_up, w_down, layer):
    bm = MOE_BLOCK
    wspec = lambda shape: pl.BlockSpec((None, None) + shape, lambda e, nb, b0: (layer, e, 0, 0))
    return pl.pallas_call(
        _expert_kernel,
        grid_spec=pltpu.PrefetchScalarGridSpec(
            num_scalar_prefetch=2,
            grid=(N_EXPERTS,),
            in_specs=[
                pl.BlockSpec(memory_space=pl.ANY),
                wspec((D, D_EXPERT)),
                wspec((D, D_EXPERT)),
                wspec((D_EXPERT, D)),
            ],
            out_specs=pl.BlockSpec(memory_space=pl.ANY),
            scratch_shapes=[
                pltpu.VMEM((3, bm, D_PACKED), I32),
                pltpu.VMEM((2, bm, D_PACKED), I32),
                pltpu.VMEM((D, D_EXPERT), BF16),
                pltpu.VMEM((D, D_EXPERT), BF16),
                pltpu.VMEM((D_EXPERT, D), BF16),
                pltpu.SemaphoreType.DMA((3,)),
                pltpu.SemaphoreType.DMA((2,)),
            ],
        ),
        out_shape=jax.ShapeDtypeStruct((P_ROWS, D_PACKED), I32),
        compiler_params=_cp(("arbitrary",)),
        name="moe_experts",
    )(nblk_e, blk0_e, buf, w_gate, w_up, w_down)


SC_CORES = 2
SC_SUBCORES = 16
SC_WORKERS = SC_CORES * SC_SUBCORES
SC_CHUNK = 64


def _sc_gather_store(table_hbm, idx_v, out_hbm, out_base, n_rows, bufs, sems):
    n_chunks = n_rows // SC_CHUNK
    assert n_chunks % 2 == 0

    def gather(j, b):
        off = pl.multiple_of(j * SC_CHUNK, SC_CHUNK)
        return pltpu.make_async_copy(table_hbm.at[idx_v.at[pl.ds(off, SC_CHUNK)]], bufs[b], sems[b])

    gather(0, 0).start()

    @pl.loop(0, n_chunks, step=2)
    def _(j):
        for b in range(2):
            jj = j + b

            @pl.when(jj + 1 < n_chunks)
            def _():
                gather(jj + 1, 1 - b).start()

            gather(jj, b).wait()
            row0 = pl.multiple_of(out_base + jj * SC_CHUNK, SC_CHUNK)
            pltpu.sync_copy(bufs[b], out_hbm.at[pl.ds(row0, SC_CHUNK)])


def _sc_row_buffers(width, dtype):
    return [pltpu.VMEM((SC_CHUNK, width), dtype), pltpu.VMEM((SC_CHUNK, width), dtype),
            pltpu.SemaphoreType.DMA, pltpu.SemaphoreType.DMA]


def _sc_gather_rows(table, idx):
    n_idx = idx.shape[0]
    width = table.shape[1]
    per_w = n_idx // SC_WORKERS
    mesh = plsc.VectorSubcoreMesh(core_axis_name="c", subcore_axis_name="s")

    def body(table_hbm, idx_hbm, out_hbm, idx_v, buf0, buf1, sem0, sem1):
        wid = lax.axis_index("s") * SC_CORES + lax.axis_index("c")
        base = wid * per_w
        pltpu.sync_copy(idx_hbm.at[pl.ds(base, per_w)], idx_v)
        _sc_gather_store(table_hbm, idx_v, out_hbm, base, per_w, (buf0, buf1), (sem0, sem1))

    return pl.kernel(
        body,
        out_type=jax.ShapeDtypeStruct((n_idx, width), table.dtype),
        mesh=mesh,
        scratch_types=[pltpu.VMEM((per_w,), I32)] + _sc_row_buffers(width, table.dtype),
        name="sc_gather_rows",
    )(table, idx)


def _sc_dispatch_rows(h, pos_km):
    per_w = P_ROWS // SC_WORKERS
    lanes = 16
    mesh = plsc.VectorSubcoreMesh(core_axis_name="c", subcore_axis_name="s")

    def body(h_hbm, pos_hbm, buf_hbm, pos_v, tok_v, buf0, buf1, sem0, sem1):
        wid = lax.axis_index("s") * SC_CORES + lax.axis_index("c")
        base = wid * per_w
        pltpu.sync_copy(pos_hbm, pos_v)

        @pl.loop(0, per_w // lanes)
        def _(i):
            off = pl.multiple_of(i * lanes, lanes)
            tok_v[pl.ds(off, lanes)] = (base + off + lax.iota(I32, lanes)) & (N_TOK - 1)

        @pl.loop(0, N_SLOTS // lanes)
        def _(i):
            off = pl.multiple_of(i * lanes, lanes)
            local = pos_v[pl.ds(off, lanes)] - base
            mine = (local >= 0) & (local < per_w)
            slot = off + lax.iota(I32, lanes)
            plsc.store_scatter(tok_v, [jnp.where(mine, local, 0)], slot & (N_TOK - 1), mask=mine)

        _sc_gather_store(h_hbm, tok_v, buf_hbm, base, per_w, (buf0, buf1), (sem0, sem1))

    return pl.kernel(
        body,
        out_type=jax.ShapeDtypeStruct((P_ROWS, h.shape[1]), h.dtype),
        mesh=mesh,
        scratch_types=[pltpu.VMEM((N_SLOTS,), I32), pltpu.VMEM((per_w,), I32)]
        + _sc_row_buffers(h.shape[1], h.dtype),
        compiler_params=pltpu.CompilerParams(needs_layout_passes=False),
        name="sc_dispatch_rows",
    )(h, pos_km)


def _combine_kernel(y0_ref, y1_ref, x_ref, sh_ref, route_ref, mod_ref, nw_ref, o_ref):
    route = route_ref[...]
    y0 = _unpack_bf16_pairs(y0_ref[...])
    y1 = _unpack_bf16_pairs(y1_ref[...])
    y = route[:, 2:3] * y0 + route[:, 3:4] * y1 + sh_ref[...]
    mod = mod_ref[...]
    o_ref[...] = x_ref[...] + mod[5:6] * (_rms(y) * nw_ref[3:4])


def _moe_combine(pos_km, eo, x1, sh, route, mod_l, nw8):
    tm = TM_MIX
    tiles_per_seq = SEQ // tm
    n_tiles = N_TOK // tm
    y2 = _sc_gather_rows(eo, pos_km)
    return pl.pallas_call(
        _combine_kernel,
        grid=(n_tiles,),
        in_specs=[
            pl.BlockSpec((tm, D_PACKED), lambda i: (i, 0)),
            pl.BlockSpec((tm, D_PACKED), lambda i: (i + n_tiles, 0)),
            pl.BlockSpec((tm, D), lambda i: (i, 0)),
            pl.BlockSpec((tm, D), lambda i: (i, 0)),
            pl.BlockSpec((tm, LANES), lambda i: (i, 0)),
            pl.BlockSpec((None, SUBLANES, D), lambda i: (i // tiles_per_seq, 0, 0)),
            pl.BlockSpec((SUBLANES, D), lambda i: (0, 0)),
        ],
        out_specs=pl.BlockSpec((tm, D), lambda i: (i, 0)),
        out_shape=jax.ShapeDtypeStruct((N_TOK, D), F32),
        compiler_params=_cp(("arbitrary",)),
        name="moe_combine",
    )(y2, y2, x1, sh, route, mod_l, nw8)


def _hier_moe_block(x1, mod_l, nw8, layer, w_rg, b_rg, w_re, b_re, w_gate, w_up, w_down,
                    ws_gate, ws_up, ws_down, w_sg):
    h, sh, route, cnt = _moe_pre(x1, mod_l, nw8, w_rg, b_rg, w_re, b_re, ws_gate, ws_up, ws_down, w_sg)
    pos_km, nblk_e, blk0_e = _moe_pos(cnt, route)
    buf = _sc_dispatch_rows(h, pos_km)
    eo = _moe_experts(nblk_e, blk0_e, buf, w_gate, w_up, w_down, layer)
    return _moe_combine(pos_km, eo, x1, sh, route, mod_l, nw8)


def _gdn_pre_kernel(x_ref, mod_ref, nw_ref, w_ref, wba_ref, cw_ref, misc_ref,
                    q_ref, k_ref, v_ref, z_ref, gb_ref, gt_ref, tail_ref, proj_ref, ba_ref, *, tiles_per_seq):
    i = pl.program_id(0)

    @pl.when(i == 0)
    def _():
        tail_ref[...] = jnp.zeros(tail_ref.shape, F32)
        proj_ref[1] = jnp.zeros(proj_ref.shape[1:], F32)
        ba_ref[1] = jnp.zeros(ba_ref.shape[1:], F32)

    refs = (x_ref, mod_ref, nw_ref, w_ref, wba_ref, cw_ref, misc_ref,
            q_ref, k_ref, v_ref, z_ref, gb_ref, gt_ref, tail_ref, proj_ref, ba_ref)
    first_of_seq = (i - 1) % tiles_per_seq == 0

    @pl.when((i & 1) == 0)
    def _():
        _gdn_pre_step(*refs, cur=0, prev=1, first_of_seq=first_of_seq)

    @pl.when((i & 1) == 1)
    def _():
        _gdn_pre_step(*refs, cur=1, prev=0, first_of_seq=first_of_seq)


def _gdn_pre_step(x_ref, mod_ref, nw_ref, w_ref, wba_ref, cw_ref, misc_ref,
                  q_ref, k_ref, v_ref, z_ref, gb_ref, gt_ref, tail_ref, proj_ref, ba_ref,
                  *, cur, prev, first_of_seq):
    tm = x_ref.shape[0]
    qkv_w = 3 * D
    x = x_ref[...]
    mod = mod_ref[...]
    nw = nw_ref[...]
    h = _rms(x) * nw[0:1] * (1.0 + mod[1:2]) + mod[0:1]
    hb = h.astype(BF16)
    proj_ref[cur] = _dot(hb, w_ref[...])
    ba_ref[cur] = _dot(hb, wba_ref[...])

    z_ref[...] = proj_ref[prev, :, qkv_w:].astype(BF16)
    pre = proj_ref[prev, :, :qkv_w]
    p3, p2, p1 = _causal_taps(pre, tail_ref, (3, 2, 1), first_of_seq)
    cw = cw_ref[...]
    conv = cw[0:1] * p3 + cw[1:2] * p2 + cw[2:3] * p1 + cw[3:4] * pre
    act = _silu(conv)
    for hd in range(HEADS):
        lo = hd * HEAD_DIM
        qh = act[:, lo:lo + HEAD_DIM]
        kh = act[:, D + lo:D + lo + HEAD_DIM]
        qn = qh * lax.rsqrt(jnp.sum(qh * qh, axis=-1, keepdims=True) + EPS) * (HEAD_DIM ** -0.5)
        kn = kh * lax.rsqrt(jnp.sum(kh * kh, axis=-1, keepdims=True) + EPS)
        q_ref[:, lo:lo + HEAD_DIM] = qn.astype(BF16)
        k_ref[:, lo:lo + HEAD_DIM] = kn.astype(BF16)
    v_ref[...] = act[:, 2 * D:].astype(BF16)

    ba = ba_ref[prev]
    misc = misc_ref[...]
    beta = jax.nn.sigmoid(ba)
    sp_in = ba + misc[1:2]
    softplus = jnp.maximum(sp_in, 0.0) + jnp.log(1.0 + jnp.exp(-jnp.abs(sp_in)))
    g = -jnp.exp(misc[0:1]) * softplus
    row = lax.broadcasted_iota(I32, (tm, tm), 0)
    col = lax.broadcasted_iota(I32, (tm, tm), 1)
    tri = jnp.where((col <= row) & ((col >> CHUNK_SHIFT) == (row >> CHUNK_SHIFT)), 1.0, 0.0)
    gc = jnp.dot(tri, g, precision=HIGHEST, preferred_element_type=F32)
    lane = lax.broadcasted_iota(I32, (tm, LANES), 1)
    gb = jnp.where(lane < HEADS, beta, gc)
    gb_ref[...] = gb
    for c in range(tm // CHUNK):
        blk = jnp.concatenate([gb[c * CHUNK:(c + 1) * CHUNK], jnp.zeros((LANES - CHUNK, LANES), F32)], axis=0)
        gt_ref[c] = blk.T[HEADS:2 * HEADS, :]


def _gdn_pre(x2, mod_l, nw8, w_in, conv_w, a_log, dt_bias):
    tm = TM_GDN
    tiles_per_seq = SEQ // tm
    qkvz = 4 * D
    w_main = w_in[:, :qkvz].astype(BF16)
    wba = jnp.pad(w_in[:, qkvz:], ((0, 0), (0, LANES - 2 * HEADS))).astype(BF16)
    cw8 = jnp.pad(conv_w, ((0, SUBLANES - conv_w.shape[0]), (0, 0)))
    misc = jnp.zeros((SUBLANES, LANES), F32)
    misc = misc.at[0, HEADS:2 * HEADS].set(a_log).at[1, HEADS:2 * HEADS].set(dt_bias)
    const = lambda i: (0, 0)
    n_tiles = N_TOK // tm
    src = lambda i: jnp.minimum(i, n_tiles - 1)
    dst = lambda i: jnp.maximum(i - 1, 0)
    tok = lambda i: (dst(i), 0)
    return pl.pallas_call(
        functools.partial(_gdn_pre_kernel, tiles_per_seq=tiles_per_seq),
        grid=(n_tiles + 1,),
        in_specs=[
            pl.BlockSpec((tm, D), lambda i: (src(i), 0)),
            pl.BlockSpec((None, SUBLANES, D), lambda i: (src(i) // tiles_per_seq, 0, 0)),
            pl.BlockSpec((SUBLANES, D), const),
            pl.BlockSpec((D, qkvz), const),
            pl.BlockSpec((D, LANES), const),
            pl.BlockSpec((SUBLANES, 3 * D), const),
            pl.BlockSpec((SUBLANES, LANES), const),
        ],
        out_specs=[
            pl.BlockSpec((tm, D), tok),
            pl.BlockSpec((tm, D), tok),
            pl.BlockSpec((tm, D), tok),
            pl.BlockSpec((tm, D), tok),
            pl.BlockSpec((tm, LANES), tok),
            pl.BlockSpec((tm // CHUNK, HEADS, LANES), lambda i: (dst(i), 0, 0)),
        ],
        out_shape=[
            jax.ShapeDtypeStruct((N_TOK, D), BF16),
            jax.ShapeDtypeStruct((N_TOK, D), BF16),
            jax.ShapeDtypeStruct((N_TOK, D), BF16),
            jax.ShapeDtypeStruct((N_TOK, D), BF16),
            jax.ShapeDtypeStruct((N_TOK, LANES), F32),
            jax.ShapeDtypeStruct((N_TOK // CHUNK, HEADS, LANES), F32),
        ],
        scratch_shapes=[pltpu.VMEM((SUBLANES, 3 * D), F32),
                        pltpu.VMEM((2, tm, qkvz), F32),
                        pltpu.VMEM((2, tm, LANES), F32)],
        compiler_params=_cp(("arbitrary",)),
        name="gdn_pre",
    )(x2, mod_l, nw8, w_main, wba, cw8, misc)


def _dot_nt(a, b):
    return lax.dot_general(a, b, (((1,), (1,)), ((), ())), preferred_element_type=F32)


def _dot_tn(a, b):
    return lax.dot_general(a, b, (((0,), (0,)), ((), ())), preferred_element_type=F32)


def _gdn_chunk_kernel(q_ref, k_ref, v_ref, gb_ref, gt_ref, o_ref, s_ref):
    @pl.when(pl.program_id(1) == 0)
    def _():
        s_ref[...] = jnp.zeros(s_ref.shape, F32)

    c = CHUNK
    nb = q_ref.shape[0]
    row = lax.broadcasted_iota(I32, (c, c), 0)
    col = lax.broadcasted_iota(I32, (c, c), 1)
    causal = col <= row
    strict = col < row
    eye = jnp.where(col == row, 1.0, 0.0)
    chains = [(b, hd) for b in range(nb) for hd in range(HEADS)]
    st = []
    for b, hd in chains:
        lo = hd * HEAD_DIM
        gb = gb_ref[b]
        q = q_ref[b, :, lo:lo + HEAD_DIM].astype(F32)
        k = k_ref[b, :, lo:lo + HEAD_DIM].astype(F32)
        v = v_ref[b, :, lo:lo + HEAD_DIM].astype(F32)
        beta = gb[:, hd:hd + 1]
        gcol = gb[:, HEADS + hd:HEADS + hd + 1]
        grow = gt_ref[b, hd:hd + 1, 0:c]
        glast = gcol[c - 1:c, :]
        egc = jnp.exp(gcol)
        kb = k * beta
        st.append(dict(
            decay=jnp.exp(jnp.where(causal, gcol - grow, -jnp.inf)),
            kq=jnp.concatenate([kb, q], axis=0).astype(BF16),
            kbf=k.astype(BF16),
            rhs=jnp.concatenate([v * beta, kb * egc], axis=-1).astype(BF16),
            qd=(q * egc).astype(BF16),
            kd=(k * jnp.exp(glast - gcol)).astype(BF16),
            eg=jnp.exp(glast)))
    for x in st:
        kk = _dot_nt(x["kq"], x["kbf"])
        x["p"] = jnp.where(strict, kk[:c] * x["decay"], 0.0)
        x["attn"] = jnp.where(causal, kk[c:] * x["decay"], 0.0).astype(BF16)
        x["t"] = eye - x["p"]
    for _ in range(5):
        for x in st:
            pb = x["p"].astype(BF16)
            x["p"] = _dot(pb, pb)
        for x in st:
            x["t"] = x["t"] + _dot(x["t"].astype(BF16), x["p"].astype(BF16))
    for x in st:
        x["uw"] = _dot(x["t"].astype(BF16), x["rhs"])
    for x, (b, hd) in zip(st, chains):
        s = s_ref[b * HEADS + hd]
        x["s"] = s
        ws = _dot(jnp.concatenate([x["uw"][:, HEAD_DIM:].astype(BF16), x["qd"]], axis=0), s.astype(BF16))
        x["vb"] = (x["uw"][:, :HEAD_DIM] - ws[:c]).astype(BF16)
        x["o"] = ws[c:]
    for x, (b, hd) in zip(st, chains):
        lo = hd * HEAD_DIM
        s_ref[b * HEADS + hd] = x["s"] * x["eg"] + _dot_tn(x["kd"], x["vb"])
        o_ref[b, :, lo:lo + HEAD_DIM] = (x["o"] + _dot(x["attn"], x["vb"])).astype(BF16)


def _gdn_chunks(q, k, v, gb, gt):
    nb = GDN_BATCH_PER_STEP
    n_chunks = SEQ // CHUNK
    tok = lambda b, c: (b, c, 0)
    o = pl.pallas_call(
        _gdn_chunk_kernel,
        grid=(BATCH // nb, n_chunks),
        in_specs=[
            pl.BlockSpec((nb, CHUNK, D), tok),
            pl.BlockSpec((nb, CHUNK, D), tok),
            pl.BlockSpec((nb, CHUNK, D), tok),
            pl.BlockSpec((nb, CHUNK, LANES), tok),
            pl.BlockSpec((nb, None, HEADS, LANES), lambda b, c: (b, c, 0, 0)),
        ],
        out_specs=pl.BlockSpec((nb, CHUNK, D), tok),
        out_shape=jax.ShapeDtypeStruct((BATCH, SEQ, D), BF16),
        scratch_shapes=[pltpu.VMEM((nb * HEADS, HEAD_DIM, HEAD_DIM), F32)],
        compiler_params=_cp(("arbitrary", "arbitrary")),
        name="gdn_chunks",
    )(q.reshape(BATCH, SEQ, D), k.reshape(BATCH, SEQ, D), v.reshape(BATCH, SEQ, D),
      gb.reshape(BATCH, SEQ, LANES), gt.reshape(BATCH, n_chunks, HEADS, LANES))
    return o.reshape(N_TOK, D)


def _gdn_post_kernel(o_ref, z_ref, x_ref, mod_ref, nw_ref, gnw_ref, wout_ref, out_ref):
    gnw = gnw_ref[0:1]
    parts = []
    for hd in range(HEADS):
        lo = hd * HEAD_DIM
        oh = o_ref[:, lo:lo + HEAD_DIM].astype(F32)
        zh = z_ref[:, lo:lo + HEAD_DIM].astype(F32)
        parts.append((_rms(oh) * gnw * _silu(zh)).astype(BF16))
    y = _dot(jnp.concatenate(parts, axis=-1), wout_ref[...])
    mod = mod_ref[...]
    out_ref[...] = x_ref[...] + mod[2:3] * (_rms(y) * nw_ref[1:2])


def _gdn_post(o, z, x2, mod_l, nw8, gdn_norm_w, w_out):
    tm = TM_MIX
    tiles_per_seq = SEQ // tm
    gnw8 = jnp.pad(gdn_norm_w.reshape(1, HEAD_DIM), ((0, SUBLANES - 1), (0, 0)))
    tok = lambda i: (i, 0)
    const = lambda i: (0, 0)
    return pl.pallas_call(
        _gdn_post_kernel,
        grid=(N_TOK // tm,),
        in_specs=[
            pl.BlockSpec((tm, D), tok),
            pl.BlockSpec((tm, D), tok),
            pl.BlockSpec((tm, D), tok),
            pl.BlockSpec((None, SUBLANES, D), lambda i: (i // tiles_per_seq, 0, 0)),
            pl.BlockSpec((SUBLANES, D), const),
            pl.BlockSpec((SUBLANES, HEAD_DIM), const),
            pl.BlockSpec((D, D), const),
        ],
        out_specs=pl.BlockSpec((tm, D), tok),
        out_shape=jax.ShapeDtypeStruct((N_TOK, D), F32),
        compiler_params=_cp(("arbitrary",)),
        name="gdn_post",
    )(o, z, x2, mod_l, nw8, gnw8, w_out.astype(BF16))


def kernel(x, c, ada_w, ada_b, norm_w, conv_in_w, conv_w, conv_out_w, gdn_in_w, gdn_conv_w, gdn_a_log,
           gdn_dt_bias, gdn_norm_w, gdn_out_w, moe_group_w, moe_group_b, moe_expert_w, moe_expert_b,
           moe_w_gate, moe_w_up, moe_w_down, shared_w_gate, shared_w_up, shared_w_down, shared_gate_w):
    mod = _ada_mod(c, ada_w, ada_b)
    nw8 = jnp.pad(norm_w, ((0, 0), (0, SUBLANES - norm_w.shape[1]), (0, 0)))
    x2 = x.reshape(N_TOK, D)

    def moe(x1, layer):
        return _hier_moe_block(
            x1, mod[layer], nw8[layer], layer,
            moe_group_w[layer], moe_group_b[layer], moe_expert_w[layer], moe_expert_b[layer],
            moe_w_gate, moe_w_up, moe_w_down,
            shared_w_gate[layer], shared_w_up[layer], shared_w_down[layer], shared_gate_w[layer])

    x2 = _conv_mixer(x2, mod[0], nw8[0], conv_in_w[0], conv_w[0], conv_out_w[0])
    x2 = moe(x2, 0)
    q, k, v, z, gb, gt = _gdn_pre(x2, mod[1], nw8[1], gdn_in_w[0], gdn_conv_w[0], gdn_a_log[0], gdn_dt_bias[0])
    o = _gdn_chunks(q, k, v, gb, gt)
    x2 = _gdn_post(o, z, x2, mod[1], nw8[1], gdn_norm_w[0], gdn_out_w[0])
    x2 = moe(x2, 1)
    return x2.reshape(BATCH, SEQ, D)
```

```python
import functools

import jax
import jax.numpy as jnp
from jax import lax
from jax.experimental import pallas as pl
from jax.experimental.pallas import tpu as pltpu
from jax.experimental.pallas import tpu_sc as plsc

F32 = jnp.float32
BF16 = jnp.bfloat16
I32 = jnp.int32
HIGHEST = lax.Precision.HIGHEST

D = 1024
D_PACKED = D // 2
BATCH = 4
SEQ = 4096
N_TOK = BATCH * SEQ
HEADS = 8
HEAD_DIM = 128
CHUNK = 64
CHUNK_SHIFT = 6
N_GROUPS = 8
N_EXPERTS = 64
D_EXPERT = 256
D_SHARED = 512
EPS = 1e-6
LANES = 128
SUBLANES = 8
EXPERT_LANE0 = N_GROUPS

MOE_BLOCK = 256
N_SLOTS = N_TOK * 2
N_BLOCKS = N_SLOTS // MOE_BLOCK + N_EXPERTS
P_ROWS = N_BLOCKS * MOE_BLOCK

TM_MIX = 512
TM_GDN = 256
TM_DMA = 256
TM_POS = 2048
GDN_BATCH_PER_STEP = 4
VMEM_LIMIT = 56 * 1024 * 1024


def _cp(sem):
    return pltpu.CompilerParams(dimension_semantics=sem, vmem_limit_bytes=VMEM_LIMIT)


def _rms(x):
    return x * lax.rsqrt(jnp.mean(x * x, axis=-1, keepdims=True) + EPS)


def _silu(x):
    return x * jax.nn.sigmoid(x)


def _dot(a, b):
    return jnp.dot(a, b, preferred_element_type=F32)


U32 = jnp.uint32
HI16 = 0xFFFF0000


def _pack_bf16_pairs(xb):
    half = xb.shape[1] // 2
    bits = lax.bitcast_convert_type(xb.astype(F32), U32)
    return lax.bitcast_convert_type(bits[:, :half] | (bits[:, half:] >> 16), I32)


def _unpack_bf16_pairs(words):
    bits = lax.bitcast_convert_type(words, U32)
    hi = lax.bitcast_convert_type(bits & jnp.uint32(HI16), F32)
    lo = lax.bitcast_convert_type(bits << 16, F32)
    return jnp.concatenate([hi, lo], axis=1)


def _causal_taps(x, tail_ref, taps, first_of_seq):
    tail = jnp.where(first_of_seq, 0.0, tail_ref[...])
    sub = lax.broadcasted_iota(I32, tail.shape, 0)
    out = []
    for s in taps:
        rolled = pltpu.roll(x, s, axis=0)
        head = jnp.where(sub < s, pltpu.roll(tail, s, axis=0), rolled[0:SUBLANES])
        out.append(jnp.concatenate([head, rolled[SUBLANES:]], axis=0))
    tail_ref[...] = x[x.shape[0] - SUBLANES:]
    return out


def _ada_kernel(c_ref, w_ref, b_ref, o_ref):
    cs = _silu(c_ref[...])
    o_ref[0] = _dot(cs.astype(BF16), w_ref[0].astype(BF16)) + b_ref[0]


def _ada_mod(c, ada_w, ada_b):
    depth = ada_w.shape[0]
    tn = 1024
    c8 = jnp.pad(c, ((0, SUBLANES - BATCH), (0, 0)))
    mod = pl.pallas_call(
        _ada_kernel,
        grid=(depth, 6 * D // tn),
        in_specs=[
            pl.BlockSpec((SUBLANES, D), lambda l, j: (0, 0)),
            pl.BlockSpec((1, D, tn), lambda l, j: (l, 0, j)),
            pl.BlockSpec((1, 1, tn), lambda l, j: (l, 0, j)),
        ],
        out_specs=pl.BlockSpec((1, SUBLANES, tn), lambda l, j: (l, 0, j)),
        out_shape=jax.ShapeDtypeStruct((depth, SUBLANES, 6 * D), F32),
        compiler_params=_cp(("arbitrary", "arbitrary")),
        name="ada_mod",
    )(c8, ada_w, ada_b.reshape(depth, 1, 6 * D))
    mod = mod[:, :BATCH].reshape(depth, BATCH, 6, D)
    return jnp.pad(mod, ((0, 0), (0, 0), (0, 2), (0, 0)))


def _conv_mixer_kernel(x_ref, mod_ref, nw_ref, win_ref, cw_ref, wout_ref, o_ref, tail_ref, *, tiles_per_seq):
    @pl.when(pl.program_id(0) == 0)
    def _():
        tail_ref[...] = jnp.zeros(tail_ref.shape, F32)

    x = x_ref[...]
    mod = mod_ref[...]
    nw = nw_ref[...]
    h = _rms(x) * nw[0:1] * (1.0 + mod[1:2]) + mod[0:1]
    bcx = _dot(h.astype(BF16), win_ref[...])
    u = bcx[:, D:2 * D] * bcx[:, 2 * D:]
    u2, u1 = _causal_taps(u, tail_ref, (2, 1), pl.program_id(0) % tiles_per_seq == 0)
    cw = cw_ref[...]
    conv = cw[0:1] * u2 + cw[1:2] * u1 + cw[2:3] * u
    y = _dot((bcx[:, :D] * conv).astype(BF16), wout_ref[...])
    o_ref[...] = x + mod[2:3] * (_rms(y) * nw[1:2])


def _conv_mixer(x2, mod_l, nw8, w_in, conv_w, w_out):
    tm = TM_MIX
    tiles_per_seq = SEQ // tm
    cw8 = jnp.pad(conv_w, ((0, SUBLANES - conv_w.shape[0]), (0, 0)))
    return pl.pallas_call(
        functools.partial(_conv_mixer_kernel, tiles_per_seq=tiles_per_seq),
        grid=(N_TOK // tm,),
        in_specs=[
            pl.BlockSpec((tm, D), lambda i: (i, 0)),
            pl.BlockSpec((None, SUBLANES, D), lambda i: (i // tiles_per_seq, 0, 0)),
            pl.BlockSpec((SUBLANES, D), lambda i: (0, 0)),
            pl.BlockSpec((D, 3 * D), lambda i: (0, 0)),
            pl.BlockSpec((SUBLANES, D), lambda i: (0, 0)),
            pl.BlockSpec((D, D), lambda i: (0, 0)),
        ],
        out_specs=pl.BlockSpec((tm, D), lambda i: (i, 0)),
        out_shape=jax.ShapeDtypeStruct((N_TOK, D), F32),
        scratch_shapes=[pltpu.VMEM((SUBLANES, D), F32)],
        compiler_params=_cp(("arbitrary",)),
        name="conv_mixer",
    )(x2, mod_l, nw8, w_in.astype(BF16), cw8, w_out.astype(BF16))


def _lane_pick(lane, mask_val_pairs):
    out = jnp.zeros(lane.shape, F32)
    for idx, val in mask_val_pairs:
        out = jnp.where(lane == float(idx), val, out)
    return out


def _moe_pre_kernel(x_ref, mod_ref, nw_ref, wr_ref, br_ref, wgu_ref, wd_ref, wsg_ref,
                    h_ref, sh_ref, route_ref, cnt_ref, carry_ref):
    tm = x_ref.shape[0]

    @pl.when(pl.program_id(0) == 0)
    def _():
        carry_ref[...] = jnp.zeros(carry_ref.shape, F32)

    x = x_ref[...]
    mod = mod_ref[...]
    nw = nw_ref[...]
    h = _rms(x) * nw[2:3] * (1.0 + mod[4:5]) + mod[3:4]
    hb = h.astype(BF16)
    h_ref[...] = _pack_bf16_pairs(hb)
    h_lo = (h - hb.astype(F32)).astype(BF16)
    hw = _dot(hb, wr_ref[...]) + _dot(h_lo, wr_ref[...])
    logits = hw[:, :LANES] + hw[:, LANES:] + br_ref[0:1]
    gu = _dot(hb, wgu_ref[...])
    lane_i = lax.broadcasted_iota(I32, (tm, LANES), 1)
    lane = lane_i.astype(F32)
    neg = jnp.float32(-jnp.inf)
    is_group = lane_i < N_GROUPS
    gl = jnp.where(is_group, logits, neg)
    gmax = jnp.max(gl, axis=-1, keepdims=True)
    gsel = jnp.min(jnp.where(gl == gmax, lane, float(LANES)), axis=-1, keepdims=True)
    psel = 1.0 / jnp.sum(jnp.where(is_group, jnp.exp(logits - gmax), 0.0), axis=-1, keepdims=True)
    lane_group = ((lane_i - EXPERT_LANE0) >> 3).astype(F32)
    in_group = (lane_i >= EXPERT_LANE0) & (lane_i < EXPERT_LANE0 + N_EXPERTS) & (lane_group == gsel)
    el = jnp.where(in_group, logits, neg)
    v1 = jnp.max(el, axis=-1, keepdims=True)
    i1 = jnp.min(jnp.where(el == v1, lane, float(LANES)), axis=-1, keepdims=True)
    el2 = jnp.where(lane == i1, neg, el)
    v2 = jnp.max(el2, axis=-1, keepdims=True)
    i2 = jnp.min(jnp.where(el2 == v2, lane, float(LANES)), axis=-1, keepdims=True)
    e2 = jnp.exp(v2 - v1)
    g1 = psel / (1.0 + e2)
    g2 = psel * e2 / (1.0 + e2)

    oh1 = lane == i1
    oh2 = lane == i2
    cnt = jnp.where(oh1 | oh2, 1.0, 0.0).astype(BF16)
    row = lax.broadcasted_iota(I32, (tm, tm), 0)
    col = lax.broadcasted_iota(I32, (tm, tm), 1)
    tri = jnp.where(col < row, 1.0, 0.0).astype(BF16)
    pre = _dot(tri, cnt) + carry_ref[0:1]
    r1 = jnp.sum(jnp.where(oh1, pre, 0.0), axis=-1, keepdims=True)
    r2 = jnp.sum(jnp.where(oh2, pre, 0.0), axis=-1, keepdims=True)
    carry_ref[...] = carry_ref[...] + jnp.sum(cnt.astype(F32), axis=0, keepdims=True)
    cnt_ref[...] = carry_ref[...]
    route_ref[...] = _lane_pick(lane, [
        (0, i1 - EXPERT_LANE0), (1, i2 - EXPERT_LANE0),
        (2, g1), (3, g2), (4, r1), (5, r2)])

    hid = _silu(gu[:, :D_SHARED]) * gu[:, D_SHARED:]
    ys = _dot(hid.astype(BF16), wd_ref[...])
    sg = jax.nn.sigmoid(jnp.sum(h * wsg_ref[0:1], axis=-1, keepdims=True))
    sh_ref[...] = sg * ys


def _moe_pre(x1, mod_l, nw8, w_rg, b_rg, w_re, b_re, ws_gate, ws_up, ws_down, w_sg):
    tm = TM_MIX
    tiles_per_seq = SEQ // tm
    pad_l = LANES - N_GROUPS - N_EXPERTS
    wr = jnp.pad(jnp.concatenate([w_rg, w_re], axis=1), ((0, 0), (0, pad_l)))
    wr_hi = wr.astype(BF16)
    wr = jnp.concatenate([wr_hi, (wr - wr_hi.astype(F32)).astype(BF16)], axis=1)
    br = jnp.pad(jnp.concatenate([b_rg, b_re])[None, :], ((0, SUBLANES - 1), (0, pad_l)))
    wgu = jnp.concatenate([ws_gate, ws_up], axis=1).astype(BF16)
    wsg8 = jnp.pad(w_sg.reshape(1, D), ((0, SUBLANES - 1), (0, 0)))
    const = lambda i: (0, 0)
    return pl.pallas_call(
        _moe_pre_kernel,
        grid=(N_TOK // tm,),
        in_specs=[
            pl.BlockSpec((tm, D), lambda i: (i, 0)),
            pl.BlockSpec((None, SUBLANES, D), lambda i: (i // tiles_per_seq, 0, 0)),
            pl.BlockSpec((SUBLANES, D), const),
            pl.BlockSpec((D, 2 * LANES), const),
            pl.BlockSpec((SUBLANES, LANES), const),
            pl.BlockSpec((D, 2 * D_SHARED), const),
            pl.BlockSpec((D_SHARED, D), const),
            pl.BlockSpec((SUBLANES, D), const),
        ],
        out_specs=[
            pl.BlockSpec((tm, D_PACKED), lambda i: (i, 0)),
            pl.BlockSpec((tm, D), lambda i: (i, 0)),
            pl.BlockSpec((tm, LANES), lambda i: (i, 0)),
            pl.BlockSpec((SUBLANES, LANES), const),
        ],
        out_shape=[
            jax.ShapeDtypeStruct((N_TOK, D_PACKED), I32),
            jax.ShapeDtypeStruct((N_TOK, D), F32),
            jax.ShapeDtypeStruct((N_TOK, LANES), F32),
            jax.ShapeDtypeStruct((SUBLANES, LANES), F32),
        ],
        scratch_shapes=[pltpu.VMEM((SUBLANES, LANES), F32)],
        compiler_params=_cp(("arbitrary",)),
        name="moe_pre",
    )(x1, mod_l, nw8, wr, br, wgu, ws_down.astype(BF16), wsg8)


def _n_blocks_per_expert(cnt_row):
    return jnp.floor((cnt_row + (MOE_BLOCK - 1)) * (1.0 / MOE_BLOCK))


def _moe_pos_kernel(cnt_ref, route_ref, pos_ref, seg_ref):
    tm = route_ref.shape[0]
    nb = _n_blocks_per_expert(cnt_ref[...])
    r = lax.broadcasted_iota(I32, (LANES, LANES), 0)
    c = lax.broadcasted_iota(I32, (LANES, LANES), 1)
    excl = jnp.dot(nb, jnp.where(r < c, 1.0, 0.0), precision=HIGHEST, preferred_element_type=F32)
    pstart = excl[0:1] * MOE_BLOCK
    route = route_ref[...]
    lane = lax.broadcasted_iota(I32, (tm, LANES), 1).astype(F32)
    e1 = route[:, 0:1] + EXPERT_LANE0
    e2 = route[:, 1:2] + EXPERT_LANE0
    p1 = jnp.sum(jnp.where(lane == e1, pstart, 0.0), axis=-1, keepdims=True) + route[:, 4:5]
    p2 = jnp.sum(jnp.where(lane == e2, pstart, 0.0), axis=-1, keepdims=True) + route[:, 5:6]
    pos_ref[...] = _lane_pick(lane, [(0, p1), (1, p2)]).T[0:SUBLANES].astype(I32)

    @pl.when(pl.program_id(0) == 0)
    def _():
        sub = lax.broadcasted_iota(I32, (SUBLANES, LANES), 0)
        seg_ref[...] = jnp.where(sub == 0, nb, excl).astype(I32)


def _moe_pos(cnt, route):
    tm = TM_POS
    pos, seg = pl.pallas_call(
        _moe_pos_kernel,
        grid=(N_TOK // tm,),
        in_specs=[
            pl.BlockSpec((SUBLANES, LANES), lambda i: (0, 0)),
            pl.BlockSpec((tm, LANES), lambda i: (i, 0)),
        ],
        out_specs=[
            pl.BlockSpec((SUBLANES, tm), lambda i: (0, i)),
            pl.BlockSpec((SUBLANES, LANES), lambda i: (0, 0)),
        ],
        out_shape=[
            jax.ShapeDtypeStruct((SUBLANES, N_TOK), I32),
            jax.ShapeDtypeStruct((SUBLANES, LANES), I32),
        ],
        compiler_params=_cp(("arbitrary",)),
        name="moe_pos",
    )(cnt, route)
    experts = slice(EXPERT_LANE0, EXPERT_LANE0 + N_EXPERTS)
    return pos[:2].reshape(-1), seg[0, experts], seg[1, experts]


def _expert_kernel(nblk_ref, blk0_ref, buf_ref, wg_ref, wu_ref, wd_ref, eo_ref,
                   xin_ref, out_ref, wgb_ref, wub_ref, wdb_ref, sem_in, sem_out):
    e = pl.program_id(0)
    last = pl.num_programs(0) - 1
    n = nblk_ref[e]
    blk0 = blk0_ref[e]
    total = blk0_ref[last] + nblk_ref[last]
    bm = MOE_BLOCK
    n_in = xin_ref.shape[0]

    def rows(g):
        return pl.ds(pl.multiple_of(g * bm, bm), bm)

    def in_copy(g):
        slot = lax.rem(g, n_in)
        return pltpu.make_async_copy(buf_ref.at[rows(g)], xin_ref.at[slot], sem_in.at[slot])

    def out_copy(g):
        return pltpu.make_async_copy(out_ref.at[g & 1], eo_ref.at[rows(g)], sem_out.at[g & 1])

    @pl.when(n > 0)
    def _():
        wgb_ref[...] = wg_ref[...].astype(BF16)
        wub_ref[...] = wu_ref[...].astype(BF16)
        wdb_ref[...] = wd_ref[...].astype(BF16)

    def block(j, carry):
        g = blk0 + j

        @pl.when(g == 0)
        def _():
            in_copy(g).start()

            @pl.when(total > 1)
            def _():
                in_copy(g + 1).start()

        @pl.when(g + 2 < total)
        def _():
            in_copy(g + 2).start()

        in_copy(g).wait()

        @pl.when(g >= 2)
        def _():
            out_copy(g - 2).wait()

        xb = _unpack_bf16_pairs(xin_ref[lax.rem(g, n_in)]).astype(BF16)
        hid = _silu(_dot(xb, wgb_ref[...])) * _dot(xb, wub_ref[...])
        out_ref[g & 1] = _pack_bf16_pairs(_dot(hid.astype(BF16), wdb_ref[...]).astype(BF16))
        out_copy(g).start()
        return carry

    lax.fori_loop(0, n, block, 0)

    @pl.when((e == last) & (total >= 2))
    def _():
        out_copy(total - 2).wait()

    @pl.when((e == last) & (total >= 1))
    def _():
        out_copy(total - 1).wait()


def _moe_experts(nblk_e, blk0_e, buf, w_gate, w_up, w_down, layer):
    bm = MOE_BLOCK
    wspec = lambda shape: pl.BlockSpec((None, None) + shape, lambda e, nb, b0: (layer, e, 0, 0))
    return pl.pallas_call(
        _expert_kernel,
        grid_spec=pltpu.PrefetchScalarGridSpec(
            num_scalar_prefetch=2,
            grid=(N_EXPERTS,),
            in_specs=[
                pl.BlockSpec(memory_space=pl.ANY),
                wspec((D, D_EXPERT)),
                wspec((D, D_EXPERT)),
                wspec((D_EXPERT, D)),
            ],
            out_specs=pl.BlockSpec(memory_space=pl.ANY),
            scratch_shapes=[
                pltpu.VMEM((3, bm, D_PACKED), I32),
                pltpu.VMEM((2, bm, D_PACKED), I32),
                pltpu.VMEM((D, D_EXPERT), BF16),
                pltpu.VMEM((D, D_EXPERT), BF16),
                pltpu.VMEM((D_EXPERT, D), BF16),
                pltpu.SemaphoreType.DMA((3,)),
                pltpu.SemaphoreType.DMA((2,)),
            ],
        ),
        out_shape=jax.ShapeDtypeStruct((P_ROWS, D_PACKED), I32),
        compiler_params=_cp(("arbitrary",)),
        name="moe_experts",
    )(nblk_e, blk0_e, buf, w_gate, w_up, w_down)


SC_CORES = 2
SC_SUBCORES = 16
SC_WORKERS = SC_CORES * SC_SUBCORES
SC_CHUNK = 64


def _sc_gather_store(table_hbm, idx_v, out_hbm, out_base, n_rows, bufs, sems):
    n_chunks = n_rows // SC_CHUNK
    assert n_chunks % 2 == 0

    def gather(j, b):
        off = pl.multiple_of(j * SC_CHUNK, SC_CHUNK)
        return pltpu.make_async_copy(table_hbm.at[idx_v.at[pl.ds(off, SC_CHUNK)]], bufs[b], sems[b])

    gather(0, 0).start()

    @pl.loop(0, n_chunks, step=2)
    def _(j):
        for b in range(2):
            jj = j + b

            @pl.when(jj + 1 < n_chunks)
            def _():
                gather(jj + 1, 1 - b).start()

            gather(jj, b).wait()
            row0 = pl.multiple_of(out_base + jj * SC_CHUNK, SC_CHUNK)
            pltpu.sync_copy(bufs[b], out_hbm.at[pl.ds(row0, SC_CHUNK)])


def _sc_row_buffers(width, dtype):
    return [pltpu.VMEM((SC_CHUNK, width), dtype), pltpu.VMEM((SC_CHUNK, width), dtype),
            pltpu.SemaphoreType.DMA, pltpu.SemaphoreType.DMA]


def _sc_gather_rows(table, idx):
    n_idx = idx.shape[0]
    width = table.shape[1]
    per_w = n_idx // SC_WORKERS
    mesh = plsc.VectorSubcoreMesh(core_axis_name="c", subcore_axis_name="s")

    def body(table_hbm, idx_hbm, out_hbm, idx_v, buf0, buf1, sem0, sem1):
        wid = lax.axis_index("s") * SC_CORES + lax.axis_index("c")
        base = wid * per_w
        pltpu.sync_copy(idx_hbm.at[pl.ds(base, per_w)], idx_v)
        _sc_gather_store(table_hbm, idx_v, out_hbm, base, per_w, (buf0, buf1), (sem0, sem1))

    return pl.kernel(
        body,
        out_type=jax.ShapeDtypeStruct((n_idx, width), table.dtype),
        mesh=mesh,
        scratch_types=[pltpu.VMEM((per_w,), I32)] + _sc_row_buffers(width, table.dtype),
        name="sc_gather_rows",
    )(table, idx)


def _sc_dispatch_rows(h, pos_km):
    per_w = P_ROWS // SC_WORKERS
    lanes = 16
    mesh = plsc.VectorSubcoreMesh(core_axis_name="c", subcore_axis_name="s")

    def body(h_hbm, pos_hbm, buf_hbm, pos_v, tok_v, buf0, buf1, sem0, sem1):
        wid = lax.axis_index("s") * SC_CORES + lax.axis_index("c")
        base = wid * per_w
        pltpu.sync_copy(pos_hbm, pos_v)

        @pl.loop(0, per_w // lanes)
        def _(i):
            off = pl.multiple_of(i * lanes, lanes)
            tok_v[pl.ds(off, lanes)] = (base + off + lax.iota(I32, lanes)) & (N_TOK - 1)

        @pl.loop(0, N_SLOTS // lanes)
        def _(i):
            off = pl.multiple_of(i * lanes, lanes)
            local = pos_v[pl.ds(off, lanes)] - base
            mine = (local >= 0) & (local < per_w)
            slot = off + lax.iota(I32, lanes)
            plsc.store_scatter(tok_v, [jnp.where(mine, local, 0)], slot & (N_TOK - 1), mask=mine)

        _sc_gather_store(h_hbm, tok_v, buf_hbm, base, per_w, (buf0, buf1), (sem0, sem1))

    return pl.kernel(
        body,
        out_type=jax.ShapeDtypeStruct((P_ROWS, h.shape[1]), h.dtype),
        mesh=mesh,
        scratch_types=[pltpu.VMEM((N_SLOTS,), I32), pltpu.VMEM((per_w,), I32)]
        + _sc_row_buffers(h.shape[1], h.dtype),
        compiler_params=pltpu.CompilerParams(needs_layout_passes=False),
        name="sc_dispatch_rows",
    )(h, pos_km)


def _combine_kernel(y0_ref, y1_ref, x_ref, sh_ref, route_ref, mod_ref, nw_ref, o_ref):
    route = route_ref[...]
    y0 = _unpack_bf16_pairs(y0_ref[...])
    y1 = _unpack_bf16_pairs(y1_ref[...])
    y = route[:, 2:3] * y0 + route[:, 3:4] * y1 + sh_ref[...]
    mod = mod_ref[...]
    o_ref[...] = x_ref[...] + mod[5:6] * (_rms(y) * nw_ref[3:4])


def _moe_combine(pos_km, eo, x1, sh, route, mod_l, nw8):
    tm = TM_MIX
    tiles_per_seq = SEQ // tm
    n_tiles = N_TOK // tm
    y2 = _sc_gather_rows(eo, pos_km)
    return pl.pallas_call(
        _combine_kernel,
        grid=(n_tiles,),
        in_specs=[
            pl.BlockSpec((tm, D_PACKED), lambda i: (i, 0)),
            pl.BlockSpec((tm, D_PACKED), lambda i: (i + n_tiles, 0)),
            pl.BlockSpec((tm, D), lambda i: (i, 0)),
            pl.BlockSpec((tm, D), lambda i: (i, 0)),
            pl.BlockSpec((tm, LANES), lambda i: (i, 0)),
            pl.BlockSpec((None, SUBLANES, D), lambda i: (i // tiles_per_seq, 0, 0)),
            pl.BlockSpec((SUBLANES, D), lambda i: (0, 0)),
        ],
        out_specs=pl.BlockSpec((tm, D), lambda i: (i, 0)),
        out_shape=jax.ShapeDtypeStruct((N_TOK, D), F32),
        compiler_params=_cp(("arbitrary",)),
        name="moe_combine",
    )(y2, y2, x1, sh, route, mod_l, nw8)


def _hier_moe_block(x1, mod_l, nw8, layer, w_rg, b_rg, w_re, b_re, w_gate, w_up, w_down,
                    ws_gate, ws_up, ws_down, w_sg):
    h, sh, route, cnt = _moe_pre(x1, mod_l, nw8, w_rg, b_rg, w_re, b_re, ws_gate, ws_up, ws_down, w_sg)
    pos_km, nblk_e, blk0_e = _moe_pos(cnt, route)
    buf = _sc_dispatch_rows(h, pos_km)
    eo = _moe_experts(nblk_e, blk0_e, buf, w_gate, w_up, w_down, layer)
    return _moe_combine(pos_km, eo, x1, sh, route, mod_l, nw8)


def _gdn_pre_kernel(x_ref, mod_ref, nw_ref, wf_ref, wba_ref, cw_ref, misc_ref,
                    q_ref, k_ref, v_ref, z_ref, gb_ref, gt_ref, tail_ref, proj_ref, ba_ref, w_ref,
                    *, tiles_per_seq):
    i = pl.program_id(0)

    @pl.when(i == 0)
    def _():
        tail_ref[...] = jnp.zeros(tail_ref.shape, F32)
        proj_ref[1] = jnp.zeros(proj_ref.shape[1:], F32)
        ba_ref[1] = jnp.zeros(ba_ref.shape[1:], F32)
        w_ref[...] = wf_ref[...].astype(BF16)

    refs = (x_ref, mod_ref, nw_ref, w_ref, wba_ref, cw_ref, misc_ref,
            q_ref, k_ref, v_ref, z_ref, gb_ref, gt_ref, tail_ref, proj_ref, ba_ref)
    first_of_seq = (i - 1) % tiles_per_seq == 0

    @pl.when((i & 1) == 0)
    def _():
        _gdn_pre_step(*refs, cur=0, prev=1, first_of_seq=first_of_seq)

    @pl.when((i & 1) == 1)
    def _():
        _gdn_pre_step(*refs, cur=1, prev=0, first_of_seq=first_of_seq)


def _gdn_pre_step(x_ref, mod_ref, nw_ref, w_ref, wba_ref, cw_ref, misc_ref,
                  q_ref, k_ref, v_ref, z_ref, gb_ref, gt_ref, tail_ref, proj_ref, ba_ref,
                  *, cur, prev, first_of_seq):
    tm = x_ref.shape[0]
    qkv_w = 3 * D
    x = x_ref[...]
    mod = mod_ref[...]
    nw = nw_ref[...]
    h = _rms(x) * nw[0:1] * (1.0 + mod[1:2]) + mod[0:1]
    hb = h.astype(BF16)
    proj_ref[cur] = _dot(hb, w_ref[...])
    ba_ref[cur] = _dot(hb, wba_ref[...])

    z_ref[...] = proj_ref[prev, :, qkv_w:].astype(BF16)
    pre = proj_ref[prev, :, :qkv_w]
    p3, p2, p1 = _causal_taps(pre, tail_ref, (3, 2, 1), first_of_seq)
    cw = cw_ref[...]
    conv = cw[0:1] * p3 + cw[1:2] * p2 + cw[2:3] * p1 + cw[3:4] * pre
    act = _silu(conv)
    for hd in range(HEADS):
        lo = hd * HEAD_DIM
        qh = act[:, lo:lo + HEAD_DIM]
        kh = act[:, D + lo:D + lo + HEAD_DIM]
        qn = qh * lax.rsqrt(jnp.sum(qh * qh, axis=-1, keepdims=True) + EPS) * (HEAD_DIM ** -0.5)
        kn = kh * lax.rsqrt(jnp.sum(kh * kh, axis=-1, keepdims=True) + EPS)
        q_ref[:, lo:lo + HEAD_DIM] = qn.astype(BF16)
        k_ref[:, lo:lo + HEAD_DIM] = kn.astype(BF16)
    v_ref[...] = act[:, 2 * D:].astype(BF16)

    ba = ba_ref[prev]
    misc = misc_ref[...]
    beta = jax.nn.sigmoid(ba)
    sp_in = ba + misc[1:2]
    softplus = jnp.maximum(sp_in, 0.0) + jnp.log(1.0 + jnp.exp(-jnp.abs(sp_in)))
    g = -jnp.exp(misc[0:1]) * softplus
    row = lax.broadcasted_iota(I32, (tm, tm), 0)
    col = lax.broadcasted_iota(I32, (tm, tm), 1)
    tri = jnp.where((col <= row) & ((col >> CHUNK_SHIFT) == (row >> CHUNK_SHIFT)), 1.0, 0.0)
    gc = jnp.dot(tri, g, precision=HIGHEST, preferred_element_type=F32)
    lane = lax.broadcasted_iota(I32, (tm, LANES), 1)
    gb = jnp.where(lane < HEADS, beta, gc)
    gb_ref[...] = gb
    for c in range(tm // CHUNK):
        blk = jnp.concatenate([gb[c * CHUNK:(c + 1) * CHUNK], jnp.zeros((LANES - CHUNK, LANES), F32)], axis=0)
        gt_ref[c] = blk.T[HEADS:2 * HEADS, :]


def _gdn_pre(x2, mod_l, nw8, w_in, conv_w, a_log, dt_bias):
    tm = TM_GDN
    tiles_per_seq = SEQ // tm
    qkvz = 4 * D
    wba = jnp.pad(w_in[:, qkvz:], ((0, 0), (0, LANES - 2 * HEADS))).astype(BF16)
    cw8 = jnp.pad(conv_w, ((0, SUBLANES - conv_w.shape[0]), (0, 0)))
    misc = jnp.zeros((SUBLANES, LANES), F32)
    misc = misc.at[0, HEADS:2 * HEADS].set(a_log).at[1, HEADS:2 * HEADS].set(dt_bias)
    const = lambda i: (0, 0)
    n_tiles = N_TOK // tm
    src = lambda i: jnp.minimum(i, n_tiles - 1)
    dst = lambda i: jnp.maximum(i - 1, 0)
    tok = lambda i: (dst(i), 0)
    return pl.pallas_call(
        functools.partial(_gdn_pre_kernel, tiles_per_seq=tiles_per_seq),
        grid=(n_tiles + 1,),
        in_specs=[
            pl.BlockSpec((tm, D), lambda i: (src(i), 0)),
            pl.BlockSpec((None, SUBLANES, D), lambda i: (src(i) // tiles_per_seq, 0, 0)),
            pl.BlockSpec((SUBLANES, D), const),
            pl.BlockSpec((D, qkvz), const, pipeline_mode=pl.Buffered(1)),
            pl.BlockSpec((D, LANES), const),
            pl.BlockSpec((SUBLANES, 3 * D), const),
            pl.BlockSpec((SUBLANES, LANES), const),
        ],
        out_specs=[
            pl.BlockSpec((tm, D), tok),
            pl.BlockSpec((tm, D), tok),
            pl.BlockSpec((tm, D), tok),
            pl.BlockSpec((tm, D), tok),
            pl.BlockSpec((tm, LANES), tok),
            pl.BlockSpec((tm // CHUNK, HEADS, LANES), lambda i: (dst(i), 0, 0)),
        ],
        out_shape=[
            jax.ShapeDtypeStruct((N_TOK, D), BF16),
            jax.ShapeDtypeStruct((N_TOK, D), BF16),
            jax.ShapeDtypeStruct((N_TOK, D), BF16),
            jax.ShapeDtypeStruct((N_TOK, D), BF16),
            jax.ShapeDtypeStruct((N_TOK, LANES), F32),
            jax.ShapeDtypeStruct((N_TOK // CHUNK, HEADS, LANES), F32),
        ],
        scratch_shapes=[pltpu.VMEM((SUBLANES, 3 * D), F32),
                        pltpu.VMEM((2, tm, qkvz), F32),
                        pltpu.VMEM((2, tm, LANES), F32),
                        pltpu.VMEM((D, qkvz), BF16)],
        compiler_params=_cp(("arbitrary",)),
        name="gdn_pre",
    )(x2, mod_l, nw8, w_in, wba, cw8, misc)


def _dot_nt(a, b):
    return lax.dot_general(a, b, (((1,), (1,)), ((), ())), preferred_element_type=F32)


def _dot_tn(a, b):
    return lax.dot_general(a, b, (((0,), (0,)), ((), ())), preferred_element_type=F32)


def _gdn_chunk_kernel(q_ref, k_ref, v_ref, gb_ref, gt_ref, o_ref, s_ref):
    @pl.when(pl.program_id(1) == 0)
    def _():
        s_ref[...] = jnp.zeros(s_ref.shape, F32)

    c = CHUNK
    nb = q_ref.shape[0]
    row = lax.broadcasted_iota(I32, (c, c), 0)
    col = lax.broadcasted_iota(I32, (c, c), 1)
    causal = col <= row
    strict = col < row
    eye = jnp.where(col == row, 1.0, 0.0)
    chains = [(b, hd) for b in range(nb) for hd in range(HEADS)]
    st = []
    for b, hd in chains:
        lo = hd * HEAD_DIM
        gb = gb_ref[b]
        q = q_ref[b, :, lo:lo + HEAD_DIM].astype(F32)
        k = k_ref[b, :, lo:lo + HEAD_DIM].astype(F32)
        v = v_ref[b, :, lo:lo + HEAD_DIM].astype(F32)
        beta = gb[:, hd:hd + 1]
        gcol = gb[:, HEADS + hd:HEADS + hd + 1]
        grow = gt_ref[b, hd:hd + 1, 0:c]
        glast = gcol[c - 1:c, :]
        egc = jnp.exp(gcol)
        kb = k * beta
        st.append(dict(
            decay=jnp.exp(jnp.where(causal, gcol - grow, -jnp.inf)),
            kq=jnp.concatenate([kb, q], axis=0).astype(BF16),
            kbf=k.astype(BF16),
            rhs=jnp.concatenate([v * beta, kb * egc], axis=-1).astype(BF16),
            qd=(q * egc).astype(BF16),
            kd=(k * jnp.exp(glast - gcol)).astype(BF16),
            eg=jnp.exp(glast)))
    for x in st:
        kk = _dot_nt(x["kq"], x["kbf"])
        x["p"] = jnp.where(strict, kk[:c] * x["decay"], 0.0)
        x["attn"] = jnp.where(causal, kk[c:] * x["decay"], 0.0).astype(BF16)
        x["t"] = eye - x["p"]
    for _ in range(5):
        for x in st:
            pb = x["p"].astype(BF16)
            x["p"] = _dot(pb, pb)
        for x in st:
            x["t"] = x["t"] + _dot(x["t"].astype(BF16), x["p"].astype(BF16))
    for x in st:
        x["uw"] = _dot(x["t"].astype(BF16), x["rhs"])
    for x, (b, hd) in zip(st, chains):
        s = s_ref[b * HEADS + hd]
        x["s"] = s
        ws = _dot(jnp.concatenate([x["uw"][:, HEAD_DIM:].astype(BF16), x["qd"]], axis=0), s.astype(BF16))
        x["vb"] = (x["uw"][:, :HEAD_DIM] - ws[:c]).astype(BF16)
        x["o"] = ws[c:]
    for x, (b, hd) in zip(st, chains):
        lo = hd * HEAD_DIM
        s_ref[b * HEADS + hd] = x["s"] * x["eg"] + _dot_tn(x["kd"], x["vb"])
        o_ref[b, :, lo:lo + HEAD_DIM] = (x["o"] + _dot(x["attn"], x["vb"])).astype(BF16)


def _gdn_chunks(q, k, v, gb, gt):
    nb = GDN_BATCH_PER_STEP
    n_chunks = SEQ // CHUNK
    tok = lambda b, c: (b, c, 0)
    o = pl.pallas_call(
        _gdn_chunk_kernel,
        grid=(BATCH // nb, n_chunks),
        in_specs=[
            pl.BlockSpec((nb, CHUNK, D), tok),
            pl.BlockSpec((nb, CHUNK, D), tok),
            pl.BlockSpec((nb, CHUNK, D), tok),
            pl.BlockSpec((nb, CHUNK, LANES), tok),
            pl.BlockSpec((nb, None, HEADS, LANES), lambda b, c: (b, c, 0, 0)),
        ],
        out_specs=pl.BlockSpec((nb, CHUNK, D), tok),
        out_shape=jax.ShapeDtypeStruct((BATCH, SEQ, D), BF16),
        scratch_shapes=[pltpu.VMEM((nb * HEADS, HEAD_DIM, HEAD_DIM), F32)],
        compiler_params=_cp(("arbitrary", "arbitrary")),
        name="gdn_chunks",
    )(q.reshape(BATCH, SEQ, D), k.reshape(BATCH, SEQ, D), v.reshape(BATCH, SEQ, D),
      gb.reshape(BATCH, SEQ, LANES), gt.reshape(BATCH, n_chunks, HEADS, LANES))
    return o.reshape(N_TOK, D)


def _gdn_post_kernel(o_ref, z_ref, x_ref, mod_ref, nw_ref, gnw_ref, wout_ref, out_ref):
    gnw = gnw_ref[0:1]
    parts = []
    for hd in range(HEADS):
        lo = hd * HEAD_DIM
        oh = o_ref[:, lo:lo + HEAD_DIM].astype(F32)
        zh = z_ref[:, lo:lo + HEAD_DIM].astype(F32)
        parts.append((_rms(oh) * gnw * _silu(zh)).astype(BF16))
    y = _dot(jnp.concatenate(parts, axis=-1), wout_ref[...])
    mod = mod_ref[...]
    out_ref[...] = x_ref[...] + mod[2:3] * (_rms(y) * nw_ref[1:2])


def _gdn_post(o, z, x2, mod_l, nw8, gdn_norm_w, w_out):
    tm = TM_MIX
    tiles_per_seq = SEQ // tm
    gnw8 = jnp.pad(gdn_norm_w.reshape(1, HEAD_DIM), ((0, SUBLANES - 1), (0, 0)))
    tok = lambda i: (i, 0)
    const = lambda i: (0, 0)
    return pl.pallas_call(
        _gdn_post_kernel,
        grid=(N_TOK // tm,),
        in_specs=[
            pl.BlockSpec((tm, D), tok),
            pl.BlockSpec((tm, D), tok),
            pl.BlockSpec((tm, D), tok),
            pl.BlockSpec((None, SUBLANES, D), lambda i: (i // tiles_per_seq, 0, 0)),
            pl.BlockSpec((SUBLANES, D), const),
            pl.BlockSpec((SUBLANES, HEAD_DIM), const),
            pl.BlockSpec((D, D), const),
        ],
        out_specs=pl.BlockSpec((tm, D), tok),
        out_shape=jax.ShapeDtypeStruct((N_TOK, D), F32),
        compiler_params=_cp(("arbitrary",)),
        name="gdn_post",
    )(o, z, x2, mod_l, nw8, gnw8, w_out.astype(BF16))


def kernel(x, c, ada_w, ada_b, norm_w, conv_in_w, conv_w, conv_out_w, gdn_in_w, gdn_conv_w, gdn_a_log,
           gdn_dt_bias, gdn_norm_w, gdn_out_w, moe_group_w, moe_group_b, moe_expert_w, moe_expert_b,
           moe_w_gate, moe_w_up, moe_w_down, shared_w_gate, shared_w_up, shared_w_down, shared_gate_w):
    mod = _ada_mod(c, ada_w, ada_b)
    nw8 = jnp.pad(norm_w, ((0, 0), (0, SUBLANES - norm_w.shape[1]), (0, 0)))
    x2 = x.reshape(N_TOK, D)

    def moe(x1, layer):
        return _hier_moe_block(
            x1, mod[layer], nw8[layer], layer,
            moe_group_w[layer], moe_group_b[layer], moe_expert_w[layer], moe_expert_b[layer],
            moe_w_gate, moe_w_up, moe_w_down,
            shared_w_gate[layer], shared_w_up[layer], shared_w_down[layer], shared_gate_w[layer])

    x2 = _conv_mixer(x2, mod[0], nw8[0], conv_in_w[0], conv_w[0], conv_out_w[0])
    x2 = moe(x2, 0)
    q, k, v, z, gb, gt = _gdn_pre(x2, mod[1], nw8[1], gdn_in_w[0], gdn_conv_w[0], gdn_a_log[0], gdn_dt_bias[0])
    o = _gdn_chunks(q, k, v, gb, gt)
    x2 = _gdn_post(o, z, x2, mod[1], nw8[1], gdn_norm_w[0], gdn_out_w[0])
    x2 = moe(x2, 1)
    return x2.reshape(BATCH, SEQ, D)
```

```python
import functools

import jax
import jax.numpy as jnp
from jax import lax
from jax.experimental import pallas as pl
from jax.experimental.pallas import tpu as pltpu
from jax.experimental.pallas import tpu_sc as plsc

F32 = jnp.float32
BF16 = jnp.bfloat16
I32 = jnp.int32
HIGHEST = lax.Precision.HIGHEST

D = 1024
D_PACKED = D // 2
BATCH = 4
SEQ = 4096
N_TOK = BATCH * SEQ
HEADS = 8
HEAD_DIM = 128
CHUNK = 64
CHUNK_SHIFT = 6
N_GROUPS = 8
N_EXPERTS = 64
D_EXPERT = 256
D_SHARED = 512
EPS = 1e-6
LANES = 128
SUBLANES = 8
EXPERT_LANE0 = N_GROUPS

MOE_BLOCK = 256
N_SLOTS = N_TOK * 2
N_BLOCKS = N_SLOTS // MOE_BLOCK + N_EXPERTS
P_ROWS = N_BLOCKS * MOE_BLOCK

TM_MIX = 512
TM_GDN = 256
TM_DMA = 256
TM_POS = 2048
GDN_ROW_CHUNK = 64
GDN_BATCH_PER_STEP = 4
VMEM_LIMIT = 56 * 1024 * 1024


def _cp(sem):
    return pltpu.CompilerParams(dimension_semantics=sem, vmem_limit_bytes=VMEM_LIMIT)


def _rms(x):
    return x * lax.rsqrt(jnp.mean(x * x, axis=-1, keepdims=True) + EPS)


def _silu(x):
    return x * jax.nn.sigmoid(x)


def _dot(a, b):
    return jnp.dot(a, b, preferred_element_type=F32)


U32 = jnp.uint32
HI16 = 0xFFFF0000


def _pack_bf16_pairs(xb):
    half = xb.shape[1] // 2
    bits = lax.bitcast_convert_type(xb.astype(F32), U32)
    return lax.bitcast_convert_type(bits[:, :half] | (bits[:, half:] >> 16), I32)


def _unpack_bf16_pairs(words):
    bits = lax.bitcast_convert_type(words, U32)
    hi = lax.bitcast_convert_type(bits & jnp.uint32(HI16), F32)
    lo = lax.bitcast_convert_type(bits << 16, F32)
    return jnp.concatenate([hi, lo], axis=1)


def _shift_rows(x, tail, s):
    sub = lax.broadcasted_iota(I32, tail.shape, 0)
    rolled = pltpu.roll(x, s, axis=0)
    head = jnp.where(sub < s, pltpu.roll(tail, s, axis=0), rolled[0:SUBLANES])
    return jnp.concatenate([head, rolled[SUBLANES:]], axis=0)


def _causal_taps(x, tail_ref, taps, first_of_seq):
    tail = jnp.where(first_of_seq, 0.0, tail_ref[...])
    sub = lax.broadcasted_iota(I32, tail.shape, 0)
    out = []
    for s in taps:
        rolled = pltpu.roll(x, s, axis=0)
        head = jnp.where(sub < s, pltpu.roll(tail, s, axis=0), rolled[0:SUBLANES])
        out.append(jnp.concatenate([head, rolled[SUBLANES:]], axis=0))
    tail_ref[...] = x[x.shape[0] - SUBLANES:]
    return out


def _ada_kernel(c_ref, w_ref, b_ref, o_ref):
    cs = _silu(c_ref[...])
    o_ref[0] = _dot(cs.astype(BF16), w_ref[0].astype(BF16)) + b_ref[0]


def _ada_mod(c, ada_w, ada_b):
    depth = ada_w.shape[0]
    tn = 1024
    c8 = jnp.pad(c, ((0, SUBLANES - BATCH), (0, 0)))
    mod = pl.pallas_call(
        _ada_kernel,
        grid=(depth, 6 * D // tn),
        in_specs=[
            pl.BlockSpec((SUBLANES, D), lambda l, j: (0, 0)),
            pl.BlockSpec((1, D, tn), lambda l, j: (l, 0, j)),
            pl.BlockSpec((1, 1, tn), lambda l, j: (l, 0, j)),
        ],
        out_specs=pl.BlockSpec((1, SUBLANES, tn), lambda l, j: (l, 0, j)),
        out_shape=jax.ShapeDtypeStruct((depth, SUBLANES, 6 * D), F32),
        compiler_params=_cp(("arbitrary", "arbitrary")),
        name="ada_mod",
    )(c8, ada_w, ada_b.reshape(depth, 1, 6 * D))
    mod = mod[:, :BATCH].reshape(depth, BATCH, 6, D)
    return jnp.pad(mod, ((0, 0), (0, 0), (0, 2), (0, 0)))


def _conv_mixer_kernel(x_ref, mod_ref, nw_ref, win_ref, cw_ref, wout_ref, o_ref, tail_ref, *, tiles_per_seq):
    @pl.when(pl.program_id(0) == 0)
    def _():
        tail_ref[...] = jnp.zeros(tail_ref.shape, F32)

    x = x_ref[...]
    mod = mod_ref[...]
    nw = nw_ref[...]
    h = _rms(x) * nw[0:1] * (1.0 + mod[1:2]) + mod[0:1]
    bcx = _dot(h.astype(BF16), win_ref[...])
    u = bcx[:, D:2 * D] * bcx[:, 2 * D:]
    u2, u1 = _causal_taps(u, tail_ref, (2, 1), pl.program_id(0) % tiles_per_seq == 0)
    cw = cw_ref[...]
    conv = cw[0:1] * u2 + cw[1:2] * u1 + cw[2:3] * u
    y = _dot((bcx[:, :D] * conv).astype(BF16), wout_ref[...])
    o_ref[...] = x + mod[2:3] * (_rms(y) * nw[1:2])


def _conv_mixer(x2, mod_l, nw8, w_in, conv_w, w_out):
    tm = TM_MIX
    tiles_per_seq = SEQ // tm
    cw8 = jnp.pad(conv_w, ((0, SUBLANES - conv_w.shape[0]), (0, 0)))
    return pl.pallas_call(
        functools.partial(_conv_mixer_kernel, tiles_per_seq=tiles_per_seq),
        grid=(N_TOK // tm,),
        in_specs=[
            pl.BlockSpec((tm, D), lambda i: (i, 0)),
            pl.BlockSpec((None, SUBLANES, D), lambda i: (i // tiles_per_seq, 0, 0)),
            pl.BlockSpec((SUBLANES, D), lambda i: (0, 0)),
            pl.BlockSpec((D, 3 * D), lambda i: (0, 0)),
            pl.BlockSpec((SUBLANES, D), lambda i: (0, 0)),
            pl.BlockSpec((D, D), lambda i: (0, 0)),
        ],
        out_specs=pl.BlockSpec((tm, D), lambda i: (i, 0)),
        out_shape=jax.ShapeDtypeStruct((N_TOK, D), F32),
        scratch_shapes=[pltpu.VMEM((SUBLANES, D), F32)],
        compiler_params=_cp(("arbitrary",)),
        name="conv_mixer",
    )(x2, mod_l, nw8, w_in.astype(BF16), cw8, w_out.astype(BF16))


def _lane_pick(lane, mask_val_pairs):
    out = jnp.zeros(lane.shape, F32)
    for idx, val in mask_val_pairs:
        out = jnp.where(lane == float(idx), val, out)
    return out


def _moe_route_kernel(x_ref, mod_ref, nw_ref, wr_ref, br_ref, wsg_ref,
                      h_ref, route_ref, cnt_ref, carry_ref):
    tm = x_ref.shape[0]

    @pl.when(pl.program_id(0) == 0)
    def _():
        carry_ref[...] = jnp.zeros(carry_ref.shape, F32)

    x = x_ref[...]
    mod = mod_ref[...]
    nw = nw_ref[...]
    h = _rms(x) * nw[2:3] * (1.0 + mod[4:5]) + mod[3:4]
    hb = h.astype(BF16)
    h_ref[...] = _pack_bf16_pairs(hb)
    h_lo = (h - hb.astype(F32)).astype(BF16)
    hw = _dot(hb, wr_ref[...]) + _dot(h_lo, wr_ref[...])
    logits = hw[:, :LANES] + hw[:, LANES:] + br_ref[0:1]
    sg = jax.nn.sigmoid(jnp.sum(h * wsg_ref[0:1], axis=-1, keepdims=True))
    lane_i = lax.broadcasted_iota(I32, (tm, LANES), 1)
    lane = lane_i.astype(F32)
    neg = jnp.float32(-jnp.inf)
    is_group = lane_i < N_GROUPS
    gl = jnp.where(is_group, logits, neg)
    gmax = jnp.max(gl, axis=-1, keepdims=True)
    gsel = jnp.min(jnp.where(gl == gmax, lane, float(LANES)), axis=-1, keepdims=True)
    psel = 1.0 / jnp.sum(jnp.where(is_group, jnp.exp(logits - gmax), 0.0), axis=-1, keepdims=True)
    lane_group = ((lane_i - EXPERT_LANE0) >> 3).astype(F32)
    in_group = (lane_i >= EXPERT_LANE0) & (lane_i < EXPERT_LANE0 + N_EXPERTS) & (lane_group == gsel)
    el = jnp.where(in_group, logits, neg)
    v1 = jnp.max(el, axis=-1, keepdims=True)
    i1 = jnp.min(jnp.where(el == v1, lane, float(LANES)), axis=-1, keepdims=True)
    el2 = jnp.where(lane == i1, neg, el)
    v2 = jnp.max(el2, axis=-1, keepdims=True)
    i2 = jnp.min(jnp.where(el2 == v2, lane, float(LANES)), axis=-1, keepdims=True)
    e2 = jnp.exp(v2 - v1)
    g1 = psel / (1.0 + e2)
    g2 = psel * e2 / (1.0 + e2)

    oh1 = lane == i1
    oh2 = lane == i2
    cnt = jnp.where(oh1 | oh2, 1.0, 0.0).astype(BF16)
    row = lax.broadcasted_iota(I32, (tm, tm), 0)
    col = lax.broadcasted_iota(I32, (tm, tm), 1)
    tri = jnp.where(col < row, 1.0, 0.0).astype(BF16)
    pre = _dot(tri, cnt) + carry_ref[0:1]
    r1 = jnp.sum(jnp.where(oh1, pre, 0.0), axis=-1, keepdims=True)
    r2 = jnp.sum(jnp.where(oh2, pre, 0.0), axis=-1, keepdims=True)
    carry_ref[...] = carry_ref[...] + jnp.sum(cnt.astype(F32), axis=0, keepdims=True)
    cnt_ref[...] = carry_ref[...]
    route_ref[...] = _lane_pick(lane, [
        (0, i1 - EXPERT_LANE0), (1, i2 - EXPERT_LANE0),
        (2, g1), (3, g2), (4, r1), (5, r2), (6, sg)])


def _moe_route(x1, mod_l, nw8, w_rg, b_rg, w_re, b_re, w_sg):
    tm = TM_MIX
    tiles_per_seq = SEQ // tm
    pad_l = LANES - N_GROUPS - N_EXPERTS
    wr = jnp.pad(jnp.concatenate([w_rg, w_re], axis=1), ((0, 0), (0, pad_l)))
    wr_hi = wr.astype(BF16)
    wr = jnp.concatenate([wr_hi, (wr - wr_hi.astype(F32)).astype(BF16)], axis=1)
    br = jnp.pad(jnp.concatenate([b_rg, b_re])[None, :], ((0, SUBLANES - 1), (0, pad_l)))
    wsg8 = jnp.pad(w_sg.reshape(1, D), ((0, SUBLANES - 1), (0, 0)))
    const = lambda i: (0, 0)
    return pl.pallas_call(
        _moe_route_kernel,
        grid=(N_TOK // tm,),
        in_specs=[
            pl.BlockSpec((tm, D), lambda i: (i, 0)),
            pl.BlockSpec((None, SUBLANES, D), lambda i: (i // tiles_per_seq, 0, 0)),
            pl.BlockSpec((SUBLANES, D), const),
            pl.BlockSpec((D, 2 * LANES), const),
            pl.BlockSpec((SUBLANES, LANES), const),
            pl.BlockSpec((SUBLANES, D), const),
        ],
        out_specs=[
            pl.BlockSpec((tm, D_PACKED), lambda i: (i, 0)),
            pl.BlockSpec((tm, LANES), lambda i: (i, 0)),
            pl.BlockSpec((SUBLANES, LANES), const),
        ],
        out_shape=[
            jax.ShapeDtypeStruct((N_TOK, D_PACKED), I32),
            jax.ShapeDtypeStruct((N_TOK, LANES), F32),
            jax.ShapeDtypeStruct((SUBLANES, LANES), F32),
        ],
        scratch_shapes=[pltpu.VMEM((SUBLANES, LANES), F32)],
        compiler_params=_cp(("arbitrary",)),
        name="moe_route",
    )(x1, mod_l, nw8, wr, br, wsg8)


def _moe_shared_kernel(h_ref, route_ref, wgu_ref, wd_ref, sh_ref):
    hb = _unpack_bf16_pairs(h_ref[...]).astype(BF16)
    gu = _dot(hb, wgu_ref[...])
    hid = _silu(gu[:, :D_SHARED]) * gu[:, D_SHARED:]
    ys = _dot(hid.astype(BF16), wd_ref[...])
    sh_ref[...] = _pack_bf16_pairs((route_ref[:, 6:7] * ys).astype(BF16))


def _moe_shared(hp, route, ws_gate, ws_up, ws_down):
    tm = TM_MIX
    const = lambda i: (0, 0)
    wgu = jnp.concatenate([ws_gate, ws_up], axis=1).astype(BF16)
    return pl.pallas_call(
        _moe_shared_kernel,
        grid=(N_TOK // tm,),
        in_specs=[
            pl.BlockSpec((tm, D_PACKED), lambda i: (i, 0)),
            pl.BlockSpec((tm, LANES), lambda i: (i, 0)),
            pl.BlockSpec((D, 2 * D_SHARED), const),
            pl.BlockSpec((D_SHARED, D), const),
        ],
        out_specs=pl.BlockSpec((tm, D_PACKED), lambda i: (i, 0)),
        out_shape=jax.ShapeDtypeStruct((N_TOK, D_PACKED), I32),
        compiler_params=_cp(("arbitrary",)),
        name="moe_shared",
    )(hp, route, wgu, ws_down.astype(BF16))


def _n_blocks_per_expert(cnt_row):
    return jnp.floor((cnt_row + (MOE_BLOCK - 1)) * (1.0 / MOE_BLOCK))


def _moe_pos_kernel(cnt_ref, route_ref, pos_ref, seg_ref):
    tm = route_ref.shape[0]
    nb = _n_blocks_per_expert(cnt_ref[...])
    r = lax.broadcasted_iota(I32, (LANES, LANES), 0)
    c = lax.broadcasted_iota(I32, (LANES, LANES), 1)
    excl = jnp.dot(nb, jnp.where(r < c, 1.0, 0.0), precision=HIGHEST, preferred_element_type=F32)
    pstart = excl[0:1] * MOE_BLOCK
    route = route_ref[...]
    lane = lax.broadcasted_iota(I32, (tm, LANES), 1).astype(F32)
    e1 = route[:, 0:1] + EXPERT_LANE0
    e2 = route[:, 1:2] + EXPERT_LANE0
    p1 = jnp.sum(jnp.where(lane == e1, pstart, 0.0), axis=-1, keepdims=True) + route[:, 4:5]
    p2 = jnp.sum(jnp.where(lane == e2, pstart, 0.0), axis=-1, keepdims=True) + route[:, 5:6]
    pos_ref[...] = _lane_pick(lane, [(0, p1), (1, p2)]).T[0:SUBLANES].astype(I32)

    @pl.when(pl.program_id(0) == 0)
    def _():
        sub = lax.broadcasted_iota(I32, (SUBLANES, LANES), 0)
        seg_ref[...] = jnp.where(sub == 0, nb, excl).astype(I32)


def _moe_pos(cnt, route):
    tm = TM_POS
    pos, seg = pl.pallas_call(
        _moe_pos_kernel,
        grid=(N_TOK // tm,),
        in_specs=[
            pl.BlockSpec((SUBLANES, LANES), lambda i: (0, 0)),
            pl.BlockSpec((tm, LANES), lambda i: (i, 0)),
        ],
        out_specs=[
            pl.BlockSpec((SUBLANES, tm), lambda i: (0, i)),
            pl.BlockSpec((SUBLANES, LANES), lambda i: (0, 0)),
        ],
        out_shape=[
            jax.ShapeDtypeStruct((SUBLANES, N_TOK), I32),
            jax.ShapeDtypeStruct((SUBLANES, LANES), I32),
        ],
        compiler_params=_cp(("arbitrary",)),
        name="moe_pos",
    )(cnt, route)
    experts = slice(EXPERT_LANE0, EXPERT_LANE0 + N_EXPERTS)
    return pos[:2].reshape(-1), seg[0, experts], seg[1, experts]


def _expert_kernel(nblk_ref, blk0_ref, buf_ref, wg_ref, wu_ref, wd_ref, eo_ref,
                   xin_ref, out_ref, wgb_ref, wub_ref, wdb_ref, sem_in, sem_out):
    e = pl.program_id(0)
    last = pl.num_programs(0) - 1
    n = nblk_ref[e]
    blk0 = blk0_ref[e]
    total = blk0_ref[last] + nblk_ref[last]
    bm = MOE_BLOCK
    n_in = xin_ref.shape[0]

    def rows(g):
        return pl.ds(pl.multiple_of(g * bm, bm), bm)

    def in_copy(g):
        slot = lax.rem(g, n_in)
        return pltpu.make_async_copy(buf_ref.at[rows(g)], xin_ref.at[slot], sem_in.at[slot])

    def out_copy(g):
        return pltpu.make_async_copy(out_ref.at[g & 1], eo_ref.at[rows(g)], sem_out.at[g & 1])

    @pl.when(n > 0)
    def _():
        wgb_ref[...] = wg_ref[...].astype(BF16)
        wub_ref[...] = wu_ref[...].astype(BF16)
        wdb_ref[...] = wd_ref[...].astype(BF16)

    def block(j, carry):
        g = blk0 + j

        @pl.when(g == 0)
        def _():
            in_copy(g).start()

            @pl.when(total > 1)
            def _():
                in_copy(g + 1).start()

        @pl.when(g + 2 < total)
        def _():
            in_copy(g + 2).start()

        in_copy(g).wait()

        @pl.when(g >= 2)
        def _():
            out_copy(g - 2).wait()

        xb = _unpack_bf16_pairs(xin_ref[lax.rem(g, n_in)]).astype(BF16)
        hid = _silu(_dot(xb, wgb_ref[...])) * _dot(xb, wub_ref[...])
        out_ref[g & 1] = _pack_bf16_pairs(_dot(hid.astype(BF16), wdb_ref[...]).astype(BF16))
        out_copy(g).start()
        return carry

    lax.fori_loop(0, n, block, 0)

    @pl.when((e == last) & (total >= 2))
    def _():
        out_copy(total - 2).wait()

    @pl.when((e == last) & (total >= 1))
    def _():
        out_copy(total - 1).wait()


def _moe_experts(nblk_e, blk0_e, buf, w_gate, w_up, w_down, layer):
    bm = MOE_BLOCK
    wspec = lambda shape: pl.BlockSpec((None, None) + shape, lambda e, nb, b0: (layer, e, 0, 0))
    return pl.pallas_call(
        _expert_kernel,
        grid_spec=pltpu.PrefetchScalarGridSpec(
            num_scalar_prefetch=2,
            grid=(N_EXPERTS,),
            in_specs=[
                pl.BlockSpec(memory_space=pl.ANY),
                wspec((D, D_EXPERT)),
                wspec((D, D_EXPERT)),
                wspec((D_EXPERT, D)),
            ],
            out_specs=pl.BlockSpec(memory_space=pl.ANY),
            scratch_shapes=[
                pltpu.VMEM((3, bm, D_PACKED), I32),
                pltpu.VMEM((2, bm, D_PACKED), I32),
                pltpu.VMEM((D, D_EXPERT), BF16),
                pltpu.VMEM((D, D_EXPERT), BF16),
                pltpu.VMEM((D_EXPERT, D), BF16),
                pltpu.SemaphoreType.DMA((3,)),
                pltpu.SemaphoreType.DMA((2,)),
            ],
        ),
        out_shape=jax.ShapeDtypeStruct((P_ROWS, D_PACKED), I32),
        compiler_params=_cp(("arbitrary",)),
        name="moe_experts",
    )(nblk_e, blk0_e, buf, w_gate, w_up, w_down)


SC_CORES = 2
SC_SUBCORES = 16
SC_WORKERS = SC_CORES * SC_SUBCORES
SC_CHUNK = 64


def _sc_gather_store(table_hbm, idx_v, out_hbm, out_base, n_rows, bufs, sems):
    n_chunks = n_rows // SC_CHUNK
    assert n_chunks % 2 == 0

    def gather(j, b):
        off = pl.multiple_of(j * SC_CHUNK, SC_CHUNK)
        return pltpu.make_async_copy(table_hbm.at[idx_v.at[pl.ds(off, SC_CHUNK)]], bufs[b], sems[b])

    gather(0, 0).start()

    @pl.loop(0, n_chunks, step=2)
    def _(j):
        for b in range(2):
            jj = j + b

            @pl.when(jj + 1 < n_chunks)
            def _():
                gather(jj + 1, 1 - b).start()

            gather(jj, b).wait()
            row0 = pl.multiple_of(out_base + jj * SC_CHUNK, SC_CHUNK)
            pltpu.sync_copy(bufs[b], out_hbm.at[pl.ds(row0, SC_CHUNK)])


def _sc_row_buffers(width, dtype):
    return [pltpu.VMEM((SC_CHUNK, width), dtype), pltpu.VMEM((SC_CHUNK, width), dtype),
            pltpu.SemaphoreType.DMA, pltpu.SemaphoreType.DMA]


def _sc_gather_rows(table, idx):
    n_idx = idx.shape[0]
    width = table.shape[1]
    per_w = n_idx // SC_WORKERS
    mesh = plsc.VectorSubcoreMesh(core_axis_name="c", subcore_axis_name="s")

    def body(table_hbm, idx_hbm, out_hbm, idx_v, buf0, buf1, sem0, sem1):
        wid = lax.axis_index("s") * SC_CORES + lax.axis_index("c")
        base = wid * per_w
        pltpu.sync_copy(idx_hbm.at[pl.ds(base, per_w)], idx_v)
        _sc_gather_store(table_hbm, idx_v, out_hbm, base, per_w, (buf0, buf1), (sem0, sem1))

    return pl.kernel(
        body,
        out_type=jax.ShapeDtypeStruct((n_idx, width), table.dtype),
        mesh=mesh,
        scratch_types=[pltpu.VMEM((per_w,), I32)] + _sc_row_buffers(width, table.dtype),
        name="sc_gather_rows",
    )(table, idx)


def _sc_dispatch_rows(h, pos_km):
    per_w = P_ROWS // SC_WORKERS
    lanes = 16
    mesh = plsc.VectorSubcoreMesh(core_axis_name="c", subcore_axis_name="s")

    def body(h_hbm, pos_hbm, buf_hbm, pos_v, tok_v, buf0, buf1, sem0, sem1):
        wid = lax.axis_index("s") * SC_CORES + lax.axis_index("c")
        base = wid * per_w
        pltpu.sync_copy(pos_hbm, pos_v)

        @pl.loop(0, per_w // lanes)
        def _(i):
            off = pl.multiple_of(i * lanes, lanes)
            tok_v[pl.ds(off, lanes)] = (base + off + lax.iota(I32, lanes)) & (N_TOK - 1)

        @pl.loop(0, N_SLOTS // lanes)
        def _(i):
            off = pl.multiple_of(i * lanes, lanes)
            local = pos_v[pl.ds(off, lanes)] - base
            mine = (local >= 0) & (local < per_w)
            slot = off + lax.iota(I32, lanes)
            plsc.store_scatter(tok_v, [jnp.where(mine, local, 0)], slot & (N_TOK - 1), mask=mine)

        _sc_gather_store(h_hbm, tok_v, buf_hbm, base, per_w, (buf0, buf1), (sem0, sem1))

    return pl.kernel(
        body,
        out_type=jax.ShapeDtypeStruct((P_ROWS, h.shape[1]), h.dtype),
        mesh=mesh,
        scratch_types=[pltpu.VMEM((N_SLOTS,), I32), pltpu.VMEM((per_w,), I32)]
        + _sc_row_buffers(h.shape[1], h.dtype),
        compiler_params=pltpu.CompilerParams(needs_layout_passes=False),
        name="sc_dispatch_rows",
    )(h, pos_km)


def _combine_kernel(y0_ref, y1_ref, x_ref, sh_ref, route_ref, mod_ref, nw_ref, o_ref):
    route = route_ref[...]
    y0 = _unpack_bf16_pairs(y0_ref[...])
    y1 = _unpack_bf16_pairs(y1_ref[...])
    y = route[:, 2:3] * y0 + route[:, 3:4] * y1 + _unpack_bf16_pairs(sh_ref[...])
    mod = mod_ref[...]
    o_ref[...] = x_ref[...] + mod[5:6] * (_rms(y) * nw_ref[3:4])


def _moe_combine(pos_km, eo, x1, sh, route, mod_l, nw8):
    tm = TM_MIX
    tiles_per_seq = SEQ // tm
    n_tiles = N_TOK // tm
    y2 = _sc_gather_rows(eo, pos_km)
    return pl.pallas_call(
        _combine_kernel,
        grid=(n_tiles,),
        in_specs=[
            pl.BlockSpec((tm, D_PACKED), lambda i: (i, 0)),
            pl.BlockSpec((tm, D_PACKED), lambda i: (i + n_tiles, 0)),
            pl.BlockSpec((tm, D), lambda i: (i, 0)),
            pl.BlockSpec((tm, D_PACKED), lambda i: (i, 0)),
            pl.BlockSpec((tm, LANES), lambda i: (i, 0)),
            pl.BlockSpec((None, SUBLANES, D), lambda i: (i // tiles_per_seq, 0, 0)),
            pl.BlockSpec((SUBLANES, D), lambda i: (0, 0)),
        ],
        out_specs=pl.BlockSpec((tm, D), lambda i: (i, 0)),
        out_shape=jax.ShapeDtypeStruct((N_TOK, D), F32),
        compiler_params=_cp(("arbitrary",)),
        name="moe_combine",
    )(y2, y2, x1, sh, route, mod_l, nw8)


def _hier_moe_block(x1, mod_l, nw8, layer, w_rg, b_rg, w_re, b_re, w_gate, w_up, w_down,
                    ws_gate, ws_up, ws_down, w_sg):
    h, route, cnt = _moe_route(x1, mod_l, nw8, w_rg, b_rg, w_re, b_re, w_sg)
    pos_km, nblk_e, blk0_e = _moe_pos(cnt, route)
    buf = _sc_dispatch_rows(h, pos_km)
    sh = _moe_shared(h, route, ws_gate, ws_up, ws_down)
    eo = _moe_experts(nblk_e, blk0_e, buf, w_gate, w_up, w_down, layer)
    return _moe_combine(pos_km, eo, x1, sh, route, mod_l, nw8)


def _gdn_pre_kernel(x_ref, mod_ref, nw_ref, wf_ref, wba_ref, cw_ref, misc_ref,
                    q_ref, k_ref, v_ref, z_ref, gb_ref, gt_ref, tail_ref, proj_ref, ba_ref, w_ref,
                    *, tiles_per_seq):
    i = pl.program_id(0)

    @pl.when(i == 0)
    def _():
        tail_ref[...] = jnp.zeros(tail_ref.shape, F32)
        proj_ref[1] = jnp.zeros(proj_ref.shape[1:], F32)
        ba_ref[1] = jnp.zeros(ba_ref.shape[1:], F32)
        w_ref[...] = wf_ref[...].astype(BF16)

    refs = (x_ref, mod_ref, nw_ref, w_ref, wba_ref, cw_ref, misc_ref,
            q_ref, k_ref, v_ref, z_ref, gb_ref, gt_ref, tail_ref, proj_ref, ba_ref)
    first_of_seq = (i - 1) % tiles_per_seq == 0

    @pl.when((i & 1) == 0)
    def _():
        _gdn_pre_step(*refs, cur=0, prev=1, first_of_seq=first_of_seq)

    @pl.when((i & 1) == 1)
    def _():
        _gdn_pre_step(*refs, cur=1, prev=0, first_of_seq=first_of_seq)


def _gdn_pre_step(x_ref, mod_ref, nw_ref, w_ref, wba_ref, cw_ref, misc_ref,
                  q_ref, k_ref, v_ref, z_ref, gb_ref, gt_ref, tail_ref, proj_ref, ba_ref,
                  *, cur, prev, first_of_seq):
    tm = x_ref.shape[0]
    qkv_w = 3 * D
    x = x_ref[...]
    mod = mod_ref[...]
    nw = nw_ref[...]
    h = _rms(x) * nw[0:1] * (1.0 + mod[1:2]) + mod[0:1]
    hb = h.astype(BF16)
    ba_ref[cur] = _dot(hb, wba_ref[...])

    def finish_head(part, hd):
        out_ref = (q_ref, k_ref, v_ref)[part]
        lo = hd * HEAD_DIM
        cols = slice(part * D + lo, part * D + lo + HEAD_DIM)
        cw = cw_ref[:, cols]
        tail = jnp.where(first_of_seq, 0.0, tail_ref[:, cols])
        for r0 in range(0, tm, GDN_ROW_CHUNK):
            rows = slice(r0, r0 + GDN_ROW_CHUNK)
            pre = proj_ref[prev, rows, cols]
            p3, p2, p1 = (_shift_rows(pre, tail, s) for s in (3, 2, 1))
            act = _silu(cw[0:1] * p3 + cw[1:2] * p2 + cw[2:3] * p1 + cw[3:4] * pre)
            if part < 2:
                scale = HEAD_DIM ** -0.5 if part == 0 else 1.0
                act = act * (lax.rsqrt(jnp.sum(act * act, axis=-1, keepdims=True) + EPS) * scale)
            out_ref[rows, lo:lo + HEAD_DIM] = act.astype(BF16)
            tail = pre[GDN_ROW_CHUNK - SUBLANES:]
        tail_ref[:, cols] = tail

    heads = [(part, hd) for part in range(3) for hd in range(HEADS)]
    n_groups = 8
    gw = proj_ref.shape[2] // n_groups
    for j in range(n_groups):
        proj_ref[cur, :, j * gw:(j + 1) * gw] = _dot(hb, w_ref[:, j * gw:(j + 1) * gw])
        for part, hd in heads[j * 3:(j + 1) * 3]:
            finish_head(part, hd)
    z_ref[...] = proj_ref[prev, :, qkv_w:].astype(BF16)

    ba = ba_ref[prev]
    misc = misc_ref[...]
    beta = jax.nn.sigmoid(ba)
    sp_in = ba + misc[1:2]
    softplus = jnp.maximum(sp_in, 0.0) + jnp.log(1.0 + jnp.exp(-jnp.abs(sp_in)))
    g = -jnp.exp(misc[0:1]) * softplus
    row = lax.broadcasted_iota(I32, (tm, tm), 0)
    col = lax.broadcasted_iota(I32, (tm, tm), 1)
    tri = jnp.where((col <= row) & ((col >> CHUNK_SHIFT) == (row >> CHUNK_SHIFT)), 1.0, 0.0)
    gc = jnp.dot(tri, g, precision=HIGHEST, preferred_element_type=F32)
    lane = lax.broadcasted_iota(I32, (tm, LANES), 1)
    gb = jnp.where(lane < HEADS, beta, gc)
    gb_ref[...] = gb
    for c in range(tm // CHUNK):
        blk = jnp.concatenate([gb[c * CHUNK:(c + 1) * CHUNK], jnp.zeros((LANES - CHUNK, LANES), F32)], axis=0)
        gt_ref[c] = blk.T[HEADS:2 * HEADS, :]


def _gdn_pre(x2, mod_l, nw8, w_in, conv_w, a_log, dt_bias):
    tm = TM_GDN
    tiles_per_seq = SEQ // tm
    qkvz = 4 * D
    wba = jnp.pad(w_in[:, qkvz:], ((0, 0), (0, LANES - 2 * HEADS))).astype(BF16)
    cw8 = jnp.pad(conv_w, ((0, SUBLANES - conv_w.shape[0]), (0, 0)))
    misc = jnp.zeros((SUBLANES, LANES), F32)
    misc = misc.at[0, HEADS:2 * HEADS].set(a_log).at[1, HEADS:2 * HEADS].set(dt_bias)
    const = lambda i: (0, 0)
    n_tiles = N_TOK // tm
    src = lambda i: jnp.minimum(i, n_tiles - 1)
    dst = lambda i: jnp.maximum(i - 1, 0)
    tok = lambda i: (dst(i), 0)
    return pl.pallas_call(
        functools.partial(_gdn_pre_kernel, tiles_per_seq=tiles_per_seq),
        grid=(n_tiles + 1,),
        in_specs=[
            pl.BlockSpec((tm, D), lambda i: (src(i), 0)),
            pl.BlockSpec((None, SUBLANES, D), lambda i: (src(i) // tiles_per_seq, 0, 0)),
            pl.BlockSpec((SUBLANES, D), const),
            pl.BlockSpec((D, qkvz), const, pipeline_mode=pl.Buffered(1)),
            pl.BlockSpec((D, LANES), const),
            pl.BlockSpec((SUBLANES, 3 * D), const),
            pl.BlockSpec((SUBLANES, LANES), const),
        ],
        out_specs=[
            pl.BlockSpec((tm, D), tok),
            pl.BlockSpec((tm, D), tok),
            pl.BlockSpec((tm, D), tok),
            pl.BlockSpec((tm, D), tok),
            pl.BlockSpec((tm, LANES), tok),
            pl.BlockSpec((tm // CHUNK, HEADS, LANES), lambda i: (dst(i), 0, 0)),
        ],
        out_shape=[
            jax.ShapeDtypeStruct((N_TOK, D), BF16),
            jax.ShapeDtypeStruct((N_TOK, D), BF16),
            jax.ShapeDtypeStruct((N_TOK, D), BF16),
            jax.ShapeDtypeStruct((N_TOK, D), BF16),
            jax.ShapeDtypeStruct((N_TOK, LANES), F32),
            jax.ShapeDtypeStruct((N_TOK // CHUNK, HEADS, LANES), F32),
        ],
        scratch_shapes=[pltpu.VMEM((SUBLANES, 3 * D), F32),
                        pltpu.VMEM((2, tm, qkvz), F32),
                        pltpu.VMEM((2, tm, LANES), F32),
                        pltpu.VMEM((D, qkvz), BF16)],
        compiler_params=_cp(("arbitrary",)),
        name="gdn_pre",
    )(x2, mod_l, nw8, w_in, wba, cw8, misc)


def _dot_nt(a, b):
    return lax.dot_general(a, b, (((1,), (1,)), ((), ())), preferred_element_type=F32)


def _dot_tn(a, b):
    return lax.dot_general(a, b, (((0,), (0,)), ((), ())), preferred_element_type=F32)


def _gdn_chunk_kernel(q_ref, k_ref, v_ref, gb_ref, gt_ref, o_ref, s_ref):
    @pl.when(pl.program_id(1) == 0)
    def _():
        s_ref[...] = jnp.zeros(s_ref.shape, F32)

    c = CHUNK
    nb = q_ref.shape[0]
    row = lax.broadcasted_iota(I32, (c, c), 0)
    col = lax.broadcasted_iota(I32, (c, c), 1)
    causal = col <= row
    strict = col < row
    eye = jnp.where(col == row, 1.0, 0.0)
    chains = [(b, hd) for b in range(nb) for hd in range(HEADS)]
    st = []
    for b, hd in chains:
        lo = hd * HEAD_DIM
        gb = gb_ref[b]
        q = q_ref[b, :, lo:lo + HEAD_DIM].astype(F32)
        k = k_ref[b, :, lo:lo + HEAD_DIM].astype(F32)
        v = v_ref[b, :, lo:lo + HEAD_DIM].astype(F32)
        beta = gb[:, hd:hd + 1]
        gcol = gb[:, HEADS + hd:HEADS + hd + 1]
        grow = gt_ref[b, hd:hd + 1, 0:c]
        glast = gcol[c - 1:c, :]
        egc = jnp.exp(gcol)
        kb = k * beta
        st.append(dict(
            decay=jnp.exp(jnp.where(causal, gcol - grow, -jnp.inf)),
            kq=jnp.concatenate([kb, q], axis=0).astype(BF16),
            kbf=k.astype(BF16),
            rhs=jnp.concatenate([v * beta, kb * egc], axis=-1).astype(BF16),
            qd=(q * egc).astype(BF16),
            kd=(k * jnp.exp(glast - gcol)).astype(BF16),
            eg=jnp.exp(glast)))
    for x in st:
        kk = _dot_nt(x["kq"], x["kbf"])
        x["p"] = jnp.where(strict, kk[:c] * x["decay"], 0.0)
        x["attn"] = jnp.where(causal, kk[c:] * x["decay"], 0.0).astype(BF16)
        x["t"] = eye - x["p"]
    for _ in range(5):
        for x in st:
            pb = x["p"].astype(BF16)
            x["p"] = _dot(pb, pb)
        for x in st:
            x["t"] = x["t"] + _dot(x["t"].astype(BF16), x["p"].astype(BF16))
    for x in st:
        x["uw"] = _dot(x["t"].astype(BF16), x["rhs"])
    for x, (b, hd) in zip(st, chains):
        s = s_ref[b * HEADS + hd]
        x["s"] = s
        ws = _dot(jnp.concatenate([x["uw"][:, HEAD_DIM:].astype(BF16), x["qd"]], axis=0), s.astype(BF16))
        x["vb"] = (x["uw"][:, :HEAD_DIM] - ws[:c]).astype(BF16)
        x["o"] = ws[c:]
    for x, (b, hd) in zip(st, chains):
        lo = hd * HEAD_DIM
        s_ref[b * HEADS + hd] = x["s"] * x["eg"] + _dot_tn(x["kd"], x["vb"])
        o_ref[b, :, lo:lo + HEAD_DIM] = (x["o"] + _dot(x["attn"], x["vb"])).astype(BF16)


def _gdn_chunks(q, k, v, gb, gt):
    nb = GDN_BATCH_PER_STEP
    n_chunks = SEQ // CHUNK
    tok = lambda b, c: (b, c, 0)
    o = pl.pallas_call(
        _gdn_chunk_kernel,
        grid=(BATCH // nb, n_chunks),
        in_specs=[
            pl.BlockSpec((nb, CHUNK, D), tok),
            pl.BlockSpec((nb, CHUNK, D), tok),
            pl.BlockSpec((nb, CHUNK, D), tok),
            pl.BlockSpec((nb, CHUNK, LANES), tok),
            pl.BlockSpec((nb, None, HEADS, LANES), lambda b, c: (b, c, 0, 0)),
        ],
        out_specs=pl.BlockSpec((nb, CHUNK, D), tok),
        out_shape=jax.ShapeDtypeStruct((BATCH, SEQ, D), BF16),
        scratch_shapes=[pltpu.VMEM((nb * HEADS, HEAD_DIM, HEAD_DIM), F32)],
        compiler_params=_cp(("arbitrary", "arbitrary")),
        name="gdn_chunks",
    )(q.reshape(BATCH, SEQ, D), k.reshape(BATCH, SEQ, D), v.reshape(BATCH, SEQ, D),
      gb.reshape(BATCH, SEQ, LANES), gt.reshape(BATCH, n_chunks, HEADS, LANES))
    return o.reshape(N_TOK, D)


def _gdn_post_kernel(o_ref, z_ref, x_ref, mod_ref, nw_ref, gnw_ref, wout_ref, out_ref):
    gnw = gnw_ref[0:1]
    parts = []
    for hd in range(HEADS):
        lo = hd * HEAD_DIM
        oh = o_ref[:, lo:lo + HEAD_DIM].astype(F32)
        zh = z_ref[:, lo:lo + HEAD_DIM].astype(F32)
        parts.append((_rms(oh) * gnw * _silu(zh)).astype(BF16))
    y = _dot(jnp.concatenate(parts, axis=-1), wout_ref[...])
    mod = mod_ref[...]
    out_ref[...] = x_ref[...] + mod[2:3] * (_rms(y) * nw_ref[1:2])


def _gdn_post(o, z, x2, mod_l, nw8, gdn_norm_w, w_out):
    tm = TM_MIX
    tiles_per_seq = SEQ // tm
    gnw8 = jnp.pad(gdn_norm_w.reshape(1, HEAD_DIM), ((0, SUBLANES - 1), (0, 0)))
    tok = lambda i: (i, 0)
    const = lambda i: (0, 0)
    return pl.pallas_call(
        _gdn_post_kernel,
        grid=(N_TOK // tm,),
        in_specs=[
            pl.BlockSpec((tm, D), tok),
            pl.BlockSpec((tm, D), tok),
            pl.BlockSpec((tm, D), tok),
            pl.BlockSpec((None, SUBLANES, D), lambda i: (i // tiles_per_seq, 0, 0)),
            pl.BlockSpec((SUBLANES, D), const),
            pl.BlockSpec((SUBLANES, HEAD_DIM), const),
            pl.BlockSpec((D, D), const),
        ],
        out_specs=pl.BlockSpec((tm, D), tok),
        out_shape=jax.ShapeDtypeStruct((N_TOK, D), F32),
        compiler_params=_cp(("arbitrary",)),
        name="gdn_post",
    )(o, z, x2, mod_l, nw8, gnw8, w_out.astype(BF16))


def kernel(x, c, ada_w, ada_b, norm_w, conv_in_w, conv_w, conv_out_w, gdn_in_w, gdn_conv_w, gdn_a_log,
           gdn_dt_bias, gdn_norm_w, gdn_out_w, moe_group_w, moe_group_b, moe_expert_w, moe_expert_b,
           moe_w_gate, moe_w_up, moe_w_down, shared_w_gate, shared_w_up, shared_w_down, shared_gate_w):
    mod = _ada_mod(c, ada_w, ada_b)
    nw8 = jnp.pad(norm_w, ((0, 0), (0, SUBLANES - norm_w.shape[1]), (0, 0)))
    x2 = x.reshape(N_TOK, D)

    def moe(x1, layer):
        return _hier_moe_block(
            x1, mod[layer], nw8[layer], layer,
            moe_group_w[layer], moe_group_b[layer], moe_expert_w[layer], moe_expert_b[layer],
            moe_w_gate, moe_w_up, moe_w_down,
            shared_w_gate[layer], shared_w_up[layer], shared_w_down[layer], shared_gate_w[layer])

    x2 = _conv_mixer(x2, mod[0], nw8[0], conv_in_w[0], conv_w[0], conv_out_w[0])
    x2 = moe(x2, 0)
    q, k, v, z, gb, gt = _gdn_pre(x2, mod[1], nw8[1], gdn_in_w[0], gdn_conv_w[0], gdn_a_log[0], gdn_dt_bias[0])
    o = _gdn_chunks(q, k, v, gb, gt)
    x2 = _gdn_post(o, z, x2, mod[1], nw8[1], gdn_norm_w[0], gdn_out_w[0])
    x2 = moe(x2, 1)
    return x2.reshape(BATCH, SEQ, D)
```

```python
import functools

import jax
import jax.numpy as jnp
from jax import lax
from jax.experimental import pallas as pl
from jax.experimental.pallas import tpu as pltpu
from jax.experimental.pallas import tpu_sc as plsc

F32 = jnp.float32
BF16 = jnp.bfloat16
I32 = jnp.int32
HIGHEST = lax.Precision.HIGHEST

D = 1024
D_PACKED = D // 2
BATCH = 4
SEQ = 4096
N_TOK = BATCH * SEQ
HEADS = 8
HEAD_DIM = 128
CHUNK = 64
CHUNK_SHIFT = 6
N_GROUPS = 8
N_EXPERTS = 64
D_EXPERT = 256
D_SHARED = 512
EPS = 1e-6
LANES = 128
SUBLANES = 8
EXPERT_LANE0 = N_GROUPS

MOE_BLOCK = 256
N_SLOTS = N_TOK * 2
N_BLOCKS = N_SLOTS // MOE_BLOCK + N_EXPERTS
P_ROWS = N_BLOCKS * MOE_BLOCK

TM_MIX = 512
TM_GDN = 256
TM_DMA = 256
TM_POS = 2048
GDN_ROW_CHUNK = 64
GDN_BATCH_PER_STEP = 4
VMEM_LIMIT = 56 * 1024 * 1024


def _cp(sem):
    return pltpu.CompilerParams(dimension_semantics=sem, vmem_limit_bytes=VMEM_LIMIT)


def _rms(x):
    return x * lax.rsqrt(jnp.mean(x * x, axis=-1, keepdims=True) + EPS)


def _silu(x):
    return x * jax.nn.sigmoid(x)


def _dot(a, b):
    return jnp.dot(a, b, preferred_element_type=F32)


U32 = jnp.uint32
HI16 = 0xFFFF0000


def _pack_bf16_pairs(xb):
    half = xb.shape[1] // 2
    bits = lax.bitcast_convert_type(xb.astype(F32), U32)
    return lax.bitcast_convert_type(bits[:, :half] | (bits[:, half:] >> 16), I32)


def _unpack_bf16_pairs(words):
    bits = lax.bitcast_convert_type(words, U32)
    hi = lax.bitcast_convert_type(bits & jnp.uint32(HI16), F32)
    lo = lax.bitcast_convert_type(bits << 16, F32)
    return jnp.concatenate([hi, lo], axis=1)


def _shift_rows(x, tail, s):
    sub = lax.broadcasted_iota(I32, tail.shape, 0)
    rolled = pltpu.roll(x, s, axis=0)
    head = jnp.where(sub < s, pltpu.roll(tail, s, axis=0), rolled[0:SUBLANES])
    return jnp.concatenate([head, rolled[SUBLANES:]], axis=0)


def _causal_taps(x, tail_ref, taps, first_of_seq):
    tail = jnp.where(first_of_seq, 0.0, tail_ref[...])
    sub = lax.broadcasted_iota(I32, tail.shape, 0)
    out = []
    for s in taps:
        rolled = pltpu.roll(x, s, axis=0)
        head = jnp.where(sub < s, pltpu.roll(tail, s, axis=0), rolled[0:SUBLANES])
        out.append(jnp.concatenate([head, rolled[SUBLANES:]], axis=0))
    tail_ref[...] = x[x.shape[0] - SUBLANES:]
    return out


def _ada_kernel(c_ref, w_ref, b_ref, o_ref):
    cs = _silu(c_ref[...])
    o_ref[0] = _dot(cs.astype(BF16), w_ref[0].astype(BF16)) + b_ref[0]


def _ada_mod(c, ada_w, ada_b):
    depth = ada_w.shape[0]
    tn = 1024
    c8 = jnp.pad(c, ((0, SUBLANES - BATCH), (0, 0)))
    mod = pl.pallas_call(
        _ada_kernel,
        grid=(depth, 6 * D // tn),
        in_specs=[
            pl.BlockSpec((SUBLANES, D), lambda l, j: (0, 0)),
            pl.BlockSpec((1, D, tn), lambda l, j: (l, 0, j)),
            pl.BlockSpec((1, 1, tn), lambda l, j: (l, 0, j)),
        ],
        out_specs=pl.BlockSpec((1, SUBLANES, tn), lambda l, j: (l, 0, j)),
        out_shape=jax.ShapeDtypeStruct((depth, SUBLANES, 6 * D), F32),
        compiler_params=_cp(("arbitrary", "arbitrary")),
        name="ada_mod",
    )(c8, ada_w, ada_b.reshape(depth, 1, 6 * D))
    mod = mod[:, :BATCH].reshape(depth, BATCH, 6, D)
    return jnp.pad(mod, ((0, 0), (0, 0), (0, 2), (0, 0)))


def _conv_mixer_kernel(x_ref, mod_ref, nw_ref, win_ref, cw_ref, wout_ref, o_ref, tail_ref, *, tiles_per_seq):
    @pl.when(pl.program_id(0) == 0)
    def _():
        tail_ref[...] = jnp.zeros(tail_ref.shape, F32)

    x = x_ref[...]
    mod = mod_ref[...]
    nw = nw_ref[...]
    h = _rms(x) * nw[0:1] * (1.0 + mod[1:2]) + mod[0:1]
    bcx = _dot(h.astype(BF16), win_ref[...])
    u = bcx[:, D:2 * D] * bcx[:, 2 * D:]
    u2, u1 = _causal_taps(u, tail_ref, (2, 1), pl.program_id(0) % tiles_per_seq == 0)
    cw = cw_ref[...]
    conv = cw[0:1] * u2 + cw[1:2] * u1 + cw[2:3] * u
    y = _dot((bcx[:, :D] * conv).astype(BF16), wout_ref[...])
    o_ref[...] = x + mod[2:3] * (_rms(y) * nw[1:2])


def _conv_mixer(x2, mod_l, nw8, w_in, conv_w, w_out):
    tm = TM_MIX
    tiles_per_seq = SEQ // tm
    cw8 = jnp.pad(conv_w, ((0, SUBLANES - conv_w.shape[0]), (0, 0)))
    return pl.pallas_call(
        functools.partial(_conv_mixer_kernel, tiles_per_seq=tiles_per_seq),
        grid=(N_TOK // tm,),
        in_specs=[
            pl.BlockSpec((tm, D), lambda i: (i, 0)),
            pl.BlockSpec((None, SUBLANES, D), lambda i: (i // tiles_per_seq, 0, 0)),
            pl.BlockSpec((SUBLANES, D), lambda i: (0, 0)),
            pl.BlockSpec((D, 3 * D), lambda i: (0, 0)),
            pl.BlockSpec((SUBLANES, D), lambda i: (0, 0)),
            pl.BlockSpec((D, D), lambda i: (0, 0)),
        ],
        out_specs=pl.BlockSpec((tm, D), lambda i: (i, 0)),
        out_shape=jax.ShapeDtypeStruct((N_TOK, D), F32),
        scratch_shapes=[pltpu.VMEM((SUBLANES, D), F32)],
        compiler_params=_cp(("arbitrary",)),
        name="conv_mixer",
    )(x2, mod_l, nw8, w_in.astype(BF16), cw8, w_out.astype(BF16))


def _lane_pick(lane, mask_val_pairs):
    out = jnp.zeros(lane.shape, F32)
    for idx, val in mask_val_pairs:
        out = jnp.where(lane == float(idx), val, out)
    return out


def _moe_route_kernel(x_ref, mod_ref, nw_ref, wr_ref, br_ref, wsg_ref,
                      h_ref, route_ref, cnt_ref, carry_ref):
    tm = x_ref.shape[0]

    @pl.when(pl.program_id(0) == 0)
    def _():
        carry_ref[...] = jnp.zeros(carry_ref.shape, F32)

    x = x_ref[...]
    mod = mod_ref[...]
    nw = nw_ref[...]
    h = _rms(x) * nw[2:3] * (1.0 + mod[4:5]) + mod[3:4]
    hb = h.astype(BF16)
    h_ref[...] = _pack_bf16_pairs(hb)
    h_lo = (h - hb.astype(F32)).astype(BF16)
    hw = _dot(hb, wr_ref[...]) + _dot(h_lo, wr_ref[...])
    logits = hw[:, :LANES] + hw[:, LANES:] + br_ref[0:1]
    sg = jax.nn.sigmoid(jnp.sum(h * wsg_ref[0:1], axis=-1, keepdims=True))
    lane_i = lax.broadcasted_iota(I32, (tm, LANES), 1)
    lane = lane_i.astype(F32)
    neg = jnp.float32(-jnp.inf)
    is_group = lane_i < N_GROUPS
    gl = jnp.where(is_group, logits, neg)
    gmax = jnp.max(gl, axis=-1, keepdims=True)
    gsel = jnp.min(jnp.where(gl == gmax, lane, float(LANES)), axis=-1, keepdims=True)
    psel = 1.0 / jnp.sum(jnp.where(is_group, jnp.exp(logits - gmax), 0.0), axis=-1, keepdims=True)
    lane_group = ((lane_i - EXPERT_LANE0) >> 3).astype(F32)
    in_group = (lane_i >= EXPERT_LANE0) & (lane_i < EXPERT_LANE0 + N_EXPERTS) & (lane_group == gsel)
    el = jnp.where(in_group, logits, neg)
    v1 = jnp.max(el, axis=-1, keepdims=True)
    i1 = jnp.min(jnp.where(el == v1, lane, float(LANES)), axis=-1, keepdims=True)
    el2 = jnp.where(lane == i1, neg, el)
    v2 = jnp.max(el2, axis=-1, keepdims=True)
    i2 = jnp.min(jnp.where(el2 == v2, lane, float(LANES)), axis=-1, keepdims=True)
    e2 = jnp.exp(v2 - v1)
    g1 = psel / (1.0 + e2)
    g2 = psel * e2 / (1.0 + e2)

    oh1 = lane == i1
    oh2 = lane == i2
    cnt = jnp.where(oh1 | oh2, 1.0, 0.0).astype(BF16)
    row = lax.broadcasted_iota(I32, (tm, tm), 0)
    col = lax.broadcasted_iota(I32, (tm, tm), 1)
    tri = jnp.where(col < row, 1.0, 0.0).astype(BF16)
    pre = _dot(tri, cnt) + carry_ref[0:1]
    r1 = jnp.sum(jnp.where(oh1, pre, 0.0), axis=-1, keepdims=True)
    r2 = jnp.sum(jnp.where(oh2, pre, 0.0), axis=-1, keepdims=True)
    carry_ref[...] = carry_ref[...] + jnp.sum(cnt.astype(F32), axis=0, keepdims=True)
    cnt_ref[...] = carry_ref[...]
    route_ref[...] = _lane_pick(lane, [
        (0, i1 - EXPERT_LANE0), (1, i2 - EXPERT_LANE0),
        (2, g1), (3, g2), (4, r1), (5, r2), (6, sg)])


def _moe_route(x1, mod_l, nw8, w_rg, b_rg, w_re, b_re, w_sg):
    tm = TM_MIX
    tiles_per_seq = SEQ // tm
    pad_l = LANES - N_GROUPS - N_EXPERTS
    wr = jnp.pad(jnp.concatenate([w_rg, w_re], axis=1), ((0, 0), (0, pad_l)))
    wr_hi = wr.astype(BF16)
    wr = jnp.concatenate([wr_hi, (wr - wr_hi.astype(F32)).astype(BF16)], axis=1)
    br = jnp.pad(jnp.concatenate([b_rg, b_re])[None, :], ((0, SUBLANES - 1), (0, pad_l)))
    wsg8 = jnp.pad(w_sg.reshape(1, D), ((0, SUBLANES - 1), (0, 0)))
    const = lambda i: (0, 0)
    return pl.pallas_call(
        _moe_route_kernel,
        grid=(N_TOK // tm,),
        in_specs=[
            pl.BlockSpec((tm, D), lambda i: (i, 0)),
            pl.BlockSpec((None, SUBLANES, D), lambda i: (i // tiles_per_seq, 0, 0)),
            pl.BlockSpec((SUBLANES, D), const),
            pl.BlockSpec((D, 2 * LANES), const),
            pl.BlockSpec((SUBLANES, LANES), const),
            pl.BlockSpec((SUBLANES, D), const),
        ],
        out_specs=[
            pl.BlockSpec((tm, D_PACKED), lambda i: (i, 0)),
            pl.BlockSpec((tm, LANES), lambda i: (i, 0)),
            pl.BlockSpec((SUBLANES, LANES), const),
        ],
        out_shape=[
            jax.ShapeDtypeStruct((N_TOK, D_PACKED), I32),
            jax.ShapeDtypeStruct((N_TOK, LANES), F32),
            jax.ShapeDtypeStruct((SUBLANES, LANES), F32),
        ],
        scratch_shapes=[pltpu.VMEM((SUBLANES, LANES), F32)],
        compiler_params=_cp(("arbitrary",)),
        name="moe_route",
    )(x1, mod_l, nw8, wr, br, wsg8)


def _moe_shared_kernel(h_ref, route_ref, wgu_ref, wd_ref, sh_ref):
    hb = _unpack_bf16_pairs(h_ref[...]).astype(BF16)
    gu = _dot(hb, wgu_ref[...])
    hid = _silu(gu[:, :D_SHARED]) * gu[:, D_SHARED:]
    ys = _dot(hid.astype(BF16), wd_ref[...])
    sh_ref[...] = _pack_bf16_pairs((route_ref[:, 6:7] * ys).astype(BF16))


def _moe_shared(hp, route, ws_gate, ws_up, ws_down):
    tm = TM_MIX
    const = lambda i: (0, 0)
    wgu = jnp.concatenate([ws_gate, ws_up], axis=1).astype(BF16)
    return pl.pallas_call(
        _moe_shared_kernel,
        grid=(N_TOK // tm,),
        in_specs=[
            pl.BlockSpec((tm, D_PACKED), lambda i: (i, 0)),
            pl.BlockSpec((tm, LANES), lambda i: (i, 0)),
            pl.BlockSpec((D, 2 * D_SHARED), const),
            pl.BlockSpec((D_SHARED, D), const),
        ],
        out_specs=pl.BlockSpec((tm, D_PACKED), lambda i: (i, 0)),
        out_shape=jax.ShapeDtypeStruct((N_TOK, D_PACKED), I32),
        compiler_params=_cp(("arbitrary",)),
        name="moe_shared",
    )(hp, route, wgu, ws_down.astype(BF16))


def _n_blocks_per_expert(cnt_row):
    return jnp.floor((cnt_row + (MOE_BLOCK - 1)) * (1.0 / MOE_BLOCK))


def _moe_pos_kernel(cnt_ref, route_ref, pos_ref, seg_ref):
    tm = route_ref.shape[0]
    nb = _n_blocks_per_expert(cnt_ref[...])
    r = lax.broadcasted_iota(I32, (LANES, LANES), 0)
    c = lax.broadcasted_iota(I32, (LANES, LANES), 1)
    excl = jnp.dot(nb, jnp.where(r < c, 1.0, 0.0), precision=HIGHEST, preferred_element_type=F32)
    pstart = excl[0:1] * MOE_BLOCK
    route = route_ref[...]
    lane = lax.broadcasted_iota(I32, (tm, LANES), 1).astype(F32)
    e1 = route[:, 0:1] + EXPERT_LANE0
    e2 = route[:, 1:2] + EXPERT_LANE0
    p1 = jnp.sum(jnp.where(lane == e1, pstart, 0.0), axis=-1, keepdims=True) + route[:, 4:5]
    p2 = jnp.sum(jnp.where(lane == e2, pstart, 0.0), axis=-1, keepdims=True) + route[:, 5:6]
    pos_ref[...] = _lane_pick(lane, [(0, p1), (1, p2)]).T[0:SUBLANES].astype(I32)

    @pl.when(pl.program_id(0) == 0)
    def _():
        sub = lax.broadcasted_iota(I32, (SUBLANES, LANES), 0)
        seg_ref[...] = jnp.where(sub == 0, nb, excl).astype(I32)


def _moe_pos(cnt, route):
    tm = TM_POS
    pos, seg = pl.pallas_call(
        _moe_pos_kernel,
        grid=(N_TOK // tm,),
        in_specs=[
            pl.BlockSpec((SUBLANES, LANES), lambda i: (0, 0)),
            pl.BlockSpec((tm, LANES), lambda i: (i, 0)),
        ],
        out_specs=[
            pl.BlockSpec((SUBLANES, tm), lambda i: (0, i)),
            pl.BlockSpec((SUBLANES, LANES), lambda i: (0, 0)),
        ],
        out_shape=[
            jax.ShapeDtypeStruct((SUBLANES, N_TOK), I32),
            jax.ShapeDtypeStruct((SUBLANES, LANES), I32),
        ],
        compiler_params=_cp(("arbitrary",)),
        name="moe_pos",
    )(cnt, route)
    experts = slice(EXPERT_LANE0, EXPERT_LANE0 + N_EXPERTS)
    return pos[:2].reshape(-1), seg[0, experts], seg[1, experts]


def _expert_kernel(nblk_ref, blk0_ref, buf_ref, wg_ref, wu_ref, wd_ref, after_ref, eo_ref,
                   xin_ref, out_ref, wgb_ref, wub_ref, wdb_ref, sem_in, sem_out):
    del after_ref
    e = pl.program_id(0)
    last = pl.num_programs(0) - 1
    n = nblk_ref[e]
    blk0 = blk0_ref[e]
    total = blk0_ref[last] + nblk_ref[last]
    bm = MOE_BLOCK
    n_in = xin_ref.shape[0]

    def rows(g):
        return pl.ds(pl.multiple_of(g * bm, bm), bm)

    def in_copy(g):
        slot = lax.rem(g, n_in)
        return pltpu.make_async_copy(buf_ref.at[rows(g)], xin_ref.at[slot], sem_in.at[slot])

    def out_copy(g):
        return pltpu.make_async_copy(out_ref.at[g & 1], eo_ref.at[rows(g)], sem_out.at[g & 1])

    @pl.when(n > 0)
    def _():
        wgb_ref[...] = wg_ref[...].astype(BF16)
        wub_ref[...] = wu_ref[...].astype(BF16)
        wdb_ref[...] = wd_ref[...].astype(BF16)

    def block(j, carry):
        g = blk0 + j

        @pl.when(g == 0)
        def _():
            in_copy(g).start()

            @pl.when(total > 1)
            def _():
                in_copy(g + 1).start()

        @pl.when(g + 2 < total)
        def _():
            in_copy(g + 2).start()

        in_copy(g).wait()

        @pl.when(g >= 2)
        def _():
            out_copy(g - 2).wait()

        xb = _unpack_bf16_pairs(xin_ref[lax.rem(g, n_in)]).astype(BF16)
        hid = _silu(_dot(xb, wgb_ref[...])) * _dot(xb, wub_ref[...])
        out_ref[g & 1] = _pack_bf16_pairs(_dot(hid.astype(BF16), wdb_ref[...]).astype(BF16))
        out_copy(g).start()
        return carry

    lax.fori_loop(0, n, block, 0)

    @pl.when((e == last) & (total >= 2))
    def _():
        out_copy(total - 2).wait()

    @pl.when((e == last) & (total >= 1))
    def _():
        out_copy(total - 1).wait()


def _moe_experts(nblk_e, blk0_e, buf, w_gate, w_up, w_down, layer, run_after):
    bm = MOE_BLOCK
    wspec = lambda shape: pl.BlockSpec((None, None) + shape, lambda e, nb, b0: (layer, e, 0, 0))
    return pl.pallas_call(
        _expert_kernel,
        grid_spec=pltpu.PrefetchScalarGridSpec(
            num_scalar_prefetch=2,
            grid=(N_EXPERTS,),
            in_specs=[
                pl.BlockSpec(memory_space=pl.ANY),
                wspec((D, D_EXPERT)),
                wspec((D, D_EXPERT)),
                wspec((D_EXPERT, D)),
                pl.BlockSpec(memory_space=pl.ANY),
            ],
            out_specs=pl.BlockSpec(memory_space=pl.ANY),
            scratch_shapes=[
                pltpu.VMEM((3, bm, D_PACKED), I32),
                pltpu.VMEM((2, bm, D_PACKED), I32),
                pltpu.VMEM((D, D_EXPERT), BF16),
                pltpu.VMEM((D, D_EXPERT), BF16),
                pltpu.VMEM((D_EXPERT, D), BF16),
                pltpu.SemaphoreType.DMA((3,)),
                pltpu.SemaphoreType.DMA((2,)),
            ],
        ),
        out_shape=jax.ShapeDtypeStruct((P_ROWS, D_PACKED), I32),
        compiler_params=_cp(("arbitrary",)),
        name="moe_experts",
    )(nblk_e, blk0_e, buf, w_gate, w_up, w_down, run_after)


SC_CORES = 2
SC_SUBCORES = 16
SC_WORKERS = SC_CORES * SC_SUBCORES
SC_CHUNK = 64


def _sc_gather_store(table_hbm, idx_v, out_hbm, out_base, n_rows, bufs, sems):
    n_chunks = n_rows // SC_CHUNK
    assert n_chunks % 2 == 0

    def gather(j, b):
        off = pl.multiple_of(j * SC_CHUNK, SC_CHUNK)
        return pltpu.make_async_copy(table_hbm.at[idx_v.at[pl.ds(off, SC_CHUNK)]], bufs[b], sems[b])

    gather(0, 0).start()

    @pl.loop(0, n_chunks, step=2)
    def _(j):
        for b in range(2):
            jj = j + b

            @pl.when(jj + 1 < n_chunks)
            def _():
                gather(jj + 1, 1 - b).start()

            gather(jj, b).wait()
            row0 = pl.multiple_of(out_base + jj * SC_CHUNK, SC_CHUNK)
            pltpu.sync_copy(bufs[b], out_hbm.at[pl.ds(row0, SC_CHUNK)])


def _sc_row_buffers(width, dtype):
    return [pltpu.VMEM((SC_CHUNK, width), dtype), pltpu.VMEM((SC_CHUNK, width), dtype),
            pltpu.SemaphoreType.DMA, pltpu.SemaphoreType.DMA]


def _sc_gather_rows(table, idx):
    n_idx = idx.shape[0]
    width = table.shape[1]
    per_w = n_idx // SC_WORKERS
    mesh = plsc.VectorSubcoreMesh(core_axis_name="c", subcore_axis_name="s")

    def body(table_hbm, idx_hbm, out_hbm, idx_v, buf0, buf1, sem0, sem1):
        wid = lax.axis_index("s") * SC_CORES + lax.axis_index("c")
        base = wid * per_w
        pltpu.sync_copy(idx_hbm.at[pl.ds(base, per_w)], idx_v)
        _sc_gather_store(table_hbm, idx_v, out_hbm, base, per_w, (buf0, buf1), (sem0, sem1))

    return pl.kernel(
        body,
        out_type=jax.ShapeDtypeStruct((n_idx, width), table.dtype),
        mesh=mesh,
        scratch_types=[pltpu.VMEM((per_w,), I32)] + _sc_row_buffers(width, table.dtype),
        name="sc_gather_rows",
    )(table, idx)


def _sc_dispatch_rows(h, pos_km):
    per_w = P_ROWS // SC_WORKERS
    lanes = 16
    mesh = plsc.VectorSubcoreMesh(core_axis_name="c", subcore_axis_name="s")

    def body(h_hbm, pos_hbm, buf_hbm, pos_v, tok_v, buf0, buf1, sem0, sem1):
        wid = lax.axis_index("s") * SC_CORES + lax.axis_index("c")
        base = wid * per_w
        pltpu.sync_copy(pos_hbm, pos_v)

        @pl.loop(0, per_w // lanes)
        def _(i):
            off = pl.multiple_of(i * lanes, lanes)
            tok_v[pl.ds(off, lanes)] = (base + off + lax.iota(I32, lanes)) & (N_TOK - 1)

        @pl.loop(0, N_SLOTS // lanes)
        def _(i):
            off = pl.multiple_of(i * lanes, lanes)
            local = pos_v[pl.ds(off, lanes)] - base
            mine = (local >= 0) & (local < per_w)
            slot = off + lax.iota(I32, lanes)
            plsc.store_scatter(tok_v, [jnp.where(mine, local, 0)], slot & (N_TOK - 1), mask=mine)

        _sc_gather_store(h_hbm, tok_v, buf_hbm, base, per_w, (buf0, buf1), (sem0, sem1))

    return pl.kernel(
        body,
        out_type=jax.ShapeDtypeStruct((P_ROWS, h.shape[1]), h.dtype),
        mesh=mesh,
        scratch_types=[pltpu.VMEM((N_SLOTS,), I32), pltpu.VMEM((per_w,), I32)]
        + _sc_row_buffers(h.shape[1], h.dtype),
        compiler_params=pltpu.CompilerParams(needs_layout_passes=False),
        name="sc_dispatch_rows",
    )(h, pos_km)


def _combine_kernel(y0_ref, y1_ref, x_ref, sh_ref, route_ref, mod_ref, nw_ref, o_ref):
    route = route_ref[...]
    y0 = _unpack_bf16_pairs(y0_ref[...])
    y1 = _unpack_bf16_pairs(y1_ref[...])
    y = route[:, 2:3] * y0 + route[:, 3:4] * y1 + _unpack_bf16_pairs(sh_ref[...])
    mod = mod_ref[...]
    o_ref[...] = x_ref[...] + mod[5:6] * (_rms(y) * nw_ref[3:4])


def _moe_combine(pos_km, eo, x1, sh, route, mod_l, nw8):
    tm = TM_MIX
    tiles_per_seq = SEQ // tm
    n_tiles = N_TOK // tm
    y2 = _sc_gather_rows(eo, pos_km)
    return pl.pallas_call(
        _combine_kernel,
        grid=(n_tiles,),
        in_specs=[
            pl.BlockSpec((tm, D_PACKED), lambda i: (i, 0)),
            pl.BlockSpec((tm, D_PACKED), lambda i: (i + n_tiles, 0)),
            pl.BlockSpec((tm, D), lambda i: (i, 0)),
            pl.BlockSpec((tm, D_PACKED), lambda i: (i, 0)),
            pl.BlockSpec((tm, LANES), lambda i: (i, 0)),
            pl.BlockSpec((None, SUBLANES, D), lambda i: (i // tiles_per_seq, 0, 0)),
            pl.BlockSpec((SUBLANES, D), lambda i: (0, 0)),
        ],
        out_specs=pl.BlockSpec((tm, D), lambda i: (i, 0)),
        out_shape=jax.ShapeDtypeStruct((N_TOK, D), F32),
        compiler_params=_cp(("arbitrary",)),
        name="moe_combine",
    )(y2, y2, x1, sh, route, mod_l, nw8)


def _hier_moe_block(x1, mod_l, nw8, layer, w_rg, b_rg, w_re, b_re, w_gate, w_up, w_down,
                    ws_gate, ws_up, ws_down, w_sg):
    h, route, cnt = _moe_route(x1, mod_l, nw8, w_rg, b_rg, w_re, b_re, w_sg)
    pos_km, nblk_e, blk0_e = _moe_pos(cnt, route)
    buf = _sc_dispatch_rows(h, pos_km)
    sh = _moe_shared(h, route, ws_gate, ws_up, ws_down)
    eo = _moe_experts(nblk_e, blk0_e, buf, w_gate, w_up, w_down, layer, run_after=sh)
    return _moe_combine(pos_km, eo, x1, sh, route, mod_l, nw8)


def _gdn_pre_kernel(x_ref, mod_ref, nw_ref, wf_ref, wba_ref, cw_ref, misc_ref,
                    q_ref, k_ref, v_ref, z_ref, gb_ref, gt_ref, tail_ref, proj_ref, ba_ref, w_ref,
                    *, tiles_per_seq):
    i = pl.program_id(0)

    @pl.when(i == 0)
    def _():
        tail_ref[...] = jnp.zeros(tail_ref.shape, F32)
        proj_ref[1] = jnp.zeros(proj_ref.shape[1:], F32)
        ba_ref[1] = jnp.zeros(ba_ref.shape[1:], F32)
        w_ref[...] = wf_ref[...].astype(BF16)

    refs = (x_ref, mod_ref, nw_ref, w_ref, wba_ref, cw_ref, misc_ref,
            q_ref, k_ref, v_ref, z_ref, gb_ref, gt_ref, tail_ref, proj_ref, ba_ref)
    first_of_seq = (i - 1) % tiles_per_seq == 0

    @pl.when((i & 1) == 0)
    def _():
        _gdn_pre_step(*refs, cur=0, prev=1, first_of_seq=first_of_seq)

    @pl.when((i & 1) == 1)
    def _():
        _gdn_pre_step(*refs, cur=1, prev=0, first_of_seq=first_of_seq)


def _gdn_pre_step(x_ref, mod_ref, nw_ref, w_ref, wba_ref, cw_ref, misc_ref,
                  q_ref, k_ref, v_ref, z_ref, gb_ref, gt_ref, tail_ref, proj_ref, ba_ref,
                  *, cur, prev, first_of_seq):
    tm = x_ref.shape[0]
    qkv_w = 3 * D
    x = x_ref[...]
    mod = mod_ref[...]
    nw = nw_ref[...]
    h = _rms(x) * nw[0:1] * (1.0 + mod[1:2]) + mod[0:1]
    hb = h.astype(BF16)
    ba_ref[cur] = _dot(hb, wba_ref[...])

    def finish_head(part, hd):
        out_ref = (q_ref, k_ref, v_ref)[part]
        lo = hd * HEAD_DIM
        cols = slice(part * D + lo, part * D + lo + HEAD_DIM)
        cw = cw_ref[:, cols]
        tail = jnp.where(first_of_seq, 0.0, tail_ref[:, cols])
        for r0 in range(0, tm, GDN_ROW_CHUNK):
            rows = slice(r0, r0 + GDN_ROW_CHUNK)
            pre = proj_ref[prev, rows, cols]
            p3, p2, p1 = (_shift_rows(pre, tail, s) for s in (3, 2, 1))
            act = _silu(cw[0:1] * p3 + cw[1:2] * p2 + cw[2:3] * p1 + cw[3:4] * pre)
            if part < 2:
                scale = HEAD_DIM ** -0.5 if part == 0 else 1.0
                act = act * (lax.rsqrt(jnp.sum(act * act, axis=-1, keepdims=True) + EPS) * scale)
            out_ref[rows, lo:lo + HEAD_DIM] = act.astype(BF16)
            tail = pre[GDN_ROW_CHUNK - SUBLANES:]
        tail_ref[:, cols] = tail

    heads = [(part, hd) for part in range(3) for hd in range(HEADS)]
    n_groups = 8
    gw = proj_ref.shape[2] // n_groups
    for j in range(n_groups):
        proj_ref[cur, :, j * gw:(j + 1) * gw] = _dot(hb, w_ref[:, j * gw:(j + 1) * gw])
        for part, hd in heads[j * 3:(j + 1) * 3]:
            finish_head(part, hd)
    z_ref[...] = proj_ref[prev, :, qkv_w:].astype(BF16)

    ba = ba_ref[prev]
    misc = misc_ref[...]
    beta = jax.nn.sigmoid(ba)
    sp_in = ba + misc[1:2]
    softplus = jnp.maximum(sp_in, 0.0) + jnp.log(1.0 + jnp.exp(-jnp.abs(sp_in)))
    g = -jnp.exp(misc[0:1]) * softplus
    row = lax.broadcasted_iota(I32, (tm, tm), 0)
    col = lax.broadcasted_iota(I32, (tm, tm), 1)
    tri = jnp.where((col <= row) & ((col >> CHUNK_SHIFT) == (row >> CHUNK_SHIFT)), 1.0, 0.0)
    gc = jnp.dot(tri, g, precision=HIGHEST, preferred_element_type=F32)
    lane = lax.broadcasted_iota(I32, (tm, LANES), 1)
    gb = jnp.where(lane < HEADS, beta, gc)
    gb_ref[...] = gb
    for c in range(tm // CHUNK):
        blk = jnp.concatenate([gb[c * CHUNK:(c + 1) * CHUNK], jnp.zeros((LANES - CHUNK, LANES), F32)], axis=0)
        gt_ref[c] = blk.T[HEADS:2 * HEADS, :]


def _gdn_pre(x2, mod_l, nw8, w_in, conv_w, a_log, dt_bias):
    tm = TM_GDN
    tiles_per_seq = SEQ // tm
    qkvz = 4 * D
    wba = jnp.pad(w_in[:, qkvz:], ((0, 0), (0, LANES - 2 * HEADS))).astype(BF16)
    cw8 = jnp.pad(conv_w, ((0, SUBLANES - conv_w.shape[0]), (0, 0)))
    misc = jnp.zeros((SUBLANES, LANES), F32)
    misc = misc.at[0, HEADS:2 * HEADS].set(a_log).at[1, HEADS:2 * HEADS].set(dt_bias)
    const = lambda i: (0, 0)
    n_tiles = N_TOK // tm
    src = lambda i: jnp.minimum(i, n_tiles - 1)
    dst = lambda i: jnp.maximum(i - 1, 0)
    tok = lambda i: (dst(i), 0)
    return pl.pallas_call(
        functools.partial(_gdn_pre_kernel, tiles_per_seq=tiles_per_seq),
        grid=(n_tiles + 1,),
        in_specs=[
            pl.BlockSpec((tm, D), lambda i: (src(i), 0)),
            pl.BlockSpec((None, SUBLANES, D), lambda i: (src(i) // tiles_per_seq, 0, 0)),
            pl.BlockSpec((SUBLANES, D), const),
            pl.BlockSpec((D, qkvz), const, pipeline_mode=pl.Buffered(1)),
            pl.BlockSpec((D, LANES), const),
            pl.BlockSpec((SUBLANES, 3 * D), const),
            pl.BlockSpec((SUBLANES, LANES), const),
        ],
        out_specs=[
            pl.BlockSpec((tm, D), tok),
            pl.BlockSpec((tm, D), tok),
            pl.BlockSpec((tm, D), tok),
            pl.BlockSpec((tm, D), tok),
            pl.BlockSpec((tm, LANES), tok),
            pl.BlockSpec((tm // CHUNK, HEADS, LANES), lambda i: (dst(i), 0, 0)),
        ],
        out_shape=[
            jax.ShapeDtypeStruct((N_TOK, D), BF16),
            jax.ShapeDtypeStruct((N_TOK, D), BF16),
            jax.ShapeDtypeStruct((N_TOK, D), BF16),
            jax.ShapeDtypeStruct((N_TOK, D), BF16),
            jax.ShapeDtypeStruct((N_TOK, LANES), F32),
            jax.ShapeDtypeStruct((N_TOK // CHUNK, HEADS, LANES), F32),
        ],
        scratch_shapes=[pltpu.VMEM((SUBLANES, 3 * D), F32),
                        pltpu.VMEM((2, tm, qkvz), F32),
                        pltpu.VMEM((2, tm, LANES), F32),
                        pltpu.VMEM((D, qkvz), BF16)],
        compiler_params=_cp(("arbitrary",)),
        name="gdn_pre",
    )(x2, mod_l, nw8, w_in, wba, cw8, misc)


def _dot_nt(a, b):
    return lax.dot_general(a, b, (((1,), (1,)), ((), ())), preferred_element_type=F32)


def _dot_tn(a, b):
    return lax.dot_general(a, b, (((0,), (0,)), ((), ())), preferred_element_type=F32)


def _gdn_chunk_kernel(q_ref, k_ref, v_ref, gb_ref, gt_ref, o_ref, s_ref):
    @pl.when(pl.program_id(1) == 0)
    def _():
        s_ref[...] = jnp.zeros(s_ref.shape, F32)

    c = CHUNK
    nb = q_ref.shape[0]
    row = lax.broadcasted_iota(I32, (c, c), 0)
    col = lax.broadcasted_iota(I32, (c, c), 1)
    causal = col <= row
    strict = col < row
    eye = jnp.where(col == row, 1.0, 0.0)
    chains = [(b, hd) for b in range(nb) for hd in range(HEADS)]
    st = []
    for b, hd in chains:
        lo = hd * HEAD_DIM
        gb = gb_ref[b]
        q = q_ref[b, :, lo:lo + HEAD_DIM].astype(F32)
        k = k_ref[b, :, lo:lo + HEAD_DIM].astype(F32)
        v = v_ref[b, :, lo:lo + HEAD_DIM].astype(F32)
        beta = gb[:, hd:hd + 1]
        gcol = gb[:, HEADS + hd:HEADS + hd + 1]
        grow = gt_ref[b, hd:hd + 1, 0:c]
        glast = gcol[c - 1:c, :]
        egc = jnp.exp(gcol)
        kb = k * beta
        st.append(dict(
            decay=jnp.exp(jnp.where(causal, gcol - grow, -jnp.inf)),
            kq=jnp.concatenate([kb, q], axis=0).astype(BF16),
            kbf=k.astype(BF16),
            rhs=jnp.concatenate([v * beta, kb * egc], axis=-1).astype(BF16),
            qd=(q * egc).astype(BF16),
            kd=(k * jnp.exp(glast - gcol)).astype(BF16),
            eg=jnp.exp(glast)))
    for x in st:
        kk = _dot_nt(x["kq"], x["kbf"])
        x["p"] = jnp.where(strict, kk[:c] * x["decay"], 0.0)
        x["attn"] = jnp.where(causal, kk[c:] * x["decay"], 0.0).astype(BF16)
        x["t"] = eye - x["p"]
    for _ in range(5):
        for x in st:
            pb = x["p"].astype(BF16)
            x["p"] = _dot(pb, pb)
        for x in st:
            x["t"] = x["t"] + _dot(x["t"].astype(BF16), x["p"].astype(BF16))
    for x in st:
        x["uw"] = _dot(x["t"].astype(BF16), x["rhs"])
    for x, (b, hd) in zip(st, chains):
        s = s_ref[b * HEADS + hd]
        x["s"] = s
        ws = _dot(jnp.concatenate([x["uw"][:, HEAD_DIM:].astype(BF16), x["qd"]], axis=0), s.astype(BF16))
        x["vb"] = (x["uw"][:, :HEAD_DIM] - ws[:c]).astype(BF16)
        x["o"] = ws[c:]
    for x, (b, hd) in zip(st, chains):
        lo = hd * HEAD_DIM
        s_ref[b * HEADS + hd] = x["s"] * x["eg"] + _dot_tn(x["kd"], x["vb"])
        o_ref[b, :, lo:lo + HEAD_DIM] = (x["o"] + _dot(x["attn"], x["vb"])).astype(BF16)


def _gdn_chunks(q, k, v, gb, gt):
    nb = GDN_BATCH_PER_STEP
    n_chunks = SEQ // CHUNK
    tok = lambda b, c: (b, c, 0)
    o = pl.pallas_call(
        _gdn_chunk_kernel,
        grid=(BATCH // nb, n_chunks),
        in_specs=[
            pl.BlockSpec((nb, CHUNK, D), tok),
            pl.BlockSpec((nb, CHUNK, D), tok),
            pl.BlockSpec((nb, CHUNK, D), tok),
            pl.BlockSpec((nb, CHUNK, LANES), tok),
            pl.BlockSpec((nb, None, HEADS, LANES), lambda b, c: (b, c, 0, 0)),
        ],
        out_specs=pl.BlockSpec((nb, CHUNK, D), tok),
        out_shape=jax.ShapeDtypeStruct((BATCH, SEQ, D), BF16),
        scratch_shapes=[pltpu.VMEM((nb * HEADS, HEAD_DIM, HEAD_DIM), F32)],
        compiler_params=_cp(("arbitrary", "arbitrary")),
        name="gdn_chunks",
    )(q.reshape(BATCH, SEQ, D), k.reshape(BATCH, SEQ, D), v.reshape(BATCH, SEQ, D),
      gb.reshape(BATCH, SEQ, LANES), gt.reshape(BATCH, n_chunks, HEADS, LANES))
    return o.reshape(N_TOK, D)


def _gdn_post_kernel(o_ref, z_ref, x_ref, mod_ref, nw_ref, gnw_ref, wout_ref, out_ref):
    gnw = gnw_ref[0:1]
    parts = []
    for hd in range(HEADS):
        lo = hd * HEAD_DIM
        oh = o_ref[:, lo:lo + HEAD_DIM].astype(F32)
        zh = z_ref[:, lo:lo + HEAD_DIM].astype(F32)
        parts.append((_rms(oh) * gnw * _silu(zh)).astype(BF16))
    y = _dot(jnp.concatenate(parts, axis=-1), wout_ref[...])
    mod = mod_ref[...]
    out_ref[...] = x_ref[...] + mod[2:3] * (_rms(y) * nw_ref[1:2])


def _gdn_post(o, z, x2, mod_l, nw8, gdn_norm_w, w_out):
    tm = TM_MIX
    tiles_per_seq = SEQ // tm
    gnw8 = jnp.pad(gdn_norm_w.reshape(1, HEAD_DIM), ((0, SUBLANES - 1), (0, 0)))
    tok = lambda i: (i, 0)
    const = lambda i: (0, 0)
    return pl.pallas_call(
        _gdn_post_kernel,
        grid=(N_TOK // tm,),
        in_specs=[
            pl.BlockSpec((tm, D), tok),
            pl.BlockSpec((tm, D), tok),
            pl.BlockSpec((tm, D), tok),
            pl.BlockSpec((None, SUBLANES, D), lambda i: (i // tiles_per_seq, 0, 0)),
            pl.BlockSpec((SUBLANES, D), const),
            pl.BlockSpec((SUBLANES, HEAD_DIM), const),
            pl.BlockSpec((D, D), const),
        ],
        out_specs=pl.BlockSpec((tm, D), tok),
        out_shape=jax.ShapeDtypeStruct((N_TOK, D), F32),
        compiler_params=_cp(("arbitrary",)),
        name="gdn_post",
    )(o, z, x2, mod_l, nw8, gnw8, w_out.astype(BF16))


def kernel(x, c, ada_w, ada_b, norm_w, conv_in_w, conv_w, conv_out_w, gdn_in_w, gdn_conv_w, gdn_a_log,
           gdn_dt_bias, gdn_norm_w, gdn_out_w, moe_group_w, moe_group_b, moe_expert_w, moe_expert_b,
           moe_w_gate, moe_w_up, moe_w_down, shared_w_gate, shared_w_up, shared_w_down, shared_gate_w):
    mod = _ada_mod(c, ada_w, ada_b)
    nw8 = jnp.pad(norm_w, ((0, 0), (0, SUBLANES - norm_w.shape[1]), (0, 0)))
    x2 = x.reshape(N_TOK, D)

    def moe(x1, layer):
        return _hier_moe_block(
            x1, mod[layer], nw8[layer], layer,
            moe_group_w[layer], moe_group_b[layer], moe_expert_w[layer], moe_expert_b[layer],
            moe_w_gate, moe_w_up, moe_w_down,
            shared_w_gate[layer], shared_w_up[layer], shared_w_down[layer], shared_gate_w[layer])

    x2 = _conv_mixer(x2, mod[0], nw8[0], conv_in_w[0], conv_w[0], conv_out_w[0])
    x2 = moe(x2, 0)
    q, k, v, z, gb, gt = _gdn_pre(x2, mod[1], nw8[1], gdn_in_w[0], gdn_conv_w[0], gdn_a_log[0], gdn_dt_bias[0])
    o = _gdn_chunks(q, k, v, gb, gt)
    x2 = _gdn_post(o, z, x2, mod[1], nw8[1], gdn_norm_w[0], gdn_out_w[0])
    x2 = moe(x2, 1)
    return x2.reshape(BATCH, SEQ, D)
```
